```python
import math
import jax, jax.numpy as jnp
from jax import lax
import numpy as np

D_MODEL = 1024
BATCH = 16
SEQ = 2048
DEPTH = 4

GRID_W = 64
CTX_LEN = 256
N_BRANCH = 4
BRANCH_W = D_MODEL // 2
RNN_W = BRANCH_W
RNN_BLOCK = 64
RNN_BLOCKS = RNN_W // RNN_BLOCK
RG_C = 8.0
CONV_W = 4
ATT_DIM = 64
ATT_HEADS = BRANCH_W // (2 * ATT_DIM)
ATT_QK_W = ATT_HEADS * 2 * ATT_DIM
ATT_V_W = ATT_HEADS * 2 * ATT_DIM
ROPE_F = ATT_DIM // 4
ROPE_BASE = 10000.0
Q_BLOCK = 128
POOL_W = BRANCH_W
POOL_WINDOWS = (2, 4, 8, 16)
POOL_GROUPS = len(POOL_WINDOWS)
POOL_GW = POOL_W // POOL_GROUPS
SSM_W = BRANCH_W
SSM_P = 64
SSM_HEADS = SSM_W // SSM_P
SSM_G = 2
SSM_N = 64
SSM_XBC_W = SSM_W + 2 * SSM_G * SSM_N
SSM_CHUNK = 128
N_EXPERTS = 32
TOP_K = 4
D_EXPERT = D_MODEL
SWIGLU_LIMIT = 7.0
SWIGLU_ALPHA = 1.702
EXPERT_BLOCK = 256
DN_ALPHA = (2 * DEPTH) ** 0.25
DN_BETA = (8 * DEPTH) ** -0.25
IN_WIDTHS = (RNN_W, RNN_W, ATT_QK_W, ATT_QK_W, ATT_V_W, POOL_W, SSM_W, SSM_XBC_W, 2 * SSM_HEADS, N_BRANCH * D_MODEL)
IN_W = sum(IN_WIDTHS)
IN_SPLITS = tuple(int(v) for v in np.cumsum(IN_WIDTHS)[:-1])

kernel_name = 'hybrid_diffusion_gated_mixers_moe'


def _normalize(x, eps=1e-6):
    xf = x.astype(jnp.float32)
    mu = jnp.mean(xf, -1, keepdims=True)
    var = jnp.mean(jnp.square(xf - mu), -1, keepdims=True)
    return ((xf - mu) * lax.rsqrt(var + eps)).astype(x.dtype)


def layer_norm(x, g, b):
    return _normalize(x, 1e-5) * g + b


def modulate(x, shift, scale):
    return _normalize(x) * (1.0 + scale) + shift


def centred_dwconv(u, w, b):
    length = u.shape[1]
    left = CONV_W // 2
    up = jnp.pad(u, ((0, 0), (left, CONV_W - 1 - left), (0, 0)))
    out = b + up[:, 0:length] * w[0]
    for k in range(1, CONV_W):
        out = out + up[:, k:k + length] * w[k]
    return out


def linear_recurrence(a, b, h0, reverse):
    def combine(left, right):
        a_l, b_l = left
        a_r, b_r = right
        return a_l * a_r, a_r * b_l + b_r
    a_cum, h = lax.associative_scan(combine, (a, b), reverse=reverse, axis=1)
    h = h + a_cum * h0[:, None]
    final = h[:, 0] if reverse else h[:, -1]
    return h, final


def rglru_coeffs(u, wa, ba, wx, bx, lam):
    bsz, length, width = u.shape
    ub = u.reshape(bsz, length, RNN_BLOCKS, RNN_BLOCK)
    r = jax.nn.sigmoid(jnp.einsum('blhi,hij->blhj', ub, wa).reshape(bsz, length, width) + ba)
    i = jax.nn.sigmoid(jnp.einsum('blhi,hij->blhj', ub, wx).reshape(bsz, length, width) + bx)
    log_a = -RG_C * r * jax.nn.softplus(-lam)
    a = jnp.exp(log_a)
    mult = jnp.sqrt(-jnp.expm1(2.0 * log_a))
    return a, mult * (i * u)


def rglru_branch(u_c, u_l, gate_c, gate_l, conv_w, conv_b, wa, ba, wx, bx, lam, need_ctx):
    dtype = u_l.dtype
    u_c = centred_dwconv(u_c, conv_w, conv_b).astype(jnp.float32)
    u_l = centred_dwconv(u_l, conv_w, conv_b).astype(jnp.float32)
    h0 = jnp.zeros((u_c.shape[0], RNN_W), jnp.float32)
    y_c = jnp.zeros_like(u_c)
    y_l = jnp.zeros_like(u_l)
    for d in range(2):
        rev = d == 1
        a_c, b_c = rglru_coeffs(u_c, wa[d], ba[d], wx[d], bx[d], lam[d])
        a_l, b_l = rglru_coeffs(u_l, wa[d], ba[d], wx[d], bx[d], lam[d])
        h_c, h_fin = linear_recurrence(a_c, b_c, h0, rev)
        h_l, _ = linear_recurrence(a_l, b_l, h_fin, rev)
        y_c = y_c + h_c
        y_l = y_l + h_l
    out_l = (jax.nn.gelu(gate_l.astype(jnp.float32)) * y_l).astype(dtype)
    out_c = (jax.nn.gelu(gate_c.astype(jnp.float32)) * y_c).astype(dtype) if need_ctx else None
    return out_l, out_c


def axial_rope_tables(n_lat):
    rows = n_lat // GRID_W
    row = jnp.repeat(jnp.arange(rows, dtype=jnp.float32), GRID_W)
    col = jnp.tile(jnp.arange(GRID_W, dtype=jnp.float32), rows)
    inv = ROPE_BASE ** (-jnp.arange(ROPE_F, dtype=jnp.float32) / ROPE_F)
    ang = jnp.stack([row[:, None] * inv, col[:, None] * inv], axis=1)
    return jnp.cos(ang), jnp.sin(ang)


def rope_2d(x, cos, sin):
    xs = x.astype(jnp.float32).reshape(*x.shape[:-1], 2, 2, ROPE_F)
    x1, x2 = xs[..., 0, :], xs[..., 1, :]
    cs = cos[None, :, None, None]
    sn = sin[None, :, None, None]
    out = jnp.stack([x1 * cs - x2 * sn, x2 * cs + x1 * sn], axis=-2)
    return out.reshape(x.shape).astype(x.dtype)


def diff_softmax_attend(q, k, v, lam):
    s = jnp.einsum('bqhcd,bkhcd->bhcqk', q, k).astype(jnp.float32) * (ATT_DIM ** -0.5)
    p = jax.nn.softmax(s, axis=-1)
    w = p[:, :, 0] - lam * p[:, :, 1]
    return jnp.einsum('bhqk,bkhe->bqhe', w.astype(v.dtype), v)


def diff_head_norm(o, subln_w, lam_init):
    bsz, length = o.shape[:2]
    of = o.astype(jnp.float32)
    of = of * lax.rsqrt(jnp.mean(jnp.square(of), -1, keepdims=True) + 1e-5) * subln_w * (1.0 - lam_init)
    return of.reshape(bsz, length, ATT_V_W).astype(o.dtype)


def diff_attention_branch(q_c, k_c, v_c, q_l, k_l, v_l, lam_vecs, subln_w, layer_idx, cos, sin, need_ctx):
    bsz, n_lat = q_l.shape[:2]
    n_ctx = k_c.shape[1]
    heads = lambda t: t.reshape(t.shape[0], t.shape[1], ATT_HEADS, 2, ATT_DIM)
    q_l = rope_2d(heads(q_l), cos, sin)
    k_l = rope_2d(heads(k_l), cos, sin)
    k_c = heads(k_c)
    v_l = v_l.reshape(bsz, n_lat, ATT_HEADS, 2 * ATT_DIM)
    v_c = v_c.reshape(bsz, n_ctx, ATT_HEADS, 2 * ATT_DIM)
    lam_init = 0.8 - 0.6 * math.exp(-0.3 * layer_idx)
    lv = lam_vecs.astype(jnp.float32)
    lam = jnp.exp(jnp.sum(lv[0] * lv[1])) - jnp.exp(jnp.sum(lv[2] * lv[3])) + lam_init
    k_all = jnp.concatenate([k_c, k_l], axis=1)
    v_all = jnp.concatenate([v_c, v_l], axis=1)
    n_blk = n_lat // Q_BLOCK
    qb = jnp.swapaxes(q_l.reshape(bsz, n_blk, Q_BLOCK, ATT_HEADS, 2, ATT_DIM), 0, 1)
    ob = lax.map(lambda qq: diff_softmax_attend(qq, k_all, v_all, lam), qb)
    o_l = jnp.swapaxes(ob, 0, 1).reshape(bsz, n_lat, ATT_HEADS, 2 * ATT_DIM)
    out_l = diff_head_norm(o_l, subln_w, lam_init)
    out_c = None
    if need_ctx:
        out_c = diff_head_norm(diff_softmax_attend(heads(q_c), k_c, v_c, lam), subln_w, lam_init)
    return out_l, out_c


def multiscale_pool(u, w, b, scale):
    bsz, length, _ = u.shape
    ug = u.astype(jnp.float32).reshape(bsz, length, POOL_GROUPS, POOL_GW)
    csum = jnp.pad(jnp.cumsum(ug, axis=1), ((0, 0), (1, 0), (0, 0), (0, 0)))
    t = jnp.arange(length)
    pooled = []
    for gi, win in enumerate(POOL_WINDOWS):
        lo = jnp.clip(t - win // 2, 0, length - 1)
        hi = jnp.clip(t + win - 1 - win // 2, 0, length - 1)
        cnt = (hi - lo + 1).astype(jnp.float32)[None, :, None]
        sg = csum[:, :, gi]
        pooled.append((sg[:, hi + 1] - sg[:, lo]) / cnt)
    mix = jnp.stack(pooled, axis=2) - ug
    y = jnp.einsum('blgc,gce->blge', mix, w).reshape(bsz, length, POOL_W) + b
    return (y * scale).astype(u.dtype)


def segsum(a):
    t = a.shape[-1]
    cs = jnp.cumsum(a, axis=-1)
    diff = cs[..., :, None] - cs[..., None, :]
    mask = jnp.tril(jnp.ones((t, t), dtype=bool))
    return jnp.where(mask, diff, -jnp.inf)


def ssd_scan(xdt, adt, bm, cm, h0):
    b, l, h, p = xdt.shape
    g, n = bm.shape[2], bm.shape[3]
    k = h // g
    c = l // SSM_CHUNK
    X = xdt.reshape(b, c, SSM_CHUNK, g, k, p)
    A = adt.reshape(b, c, SSM_CHUNK, g, k).transpose(0, 3, 4, 1, 2)
    Bc = bm.reshape(b, c, SSM_CHUNK, g, n)
    Cc = cm.reshape(b, c, SSM_CHUNK, g, n)
    a_cs = jnp.cumsum(A, axis=-1)
    lmat = jnp.exp(segsum(A))
    cb = jnp.einsum('bclgn,bcsgn->bgcls', Cc, Bc)
    y_diag = jnp.einsum('bgcls,bgkcls,bcsgkp->bclgkp', cb, lmat, X)
    decay_states = jnp.exp(a_cs[..., -1:] - a_cs)
    states = jnp.einsum('bclgn,bgkcl,bclgkp->bcgkpn', Bc, decay_states, X)
    states = jnp.concatenate([h0.reshape(b, g, k, p, n)[:, None], states], axis=1)
    chunk_a = jnp.pad(a_cs[..., -1], ((0, 0), (0, 0), (0, 0), (1, 0)))
    decay_chunk = jnp.exp(segsum(chunk_a))
    new_states = jnp.einsum('bgkzc,bcgkpn->bzgkpn', decay_chunk, states)
    prev_states, final = new_states[:, :-1], new_states[:, -1]
    y_off = jnp.einsum('bclgn,bcgkpn,bgkcl->bclgkp', Cc, prev_states, jnp.exp(a_cs))
    y = (y_diag + y_off).reshape(b, l, h, p)
    return y, final.reshape(b, h, p, n)


def ssd_prep(xbc, conv_w, conv_b):
    b, l, _ = xbc.shape
    u = jax.nn.silu(centred_dwconv(xbc, conv_w, conv_b).astype(jnp.float32))
    xs, bm, cm = jnp.split(u, (SSM_W, SSM_W + SSM_G * SSM_N), axis=-1)
    return (xs.reshape(b, l, SSM_HEADS, SSM_P), bm.reshape(b, l, SSM_G, SSM_N), cm.reshape(b, l, SSM_G, SSM_N))


def ssd_direction(xs, bm, cm, dt_raw, dt_bias, a_log, d_skip, h0, reverse):
    dt = jax.nn.softplus(dt_raw.astype(jnp.float32) + dt_bias)
    a = -jnp.exp(a_log.astype(jnp.float32))
    xdt = xs * dt[..., None]
    adt = dt * a
    if reverse:
        xdt, adt, bm, cm = (jnp.flip(t, axis=1) for t in (xdt, adt, bm, cm))
    y, h_fin = ssd_scan(xdt, adt, bm, cm, h0)
    if reverse:
        y = jnp.flip(y, axis=1)
    return y + d_skip[:, None] * xs, h_fin


def gated_rmsnorm(y, z, w):
    b, l = y.shape[:2]
    g = (y.reshape(b, l, SSM_W) * jax.nn.silu(z.astype(jnp.float32))).reshape(b, l, SSM_G, SSM_W // SSM_G)
    g = g * lax.rsqrt(jnp.mean(jnp.square(g), -1, keepdims=True) + 1e-5)
    return g.reshape(b, l, SSM_W) * w


def ssd_branch(z_c, xbc_c, dt_c, z_l, xbc_l, dt_l, conv_w, conv_b, dt_bias, a_log, d_skip, norm_w, need_ctx):
    dtype = z_l.dtype
    xs_c, b_c, c_c = ssd_prep(xbc_c, conv_w, conv_b)
    xs_l, b_l, c_l = ssd_prep(xbc_l, conv_w, conv_b)
    h0 = jnp.zeros((xs_c.shape[0], SSM_HEADS, SSM_P, SSM_N), jnp.float32)
    y_c = jnp.zeros_like(xs_c)
    y_l = jnp.zeros_like(xs_l)
    for d in range(2):
        sl = slice(d * SSM_HEADS, (d + 1) * SSM_HEADS)
        yc_d, h_fin = ssd_direction(xs_c, b_c, c_c, dt_c[..., sl], dt_bias[d], a_log[d], d_skip[d], h0, d == 1)
        yl_d, _ = ssd_direction(xs_l, b_l, c_l, dt_l[..., sl], dt_bias[d], a_log[d], d_skip[d], h_fin, d == 1)
        y_c = y_c + yc_d
        y_l = y_l + yl_d
    out_l = gated_rmsnorm(y_l, z_l, norm_w).astype(dtype)
    out_c = gated_rmsnorm(y_c, z_c, norm_w).astype(dtype) if need_ctx else None
    return out_l, out_c


def merge_branches(ys, gates_raw, w_branch, w_out):
    b, l, _ = gates_raw.shape
    g = jax.nn.sigmoid(gates_raw.astype(jnp.float32)).reshape(b, l, N_BRANCH, D_MODEL).astype(gates_raw.dtype)
    m = g[:, :, 0] * (ys[0] @ w_branch[0])
    for i in range(1, N_BRANCH):
        m = m + g[:, :, i] * (ys[i] @ w_branch[i])
    return m @ w_out


def moe_ffn(h, router_w, router_b, w_gate_up, b_gate_up, w_down, b_down):
    n, d = h.shape
    logits = (h @ router_w + router_b).astype(jnp.float32)
    top_logit, top_e = lax.top_k(logits, TOP_K)
    probs = jax.nn.softmax(top_logit, axis=-1)
    a = n * TOP_K
    flat_e = top_e.reshape(a)
    flat_tok = jnp.arange(a, dtype=jnp.int32) // TOP_K
    flat_p = probs.reshape(a)
    order = jnp.argsort(flat_e)
    se = flat_e[order]
    counts = jnp.bincount(flat_e, length=N_EXPERTS)
    pcounts = (counts + EXPERT_BLOCK - 1) // EXPERT_BLOCK * EXPERT_BLOCK
    start = jnp.cumsum(counts) - counts
    pend = jnp.cumsum(pcounts)
    pstart = pend - pcounts
    dest = pstart[se] + jnp.arange(a, dtype=jnp.int32) - start[se]
    n_blocks = -(-a // EXPERT_BLOCK) + N_EXPERTS
    slots = n_blocks * EXPERT_BLOCK
    slot_tok = jnp.full((slots,), n, jnp.int32).at[dest].set(flat_tok[order])
    slot_p = jnp.zeros((slots,), jnp.float32).at[dest].set(flat_p[order])
    block_e = jnp.minimum(jnp.searchsorted(pend // EXPERT_BLOCK, jnp.arange(n_blocks), side='right'), N_EXPERTS - 1)
    h_pad = jnp.concatenate([h, jnp.zeros((1, d), h.dtype)], axis=0)
    xb = h_pad[slot_tok].reshape(n_blocks, EXPERT_BLOCK, d)

    def expert_block(args):
        xe, e = args
        gu = xe @ w_gate_up[e] + b_gate_up[e]
        gate, up = gu[:, :D_EXPERT], gu[:, D_EXPERT:]
        gate = jnp.minimum(gate, SWIGLU_LIMIT)
        up = jnp.clip(up, -SWIGLU_LIMIT, SWIGLU_LIMIT)
        glu = gate * jax.nn.sigmoid(gate * SWIGLU_ALPHA)
        return ((up + 1.0) * glu) @ w_down[e] + b_down[e]

    yb = lax.map(expert_block, (xb, block_e)).reshape(slots, d)
    out = jnp.zeros((n + 1, d), yb.dtype).at[slot_tok].add(yb * slot_p[:, None].astype(yb.dtype))
    return out[:n]


def setup_inputs(seed: int = 0) -> dict:
    key = jax.random.key(seed)
    it = iter(jax.random.split(key, 64))
    nrm = lambda shape, scale: jax.random.normal(next(it), shape, jnp.float32) * scale
    D, L = D_MODEL, DEPTH
    a0 = jax.random.uniform(next(it), (L, 2, RNN_W), jnp.float32, 0.9, 0.999)
    s0 = a0 ** (1.0 / RG_C)
    dt0 = jnp.exp(jax.random.uniform(next(it), (L, 2, SSM_HEADS), jnp.float32, math.log(1e-3), math.log(1e-1)))
    return {
        'x': nrm((BATCH, SEQ, D), 1.0),
        'c': nrm((BATCH, D), 1.0),
        'ctx': nrm((BATCH, CTX_LEN, D), 1.0),
        'c_ctx': nrm((D,), 1.0),
        'w_ada': nrm((L, D, 6 * D), 0.5 * D ** -0.5),
        'b_ada': nrm((L, 6 * D), 0.02),
        'w_in': nrm((L, D, IN_W), D ** -0.5),
        'rnn_conv_w': nrm((L, CONV_W, RNN_W), CONV_W ** -0.5),
        'rnn_conv_b': nrm((L, RNN_W), 0.02),
        'rnn_wa': nrm((L, 2, RNN_BLOCKS, RNN_BLOCK, RNN_BLOCK), RNN_BLOCK ** -0.5),
        'rnn_ba': nrm((L, 2, RNN_W), 0.02),
        'rnn_wx': nrm((L, 2, RNN_BLOCKS, RNN_BLOCK, RNN_BLOCK), RNN_BLOCK ** -0.5),
        'rnn_bx': nrm((L, 2, RNN_W), 0.02),
        'rnn_lam': jnp.log(s0) - jnp.log1p(-s0),
        'att_lambda': nrm((L, 4, ATT_DIM), 0.1),
        'att_subln': 1.0 + nrm((L, 2 * ATT_DIM), 0.02),
        'pool_w': nrm((L, POOL_GROUPS, POOL_GW, POOL_GW), POOL_GW ** -0.5),
        'pool_b': nrm((L, POOL_W), 0.02),
        'pool_scale': 1.0 + nrm((L, POOL_W), 0.02),
        'ssm_conv_w': nrm((L, CONV_W, SSM_XBC_W), CONV_W ** -0.5),
        'ssm_conv_b': nrm((L, SSM_XBC_W), 0.02),
        'ssm_dt_bias': dt0 + jnp.log(-jnp.expm1(-dt0)),
        'ssm_a_log': jnp.log(jax.random.uniform(next(it), (L, 2, SSM_HEADS), jnp.float32, 1.0, 16.0)),
        'ssm_d': 1.0 + nrm((L, 2, SSM_HEADS), 0.02),
        'ssm_norm': 1.0 + nrm((L, SSM_W), 0.02),
        'w_branch': nrm((L, N_BRANCH, BRANCH_W, D), BRANCH_W ** -0.5),
        'w_out': nrm((L, D, D), DN_BETA * D ** -0.5),
        'ln1_g': 1.0 + nrm((L, D), 0.02),
        'ln1_b': nrm((L, D), 0.02),
        'ln2_g': 1.0 + nrm((L, D), 0.02),
        'ln2_b': nrm((L, D), 0.02),
        'router_w': nrm((L, D, N_EXPERTS), D ** -0.5),
        'router_b': nrm((L, N_EXPERTS), 0.01),
        'w_gate_up': nrm((L, N_EXPERTS, D, 2 * D_EXPERT), D ** -0.5),
        'b_gate_up': nrm((L, N_EXPERTS, 2 * D_EXPERT), 0.02),
        'w_down': nrm((L, N_EXPERTS, D_EXPERT, D), DN_BETA * D_EXPERT ** -0.5),
        'b_down': nrm((L, N_EXPERTS, D), 0.02),
    }


def reference(x, c, ctx, c_ctx, w_ada, b_ada, w_in, rnn_conv_w, rnn_conv_b, rnn_wa, rnn_ba, rnn_wx, rnn_bx, rnn_lam, att_lambda, att_subln, pool_w, pool_b, pool_scale, ssm_conv_w, ssm_conv_b, ssm_dt_bias, ssm_a_log, ssm_d, ssm_norm, w_branch, w_out, ln1_g, ln1_b, ln2_g, ln2_b, router_w, router_b, w_gate_up, b_gate_up, w_down, b_down):
    n_lat = x.shape[1]
    cos, sin = axial_rope_tables(n_lat)
    xl, xc = x, ctx
    for li in range(DEPTH):
        need_ctx = li < DEPTH - 1
        mod_l = (jax.nn.silu(c) @ w_ada[li] + b_ada[li])[:, None, :]
        mod_c = (jax.nn.silu(c_ctx) @ w_ada[li] + b_ada[li])[None, None, :]
        sh1, sc1, g1, sh2, sc2, g2 = jnp.split(mod_l, 6, axis=-1)
        csh1, csc1, cg1, csh2, csc2, cg2 = jnp.split(mod_c, 6, axis=-1)

        hl = modulate(xl, sh1, sc1)
        hc = modulate(xc, csh1, csc1)
        (rx_l, rg_l, q_l, k_l, v_l, pu_l, z_l, xbc_l, dt_l, mg_l) = jnp.split(hl @ w_in[li], IN_SPLITS, axis=-1)
        (rx_c, rg_c, q_c, k_c, v_c, pu_c, z_c, xbc_c, dt_c, mg_c) = jnp.split(hc @ w_in[li], IN_SPLITS, axis=-1)
        rnn_l, rnn_c = rglru_branch(rx_c, rx_l, rg_c, rg_l, rnn_conv_w[li], rnn_conv_b[li], rnn_wa[li], rnn_ba[li], rnn_wx[li], rnn_bx[li], rnn_lam[li], need_ctx)
        att_l, att_c = diff_attention_branch(q_c, k_c, v_c, q_l, k_l, v_l, att_lambda[li], att_subln[li], li, cos, sin, need_ctx)
        pool_l = multiscale_pool(pu_l, pool_w[li], pool_b[li], pool_scale[li])
        ssm_l, ssm_c = ssd_branch(z_c, xbc_c, dt_c, z_l, xbc_l, dt_l, ssm_conv_w[li], ssm_conv_b[li], ssm_dt_bias[li], ssm_a_log[li], ssm_d[li], ssm_norm[li], need_ctx)
        y_l = merge_branches((rnn_l, att_l, pool_l, ssm_l), mg_l, w_branch[li], w_out[li])
        if need_ctx:
            pool_c = multiscale_pool(pu_c, pool_w[li], pool_b[li], pool_scale[li])
            y_c = merge_branches((rnn_c, att_c, pool_c, ssm_c), mg_c, w_branch[li], w_out[li])
            xc = layer_norm(DN_ALPHA * xc + cg1 * y_c, ln1_g[li], ln1_b[li])
        xl = layer_norm(DN_ALPHA * xl + g1 * y_l, ln1_g[li], ln1_b[li])

        hl2 = modulate(xl, sh2, sc2).reshape(-1, D_MODEL)
        if need_ctx:
            hc2 = modulate(xc, csh2, csc2).reshape(-1, D_MODEL)
            f = moe_ffn(jnp.concatenate([hc2, hl2], axis=0), router_w[li], router_b[li], w_gate_up[li], b_gate_up[li], w_down[li], b_down[li])
            f_c, f_l = f[:hc2.shape[0]], f[hc2.shape[0]:]
            xc = layer_norm(DN_ALPHA * xc + cg2 * f_c.reshape(xc.shape), ln2_g[li], ln2_b[li])
        else:
            f_l = moe_ffn(hl2, router_w[li], router_b[li], w_gate_up[li], b_gate_up[li], w_down[li], b_down[li])
        xl = layer_norm(DN_ALPHA * xl + g2 * f_l.reshape(xl.shape), ln2_g[li], ln2_b[li])
    return xl
```

```python
import functools
import math

import jax
import jax.numpy as jnp
import numpy as np
from jax import lax
from jax.experimental import pallas as pl
from jax.experimental.pallas import tpu as pltpu

D_MODEL = 1024
DEPTH = 4
GRID_W = 64
N_BRANCH = 4
BRANCH_W = D_MODEL // 2
RNN_W = BRANCH_W
RNN_BLOCK = 64
RNN_BLOCKS = RNN_W // RNN_BLOCK
RG_C = 8.0
CONV_W = 4
ATT_DIM = 64
ATT_HEADS = BRANCH_W // (2 * ATT_DIM)
ATT_QK_W = ATT_HEADS * 2 * ATT_DIM
ATT_V_W = ATT_HEADS * 2 * ATT_DIM
ROPE_F = ATT_DIM // 4
ROPE_BASE = 10000.0
POOL_W = BRANCH_W
POOL_WINDOWS = (2, 4, 8, 16)
POOL_GROUPS = len(POOL_WINDOWS)
POOL_GW = POOL_W // POOL_GROUPS
SSM_W = BRANCH_W
SSM_P = 64
SSM_HEADS = SSM_W // SSM_P
SSM_G = 2
SSM_N = 64
SSM_XBC_W = SSM_W + 2 * SSM_G * SSM_N
SSM_CHUNK = 128
N_EXPERTS = 32
TOP_K = 4
D_EXPERT = D_MODEL
SWIGLU_LIMIT = 7.0
SWIGLU_ALPHA = 1.702
EXPERT_BLOCK = 256
DN_ALPHA = (2 * DEPTH) ** 0.25
IN_WIDTHS = (RNN_W, RNN_W, ATT_QK_W, ATT_QK_W, ATT_V_W, POOL_W, SSM_W, SSM_XBC_W, 2 * SSM_HEADS, N_BRANCH * D_MODEL)
IN_W = sum(IN_WIDTHS)
IN_SPLITS = tuple(int(v) for v in np.cumsum(IN_WIDTHS)[:-1])

LANES = 128
VMEM_LIMIT = 48 * 1024 * 1024

BF16 = jnp.bfloat16
F32 = jnp.float32


def _round_up(v, m):
    return (v + m - 1) // m * m


def _mod_proj_kernel(x_ref, sh_ref, sc_ref, w_ref, o_ref, h_ref):
    @pl.when(pl.program_id(2) == 0)
    def _():
        x = x_ref[0]
        mu = jnp.mean(x, -1, keepdims=True)
        xc = x - mu
        var = jnp.mean(xc * xc, -1, keepdims=True)
        h = xc * lax.rsqrt(var + 1e-6) * (1.0 + sc_ref[0]) + sh_ref[0]
        h_ref[...] = h.astype(BF16)

    o_ref[0] = jnp.dot(h_ref[...], w_ref[...], preferred_element_type=F32)


def mod_proj(x, shift, scale, w, tm, tn):
    b, t, d = x.shape
    n = w.shape[1]
    return pl.pallas_call(
        _mod_proj_kernel,
        out_shape=jax.ShapeDtypeStruct((b, t, n), F32),
        grid=(b, t // tm, n // tn),
        in_specs=[
            pl.BlockSpec((1, tm, d), lambda bi, i, j: (bi, i, 0)),
            pl.BlockSpec((1, 1, d), lambda bi, i, j: (bi, 0, 0)),
            pl.BlockSpec((1, 1, d), lambda bi, i, j: (bi, 0, 0)),
            pl.BlockSpec((d, tn), lambda bi, i, j: (0, j)),
        ],
        out_specs=pl.BlockSpec((1, tm, tn), lambda bi, i, j: (bi, i, j)),
        scratch_shapes=[pltpu.VMEM((tm, d), BF16)],
        compiler_params=pltpu.CompilerParams(
            dimension_semantics=("parallel", "parallel", "arbitrary"), vmem_limit_bytes=VMEM_LIMIT),
        name="mod_proj",
    )(x, shift, scale, w)


def _proj_kernel(x_ref, w_ref, o_ref):
    o_ref[...] = jnp.dot(x_ref[...].astype(BF16), w_ref[...], preferred_element_type=F32)


def proj(x, w, tm):
    m, k = x.shape
    n = w.shape[1]
    return pl.pallas_call(
        _proj_kernel,
        out_shape=jax.ShapeDtypeStruct((m, n), F32),
        grid=(m // tm,),
        in_specs=[pl.BlockSpec((tm, k), lambda i: (i, 0)), pl.BlockSpec((k, n), lambda i: (0, 0))],
        out_specs=pl.BlockSpec((tm, n), lambda i: (i, 0)),
        compiler_params=pltpu.CompilerParams(dimension_semantics=("parallel",), vmem_limit_bytes=VMEM_LIMIT),
        name="proj",
    )(x, w)


def _diff_attn_kernel(par_ref, q_ref, k_ref, v_ref, g_ref, o_ref):
    lam = par_ref[0]
    out_scale = par_ref[1]
    q = q_ref[0]
    k = k_ref[0]
    lane = lax.broadcasted_iota(jnp.int32, q.shape, 1)
    dims = (((1,), (1,)), ((), ()))
    q1 = jnp.where(lane < ATT_DIM, q, jnp.zeros_like(q))
    q2 = jnp.where(lane >= ATT_DIM, q, jnp.zeros_like(q))

    def softmax(qh):
        s = lax.dot_general(qh, k, dims, preferred_element_type=F32) * (ATT_DIM ** -0.5)
        e = jnp.exp(s - jnp.max(s, -1, keepdims=True))
        return e / jnp.sum(e, -1, keepdims=True)

    w = softmax(q1) - lam * softmax(q2)
    o = jnp.dot(w.astype(BF16), v_ref[0], preferred_element_type=F32)
    o = o * lax.rsqrt(jnp.mean(o * o, -1, keepdims=True) + 1e-5) * g_ref[...] * out_scale
    o_ref[0] = o


def diff_attention(q, k, v, subln, lam, out_scale, tq):
    b, nq, _ = q.shape
    nk = k.shape[1]
    hw = 2 * ATT_DIM
    par = jnp.stack([lam, out_scale]).astype(F32)
    return pl.pallas_call(
        _diff_attn_kernel,
        out_shape=jax.ShapeDtypeStruct((b, nq, ATT_V_W), F32),
        grid=(b, ATT_HEADS, nq // tq),
        in_specs=[
            pl.BlockSpec(memory_space=pltpu.SMEM),
            pl.BlockSpec((1, tq, hw), lambda bi, h, i: (bi, i, h)),
            pl.BlockSpec((1, nk, hw), lambda bi, h, i: (bi, 0, h)),
            pl.BlockSpec((1, nk, hw), lambda bi, h, i: (bi, 0, h)),
            pl.BlockSpec((1, hw), lambda bi, h, i: (0, 0)),
        ],
        out_specs=pl.BlockSpec((1, tq, hw), lambda bi, h, i: (bi, i, h)),
        compiler_params=pltpu.CompilerParams(
            dimension_semantics=("parallel", "parallel", "parallel"), vmem_limit_bytes=VMEM_LIMIT),
        name="diff_attention",
    )(par, q, k, v, subln.reshape(1, hw).astype(F32))


def _expert_kernel(be_ref, x_ref, p_ref, wgu_ref, bgu_ref, wd_ref, bd_ref, o_ref):
    del be_ref
    gu = jnp.dot(x_ref[...], wgu_ref[0], preferred_element_type=F32) + bgu_ref[0]
    gate = jnp.minimum(gu[:, :D_EXPERT], SWIGLU_LIMIT)
    up = jnp.clip(gu[:, D_EXPERT:], -SWIGLU_LIMIT, SWIGLU_LIMIT)
    glu = gate * jax.nn.sigmoid(gate * SWIGLU_ALPHA)
    act = ((up + 1.0) * glu).astype(BF16)
    y = jnp.dot(act, wd_ref[0], preferred_element_type=F32) + bd_ref[0]
    o_ref[...] = y * p_ref[...]


def expert_blocks(block_e, xb, slot_p, wgu, bgu, wd, bd):
    slots, d = xb.shape
    n_blocks = slots // EXPERT_BLOCK
    grid_spec = pltpu.PrefetchScalarGridSpec(
        num_scalar_prefetch=1,
        grid=(n_blocks,),
        in_specs=[
            pl.BlockSpec((EXPERT_BLOCK, d), lambda i, be: (i, 0)),
            pl.BlockSpec((EXPERT_BLOCK, 1), lambda i, be: (i, 0)),
            pl.BlockSpec((1, d, 2 * D_EXPERT), lambda i, be: (be[i], 0, 0)),
            pl.BlockSpec((1, 1, 2 * D_EXPERT), lambda i, be: (be[i], 0, 0)),
            pl.BlockSpec((1, D_EXPERT, d), lambda i, be: (be[i], 0, 0)),
            pl.BlockSpec((1, 1, d), lambda i, be: (be[i], 0, 0)),
        ],
        out_specs=pl.BlockSpec((EXPERT_BLOCK, d), lambda i, be: (i, 0)),
    )
    return pl.pallas_call(
        _expert_kernel,
        out_shape=jax.ShapeDtypeStruct((slots, d), F32),
        grid_spec=grid_spec,
        compiler_params=pltpu.CompilerParams(dimension_semantics=("arbitrary",), vmem_limit_bytes=VMEM_LIMIT),
        name="expert_blocks",
    )(block_e, xb, slot_p.reshape(slots, 1), wgu, bgu.reshape(N_EXPERTS, 1, -1), wd, bd.reshape(N_EXPERTS, 1, -1))


def _normalize(x, eps=1e-6):
    mu = jnp.mean(x, -1, keepdims=True)
    var = jnp.mean(jnp.square(x - mu), -1, keepdims=True)
    return (x - mu) * lax.rsqrt(var + eps)


def layer_norm(x, g, b):
    return _normalize(x, 1e-5) * g + b


def modulate(x, shift, scale):
    return _normalize(x) * (1.0 + scale) + shift


def centred_dwconv(u, w, b):
    length = u.shape[1]
    left = CONV_W // 2
    up = jnp.pad(u, ((0, 0), (left, CONV_W - 1 - left), (0, 0)))
    out = b + up[:, 0:length] * w[0]
    for k in range(1, CONV_W):
        out = out + up[:, k:k + length] * w[k]
    return out


def linear_recurrence(a, b, h0, reverse):
    def combine(left, right):
        a_l, b_l = left
        a_r, b_r = right
        return a_l * a_r, a_r * b_l + b_r
    a_cum, h = lax.associative_scan(combine, (a, b), reverse=reverse, axis=1)
    h = h + a_cum * h0[:, None]
    final = h[:, 0] if reverse else h[:, -1]
    return h, final


def rglru_coeffs(u, wa, ba, wx, bx, lam):
    bsz, length, width = u.shape
    ub = u.reshape(bsz, length, RNN_BLOCKS, RNN_BLOCK)
    r = jax.nn.sigmoid(jnp.einsum('blhi,hij->blhj', ub, wa).reshape(bsz, length, width) + ba)
    i = jax.nn.sigmoid(jnp.einsum('blhi,hij->blhj', ub, wx).reshape(bsz, length, width) + bx)
    log_a = -RG_C * r * jax.nn.softplus(-lam)
    a = jnp.exp(log_a)
    mult = jnp.sqrt(-jnp.expm1(2.0 * log_a))
    return a, mult * (i * u)


def rglru_branch(u_c, u_l, gate_c, gate_l, conv_w, conv_b, wa, ba, wx, bx, lam, need_ctx):
    u_c = centred_dwconv(u_c, conv_w, conv_b)
    u_l = centred_dwconv(u_l, conv_w, conv_b)
    h0 = jnp.zeros((u_c.shape[0], RNN_W), F32)
    y_c = jnp.zeros_like(u_c)
    y_l = jnp.zeros_like(u_l)
    for d in range(2):
        rev = d == 1
        a_c, b_c = rglru_coeffs(u_c, wa[d], ba[d], wx[d], bx[d], lam[d])
        a_l, b_l = rglru_coeffs(u_l, wa[d], ba[d], wx[d], bx[d], lam[d])
        h_c, h_fin = linear_recurrence(a_c, b_c, h0, rev)
        h_l, _ = linear_recurrence(a_l, b_l, h_fin, rev)
        y_c = y_c + h_c
        y_l = y_l + h_l
    out_l = jax.nn.gelu(gate_l) * y_l
    out_c = jax.nn.gelu(gate_c) * y_c if need_ctx else None
    return out_l, out_c


def axial_rope_tables(n_lat):
    rows = n_lat // GRID_W
    row = jnp.repeat(jnp.arange(rows, dtype=F32), GRID_W)
    col = jnp.tile(jnp.arange(GRID_W, dtype=F32), rows)
    inv = ROPE_BASE ** (-jnp.arange(ROPE_F, dtype=F32) / ROPE_F)
    ang = jnp.stack([row[:, None] * inv, col[:, None] * inv], axis=1)
    return jnp.cos(ang), jnp.sin(ang)


def rope_2d(x, cos, sin):
    xs = x.reshape(*x.shape[:-1], 2, 2, ROPE_F)
    x1, x2 = xs[..., 0, :], xs[..., 1, :]
    cs = cos[None, :, None, None]
    sn = sin[None, :, None, None]
    out = jnp.stack([x1 * cs - x2 * sn, x2 * cs + x1 * sn], axis=-2)
    return out.reshape(x.shape)


def diff_attention_branch(q_c, k_c, v_c, q_l, k_l, v_l, lam_vecs, subln_w, layer_idx, cos, sin, need_ctx):
    bsz, n_lat = q_l.shape[:2]
    heads = lambda t: t.reshape(t.shape[0], t.shape[1], ATT_HEADS, 2, ATT_DIM)
    q_l = rope_2d(heads(q_l), cos, sin).reshape(bsz, n_lat, ATT_QK_W)
    k_l = rope_2d(heads(k_l), cos, sin).reshape(bsz, n_lat, ATT_QK_W)
    lam_init = 0.8 - 0.6 * math.exp(-0.3 * layer_idx)
    lam = jnp.exp(jnp.sum(lam_vecs[0] * lam_vecs[1])) - jnp.exp(jnp.sum(lam_vecs[2] * lam_vecs[3])) + lam_init
    out_scale = jnp.asarray(1.0 - lam_init, F32)
    k_all = jnp.concatenate([k_c, k_l], axis=1).astype(BF16)
    v_all = jnp.concatenate([v_c, v_l], axis=1).astype(BF16)
    out_l = diff_attention(q_l.astype(BF16), k_all, v_all, subln_w, lam, out_scale, 256)
    out_c = None
    if need_ctx:
        out_c = diff_attention(q_c.astype(BF16), k_c.astype(BF16), v_c.astype(BF16), subln_w, lam, out_scale, 256)
    return out_l, out_c


def multiscale_pool(u, w, b, scale):
    bsz, length, _ = u.shape
    ug = u.reshape(bsz, length, POOL_GROUPS, POOL_GW)
    csum = jnp.pad(jnp.cumsum(ug, axis=1), ((0, 0), (1, 0), (0, 0), (0, 0)))
    t = jnp.arange(length)
    pooled = []
    for gi, win in enumerate(POOL_WINDOWS):
        lo = jnp.clip(t - win // 2, 0, length - 1)
        hi = jnp.clip(t + win - 1 - win // 2, 0, length - 1)
        cnt = (hi - lo + 1).astype(F32)[None, :, None]
        sg = csum[:, :, gi]
        pooled.append((sg[:, hi + 1] - sg[:, lo]) / cnt)
    mix = jnp.stack(pooled, axis=2) - ug
    y = jnp.einsum('blgc,gce->blge', mix, w).reshape(bsz, length, POOL_W) + b
    return y * scale


def segsum(a):
    t = a.shape[-1]
    cs = jnp.cumsum(a, axis=-1)
    diff = cs[..., :, None] - cs[..., None, :]
    mask = jnp.tril(jnp.ones((t, t), dtype=bool))
    return jnp.where(mask, diff, -jnp.inf)


def ssd_scan(xdt, adt, bm, cm, h0):
    b, l, h, p = xdt.shape
    g, n = bm.shape[2], bm.shape[3]
    k = h // g
    c = l // SSM_CHUNK
    X = xdt.reshape(b, c, SSM_CHUNK, g, k, p)
    A = adt.reshape(b, c, SSM_CHUNK, g, k).transpose(0, 3, 4, 1, 2)
    Bc = bm.reshape(b, c, SSM_CHUNK, g, n)
    Cc = cm.reshape(b, c, SSM_CHUNK, g, n)
    a_cs = jnp.cumsum(A, axis=-1)
    lmat = jnp.exp(segsum(A))
    cb = jnp.einsum('bclgn,bcsgn->bgcls', Cc, Bc)
    y_diag = jnp.einsum('bgcls,bgkcls,bcsgkp->bclgkp', cb, lmat, X)
    decay_states = jnp.exp(a_cs[..., -1:] - a_cs)
    states = jnp.einsum('bclgn,bgkcl,bclgkp->bcgkpn', Bc, decay_states, X)
    states = jnp.concatenate([h0.reshape(b, g, k, p, n)[:, None], states], axis=1)
    chunk_a = jnp.pad(a_cs[..., -1], ((0, 0), (0, 0), (0, 0), (1, 0)))
    decay_chunk = jnp.exp(segsum(chunk_a))
    new_states = jnp.einsum('bgkzc,bcgkpn->bzgkpn', decay_chunk, states)
    prev_states, final = new_states[:, :-1], new_states[:, -1]
    y_off = jnp.einsum('bclgn,bcgkpn,bgkcl->bclgkp', Cc, prev_states, jnp.exp(a_cs))
    y = (y_diag + y_off).reshape(b, l, h, p)
    return y, final.reshape(b, h, p, n)


def ssd_prep(xbc, conv_w, conv_b):
    b, l, _ = xbc.shape
    u = jax.nn.silu(centred_dwconv(xbc, conv_w, conv_b))
    xs, bm, cm = jnp.split(u, (SSM_W, SSM_W + SSM_G * SSM_N), axis=-1)
    return (xs.reshape(b, l, SSM_HEADS, SSM_P), bm.reshape(b, l, SSM_G, SSM_N), cm.reshape(b, l, SSM_G, SSM_N))


def ssd_direction(xs, bm, cm, dt_raw, dt_bias, a_log, d_skip, h0, reverse):
    dt = jax.nn.softplus(dt_raw + dt_bias)
    a = -jnp.exp(a_log)
    xdt = xs * dt[..., None]
    adt = dt * a
    if reverse:
        xdt, adt, bm, cm = (jnp.flip(t, axis=1) for t in (xdt, adt, bm, cm))
    y, h_fin = ssd_scan(xdt, adt, bm, cm, h0)
    if reverse:
        y = jnp.flip(y, axis=1)
    return y + d_skip[:, None] * xs, h_fin


def gated_rmsnorm(y, z, w):
    b, l = y.shape[:2]
    g = (y.reshape(b, l, SSM_W) * jax.nn.silu(z)).reshape(b, l, SSM_G, SSM_W // SSM_G)
    g = g * lax.rsqrt(jnp.mean(jnp.square(g), -1, keepdims=True) + 1e-5)
    return g.reshape(b, l, SSM_W) * w


def ssd_branch(z_c, xbc_c, dt_c, z_l, xbc_l, dt_l, conv_w, conv_b, dt_bias, a_log, d_skip, norm_w, need_ctx):
    xs_c, b_c, c_c = ssd_prep(xbc_c, conv_w, conv_b)
    xs_l, b_l, c_l = ssd_prep(xbc_l, conv_w, conv_b)
    h0 = jnp.zeros((xs_c.shape[0], SSM_HEADS, SSM_P, SSM_N), F32)
    y_c = jnp.zeros_like(xs_c)
    y_l = jnp.zeros_like(xs_l)
    for d in range(2):
        sl = slice(d * SSM_HEADS, (d + 1) * SSM_HEADS)
        yc_d, h_fin = ssd_direction(xs_c, b_c, c_c, dt_c[..., sl], dt_bias[d], a_log[d], d_skip[d], h0, d == 1)
        yl_d, _ = ssd_direction(xs_l, b_l, c_l, dt_l[..., sl], dt_bias[d], a_log[d], d_skip[d], h_fin, d == 1)
        y_c = y_c + yc_d
        y_l = y_l + yl_d
    out_l = gated_rmsnorm(y_l, z_l, norm_w)
    out_c = gated_rmsnorm(y_c, z_c, norm_w) if need_ctx else None
    return out_l, out_c


def merge_branches(ys, gates_raw, w_branch, w_out):
    b, l, _ = gates_raw.shape
    g = jax.nn.sigmoid(gates_raw).reshape(b, l, N_BRANCH, D_MODEL)
    tm = 512 if (b * l) % 512 == 0 else 256
    m = g[:, :, 0] * proj(ys[0].reshape(b * l, BRANCH_W), w_branch[0], tm).reshape(b, l, D_MODEL)
    for i in range(1, N_BRANCH):
        m = m + g[:, :, i] * proj(ys[i].reshape(b * l, BRANCH_W), w_branch[i], tm).reshape(b, l, D_MODEL)
    return proj(m.reshape(b * l, D_MODEL), w_out, tm).reshape(b, l, D_MODEL)


def moe_ffn(h, router_w, router_b, wgu, bgu, wd, bd):
    n, d = h.shape
    logits = h @ router_w + router_b
    top_logit, top_e = lax.top_k(logits, TOP_K)
    probs = jax.nn.softmax(top_logit, axis=-1)
    a = n * TOP_K
    flat_e = top_e.reshape(a)
    flat_tok = jnp.arange(a, dtype=jnp.int32) // TOP_K
    flat_p = probs.reshape(a)
    order = jnp.argsort(flat_e)
    se = flat_e[order]
    counts = jnp.bincount(flat_e, length=N_EXPERTS)
    pcounts = (counts + EXPERT_BLOCK - 1) // EXPERT_BLOCK * EXPERT_BLOCK
    start = jnp.cumsum(counts) - counts
    pend = jnp.cumsum(pcounts)
    pstart = pend - pcounts
    dest = pstart[se] + jnp.arange(a, dtype=jnp.int32) - start[se]
    n_blocks = -(-a // EXPERT_BLOCK) + N_EXPERTS
    slots = n_blocks * EXPERT_BLOCK
    slot_tok = jnp.full((slots,), n, jnp.int32).at[dest].set(flat_tok[order])
    slot_p = jnp.zeros((slots,), F32).at[dest].set(flat_p[order])
    block_e = jnp.minimum(jnp.searchsorted(pend // EXPERT_BLOCK, jnp.arange(n_blocks), side='right'), N_EXPERTS - 1)
    h_pad = jnp.concatenate([h, jnp.zeros((1, d), h.dtype)], axis=0).astype(BF16)
    xb = h_pad[slot_tok]
    yb = expert_blocks(block_e.astype(jnp.int32), xb, slot_p, wgu, bgu, wd, bd)
    out = jnp.zeros((n + 1, d), F32).at[slot_tok].add(yb)
    return out[:n]


def kernel(x, c, ctx, c_ctx, w_ada, b_ada, w_in, rnn_conv_w, rnn_conv_b, rnn_wa, rnn_ba, rnn_wx, rnn_bx, rnn_lam, att_lambda, att_subln, pool_w, pool_b, pool_scale, ssm_conv_w, ssm_conv_b, ssm_dt_bias, ssm_a_log, ssm_d, ssm_norm, w_branch, w_out, ln1_g, ln1_b, ln2_g, ln2_b, router_w, router_b, w_gate_up, b_gate_up, w_down, b_down):
    bsz, n_lat, _ = x.shape
    n_ctx = ctx.shape[1]
    cos, sin = axial_rope_tables(n_lat)
    in_pad = _round_up(IN_W, 4 * 17 * LANES)
    w_in_b = jnp.pad(w_in, ((0, 0), (0, 0), (0, in_pad - IN_W))).astype(BF16)
    w_branch_b = w_branch.astype(BF16)
    w_out_b = w_out.astype(BF16)
    wgu_b = w_gate_up.astype(BF16)
    wd_b = w_down.astype(BF16)
    xl, xc = x, ctx
    for li in range(DEPTH):
        need_ctx = li < DEPTH - 1
        mod_l = (jax.nn.silu(c) @ w_ada[li] + b_ada[li])[:, None, :]
        mod_c = (jax.nn.silu(c_ctx) @ w_ada[li] + b_ada[li])[None, None, :]
        sh1, sc1, g1, sh2, sc2, g2 = jnp.split(mod_l, 6, axis=-1)
        csh1, csc1, cg1, csh2, csc2, cg2 = jnp.split(mod_c, 6, axis=-1)

        bc = lambda t: jnp.broadcast_to(t, (bsz, 1, D_MODEL))
        pl_ = mod_proj(xl, sh1, sc1, w_in_b[li], 512, in_pad // 4)[..., :IN_W]
        pc_ = mod_proj(xc, bc(csh1), bc(csc1), w_in_b[li], 256, in_pad // 4)[..., :IN_W]
        (rx_l, rg_l, q_l, k_l, v_l, pu_l, z_l, xbc_l, dt_l, mg_l) = jnp.split(pl_, IN_SPLITS, axis=-1)
        (rx_c, rg_c, q_c, k_c, v_c, pu_c, z_c, xbc_c, dt_c, mg_c) = jnp.split(pc_, IN_SPLITS, axis=-1)
        rnn_l, rnn_c = rglru_branch(rx_c, rx_l, rg_c, rg_l, rnn_conv_w[li], rnn_conv_b[li], rnn_wa[li], rnn_ba[li], rnn_wx[li], rnn_bx[li], rnn_lam[li], need_ctx)
        att_l, att_c = diff_attention_branch(q_c, k_c, v_c, q_l, k_l, v_l, att_lambda[li], att_subln[li], li, cos, sin, need_ctx)
        pool_l = multiscale_pool(pu_l, pool_w[li], pool_b[li], pool_scale[li])
        ssm_l, ssm_c = ssd_branch(z_c, xbc_c, dt_c, z_l, xbc_l, dt_l, ssm_conv_w[li], ssm_conv_b[li], ssm_dt_bias[li], ssm_a_log[li], ssm_d[li], ssm_norm[li], need_ctx)
        y_l = merge_branches((rnn_l, att_l, pool_l, ssm_l), mg_l, w_branch_b[li], w_out_b[li])
        if need_ctx:
            pool_c = multiscale_pool(pu_c, pool_w[li], pool_b[li], pool_scale[li])
            y_c = merge_branches((rnn_c, att_c, pool_c, ssm_c), mg_c, w_branch_b[li], w_out_b[li])
            xc = layer_norm(DN_ALPHA * xc + cg1 * y_c, ln1_g[li], ln1_b[li])
        xl = layer_norm(DN_ALPHA * xl + g1 * y_l, ln1_g[li], ln1_b[li])

        hl2 = modulate(xl, sh2, sc2).reshape(-1, D_MODEL)
        if need_ctx:
            hc2 = modulate(xc, csh2, csc2).reshape(-1, D_MODEL)
            f = moe_ffn(jnp.concatenate([hc2, hl2], axis=0), router_w[li], router_b[li], wgu_b[li], b_gate_up[li], wd_b[li], b_down[li])
            f_c, f_l = f[:hc2.shape[0]], f[hc2.shape[0]:]
            xc = layer_norm(DN_ALPHA * xc + cg2 * f_c.reshape(xc.shape), ln2_g[li], ln2_b[li])
        else:
            f_l = moe_ffn(hl2, router_w[li], router_b[li], wgu_b[li], b_gate_up[li], wd_b[li], b_down[li])
        xl = layer_norm(DN_ALPHA * xl + g2 * f_l.reshape(xl.shape), ln2_g[li], ln2_b[li])
    return xl
```

```python
import functools
import math

import jax
import jax.numpy as jnp
import numpy as np
from jax import lax
from jax.experimental import pallas as pl
from jax.experimental.pallas import tpu as pltpu

D_MODEL = 1024
DEPTH = 4
GRID_W = 64
N_BRANCH = 4
BRANCH_W = D_MODEL // 2
RNN_W = BRANCH_W
RNN_BLOCK = 64
RG_C = 8.0
CONV_W = 4
ATT_DIM = 64
ATT_HEADS = BRANCH_W // (2 * ATT_DIM)
ATT_V_W = ATT_HEADS * 2 * ATT_DIM
ROPE_F = ATT_DIM // 4
ROPE_BASE = 10000.0
POOL_W = BRANCH_W
POOL_WINDOWS = (2, 4, 8, 16)
POOL_GROUPS = len(POOL_WINDOWS)
POOL_GW = POOL_W // POOL_GROUPS
SSM_W = BRANCH_W
SSM_P = 64
SSM_HEADS = SSM_W // SSM_P
SSM_G = 2
SSM_K = SSM_HEADS // SSM_G
SSM_N = 64
SSM_GW = SSM_W // SSM_G
SSM_CHUNK = 128
N_EXPERTS = 32
TOP_K = 4
D_EXPERT = D_MODEL
SWIGLU_LIMIT = 7.0
SWIGLU_ALPHA = 1.702
EXPERT_BLOCK = 256
DN_ALPHA = (2 * DEPTH) ** 0.25
IN_WIDTHS = (RNN_W, RNN_W, BRANCH_W, BRANCH_W, ATT_V_W, POOL_W, SSM_W, SSM_W + 2 * SSM_G * SSM_N, 2 * SSM_HEADS, N_BRANCH * D_MODEL)
IN_OFFS = tuple(int(v) for v in np.cumsum((0,) + IN_WIDTHS))

LANES = 128
SUBLANES = 8
VMEM_LIMIT = 48 * 1024 * 1024
ROW_TILE = 256

COL_RX, COL_RG, COL_Q, COL_K, COL_V, COL_PU, COL_Z, COL_SSD = 0, 1, 2, 3, 4, 5, 6, 7
PROJ_W = (COL_SSD + SSM_G) * BRANCH_W
SSD_XBC = SSM_GW + 2 * SSM_N

BF16 = jnp.bfloat16
F32 = jnp.float32


def _params(*sem):
    return pltpu.CompilerParams(dimension_semantics=sem, vmem_limit_bytes=VMEM_LIMIT)


def _normalize(x, eps):
    mu = jnp.mean(x, -1, keepdims=True)
    xc = x - mu
    var = jnp.mean(xc * xc, -1, keepdims=True)
    return xc * lax.rsqrt(var + eps)


def _segment_valid(t, off, n_ctx, total):
    lo = jnp.where(t < n_ctx, 0, n_ctx)
    hi = jnp.where(t < n_ctx, n_ctx, total)
    return (t + off >= lo) & (t + off < hi)


def _shift_rows(u, off, t, n_ctx):
    total = u.shape[0]
    rolled = pltpu.roll(u, (-off) % total, 0)
    return jnp.where(_segment_valid(t, off, n_ctx, total), rolled, 0.0)


def _centred_conv(u, w_ref, b_ref, n_ctx):
    t = lax.broadcasted_iota(jnp.int32, (u.shape[0], 1), 0)
    left = CONV_W // 2
    out = b_ref[...] + _shift_rows(u, -left, t, n_ctx) * w_ref[0:1, :]
    for k in range(1, CONV_W):
        tap = u if k == left else _shift_rows(u, k - left, t, n_ctx)
        out = out + tap * w_ref[k:k + 1, :]
    return out


def _in_proj_kernel(x_ref, mod_ref, w_ref, o_ref, h_ref, *, n_ctx):
    @pl.when(pl.program_id(2) == 0)
    def _():
        tm = x_ref.shape[1]
        row = pl.program_id(1) * tm + lax.broadcasted_iota(jnp.int32, (tm, 1), 0)
        is_ctx = row < n_ctx
        shift = jnp.where(is_ctx, mod_ref[0, 0, 0:1, :], mod_ref[0, 1, 0:1, :])
        scale = jnp.where(is_ctx, mod_ref[0, 0, 1:2, :], mod_ref[0, 1, 1:2, :])
        h_ref[...] = (_normalize(x_ref[0], 1e-6) * (1.0 + scale) + shift).astype(BF16)

    o_ref[0] = jnp.dot(h_ref[...], w_ref[...], preferred_element_type=F32)


def in_proj(x, mods, w, n_ctx, tm, tn):
    b, t, d = x.shape
    n = w.shape[1]
    return pl.pallas_call(
        functools.partial(_in_proj_kernel, n_ctx=n_ctx),
        out_shape=jax.ShapeDtypeStruct((b, t, n), F32),
        grid=(b, t // tm, n // tn),
        in_specs=[
            pl.BlockSpec((1, tm, d), lambda bi, i, j: (bi, i, 0)),
            pl.BlockSpec((1, 2, 6, d), lambda bi, i, j: (bi, 0, 0, 0)),
            pl.BlockSpec((d, tn), lambda bi, i, j: (0, j)),
        ],
        out_specs=pl.BlockSpec((1, tm, tn), lambda bi, i, j: (bi, i, j)),
        scratch_shapes=[pltpu.VMEM((tm, d), BF16)],
        compiler_params=_params("parallel", "parallel", "arbitrary"),
        name="in_proj",
    )(x, mods, w)


SCAN_ROWS = 64


def _tile_scan(a, b, reverse):
    rows = a.shape[0]
    sub = lax.broadcasted_iota(jnp.int32, (rows, 1), 0) % SUBLANES
    s = 1
    while s < SUBLANES:
        shift = (rows - s) if reverse else s
        keep = (sub + s < SUBLANES) if reverse else (sub >= s)
        a_sh = pltpu.roll(a, shift, 0)
        b_sh = pltpu.roll(b, shift, 0)
        b = jnp.where(keep, a * b_sh + b, b)
        a = jnp.where(keep, a * a_sh, a)
        s *= 2
    return a, b


def _rglru_kernel(rx_ref, rg_ref, cw_ref, cb_ref, wg_ref, bg_ref, sp_ref, o_ref, a_ref, b_ref, y_ref, *, n_ctx):
    total = rx_ref.shape[1]
    cw = rx_ref.shape[2]
    u = _centred_conv(rx_ref[0], cw_ref, cb_ref, n_ctx)
    g = jnp.dot(u.astype(BF16), wg_ref[0], preferred_element_type=F32) + bg_ref[0]
    n_tiles = SCAN_ROWS // SUBLANES

    for d in range(2):
        reverse = d == 1
        r = jax.nn.sigmoid(g[:, (2 * d) * cw:(2 * d + 1) * cw])
        i = jax.nn.sigmoid(g[:, (2 * d + 1) * cw:(2 * d + 2) * cw])
        log_a = -RG_C * r * sp_ref[0, :, d * cw:(d + 1) * cw]
        a = jnp.exp(log_a)
        a_ref[...] = a
        b_ref[...] = jnp.sqrt((1.0 - a) * (1.0 + a)) * (i * u)

        def step(it, h, lo, n_steps, reverse=reverse):
            blk = (n_steps - 1 - it) if reverse else it
            r0 = pl.multiple_of(lo + blk * SCAN_ROWS, SCAN_ROWS)
            a, b = _tile_scan(a_ref[pl.ds(r0, SCAN_ROWS), :], b_ref[pl.ds(r0, SCAN_ROWS), :], reverse)
            outs = [None] * n_tiles
            for j in (range(n_tiles - 1, -1, -1) if reverse else range(n_tiles)):
                sl = slice(j * SUBLANES, (j + 1) * SUBLANES)
                hj = b[sl] + a[sl] * h
                h = hj[0:1] if reverse else hj[SUBLANES - 1:SUBLANES]
                outs[j] = hj
            hs = jnp.concatenate(outs, axis=0)
            if reverse:
                y_ref[pl.ds(r0, SCAN_ROWS), :] = y_ref[pl.ds(r0, SCAN_ROWS), :] + hs
            else:
                y_ref[pl.ds(r0, SCAN_ROWS), :] = hs
            return h

        h0 = jnp.zeros((1, cw), F32)
        h_c = lax.fori_loop(0, n_ctx // SCAN_ROWS, functools.partial(step, lo=0, n_steps=n_ctx // SCAN_ROWS), h0)
        n_lat_steps = (total - n_ctx) // SCAN_ROWS
        lax.fori_loop(0, n_lat_steps, functools.partial(step, lo=n_ctx, n_steps=n_lat_steps), h_c)

    o_ref[0] = (jax.nn.gelu(rg_ref[0]) * y_ref[...]).astype(BF16)


def rglru(proj, conv_w, conv_b, wg, bg, sp, n_ctx):
    b, t, _ = proj.shape
    cw = LANES
    n_cg = RNN_W // cw
    per_block = BRANCH_W // cw
    return pl.pallas_call(
        functools.partial(_rglru_kernel, n_ctx=n_ctx),
        out_shape=jax.ShapeDtypeStruct((b, t, RNN_W), BF16),
        grid=(b, n_cg),
        in_specs=[
            pl.BlockSpec((1, t, cw), lambda bi, c: (bi, 0, COL_RX * per_block + c)),
            pl.BlockSpec((1, t, cw), lambda bi, c: (bi, 0, COL_RG * per_block + c)),
            pl.BlockSpec((CONV_W, cw), lambda bi, c: (0, c)),
            pl.BlockSpec((1, cw), lambda bi, c: (0, c)),
            pl.BlockSpec((1, cw, 4 * cw), lambda bi, c: (c, 0, 0)),
            pl.BlockSpec((1, 1, 4 * cw), lambda bi, c: (c, 0, 0)),
            pl.BlockSpec((1, 1, 2 * cw), lambda bi, c: (c, 0, 0)),
        ],
        out_specs=pl.BlockSpec((1, t, cw), lambda bi, c: (bi, 0, c)),
        scratch_shapes=[pltpu.VMEM((t, cw), F32), pltpu.VMEM((t, cw), F32), pltpu.VMEM((t, cw), F32)],
        compiler_params=_params("parallel", "parallel"),
        name="rglru",
    )(proj, proj, conv_w, conv_b, wg, bg, sp)


def _pool_kernel(u_ref, w_ref, b_ref, s_ref, o_ref, *, n_ctx):
    total = u_ref.shape[1]
    t = lax.broadcasted_iota(jnp.int32, (total, 1), 0)
    for gi, win in enumerate(POOL_WINDOWS):
        cols = slice(gi * POOL_GW, (gi + 1) * POOL_GW)
        u = u_ref[0, :, cols]
        acc = u
        cnt = jnp.ones((total, 1), F32)
        for off in range(-(win // 2), win - win // 2):
            if off == 0:
                continue
            acc = acc + _shift_rows(u, off, t, n_ctx)
            cnt = cnt + _segment_valid(t, off, n_ctx, total).astype(F32)
        mix = acc / cnt - u
        y = jnp.dot(mix.astype(BF16), w_ref[gi], preferred_element_type=F32) + b_ref[:, cols]
        o_ref[0, :, cols] = (y * s_ref[:, cols]).astype(BF16)


def pool(proj, w, bias, scale, n_ctx):
    b, t, _ = proj.shape
    return pl.pallas_call(
        functools.partial(_pool_kernel, n_ctx=n_ctx),
        out_shape=jax.ShapeDtypeStruct((b, t, POOL_W), BF16),
        grid=(b,),
        in_specs=[
            pl.BlockSpec((1, t, POOL_W), lambda bi: (bi, 0, COL_PU)),
            pl.BlockSpec((POOL_GROUPS, POOL_GW, POOL_GW), lambda bi: (0, 0, 0)),
            pl.BlockSpec((1, POOL_W), lambda bi: (0, 0)),
            pl.BlockSpec((1, POOL_W), lambda bi: (0, 0)),
        ],
        out_specs=pl.BlockSpec((1, t, POOL_W), lambda bi: (bi, 0, 0)),
        compiler_params=_params("parallel"),
        name="pool",
    )(proj, w, bias, scale)


def _split_dot(tri, v):
    hi = v.astype(BF16)
    r1 = v - hi.astype(F32)
    mid = r1.astype(BF16)
    lo = (r1 - mid.astype(F32)).astype(BF16)
    return (jnp.dot(tri, hi, preferred_element_type=F32) + jnp.dot(tri, mid, preferred_element_type=F32)
            + jnp.dot(tri, lo, preferred_element_type=F32))


def _split_dot_lhs(v, tri):
    hi = v.astype(BF16)
    r1 = v - hi.astype(F32)
    mid = r1.astype(BF16)
    lo = (r1 - mid.astype(F32)).astype(BF16)
    return (jnp.dot(hi, tri, preferred_element_type=F32) + jnp.dot(mid, tri, preferred_element_type=F32)
            + jnp.dot(lo, tri, preferred_element_type=F32))


def _per_head(cols, width):
    head = lax.broadcasted_iota(jnp.int32, (1, width), 1) // SSM_P
    out = cols[SSM_K - 1]
    for k in range(SSM_K - 2, -1, -1):
        out = jnp.where(head == k, cols[k], out)
    return out


def _ssd_kernel(blk_ref, z_ref, cw_ref, cb_ref, dtb_ref, alog_ref, dsk_ref, nw_ref, o_ref,
                u_ref, dt_ref, adt_ref, tr_ref, y_ref, st_ref, *, n_ctx):
    total = blk_ref.shape[1]
    ck = SSM_CHUNK
    n_chunks = total // ck
    ctx_chunks = n_ctx // ck
    u_ref[...] = jax.nn.silu(_centred_conv(blk_ref[0, :, 0:SSD_XBC], cw_ref.at[0], cb_ref.at[0], n_ctx))
    dt = jax.nn.softplus(blk_ref[0, :, SSD_XBC:SSD_XBC + LANES] + dtb_ref[0])
    dt_ref[...] = dt
    adt_ref[...] = dt * (-jnp.exp(alog_ref[0]))
    for c in range(n_chunks):
        rows = slice(c * ck, (c + 1) * ck)
        tr_ref[c, 0:ck, :] = u_ref[rows, SSM_GW:SSM_GW + 2 * SSM_N].T
        tr_ref[c, ck:2 * ck, :] = adt_ref[rows, :].T

    ri = lax.broadcasted_iota(jnp.int32, (ck, ck), 0)
    ci = lax.broadcasted_iota(jnp.int32, (ck, ck), 1)
    lower = (ci <= ri)
    lower_b = lower.astype(BF16)
    upper_b = (ci >= ri).astype(BF16)
    lane = lax.broadcasted_iota(jnp.int32, (1, LANES), 1)
    head_w = lax.broadcasted_iota(jnp.int32, (1, SSM_GW), 1) // SSM_P
    srow = lax.broadcasted_iota(jnp.int32, (ck, 1), 0)

    for d in range(2):
        reverse = d == 1
        tri_col, tri_row = (upper_b, lower_b) if reverse else (lower_b, upper_b)
        causal = (ci >= ri) if reverse else lower

        def chunk(it, carry, lo, n_steps, reverse=reverse, d=d, tri_col=tri_col, tri_row=tri_row, causal=causal):
            c = lo + ((n_steps - 1 - it) if reverse else it)
            r0 = pl.multiple_of(c * ck, ck)
            xs = u_ref[pl.ds(r0, ck), 0:SSM_GW]
            bc = u_ref[pl.ds(r0, ck), SSM_GW:SSM_GW + 2 * SSM_N]
            tr = tr_ref[c, 0:ck, :]
            c_lo = jnp.where(lane < SSM_N, pltpu.roll(bc, SSM_N, 1), 0.0).astype(BF16)
            cb = jnp.dot(c_lo, tr.astype(BF16), preferred_element_type=F32)
            cs_col = _split_dot(tri_col, adt_ref[pl.ds(r0, ck), :])
            cs_row = _split_dot_lhs(tr_ref[c, ck:2 * ck, :], tri_row)
            edge = ck - 1 if not reverse else 0
            dtc = dt_ref[pl.ds(r0, ck), :]
            cols, tots, dts = [], [], []
            y = jnp.zeros((ck, SSM_GW), F32)
            for k in range(SSM_K):
                j = d * SSM_K + k
                cols.append(cs_col[:, j:j + 1])
                tots.append(cs_col[edge:edge + 1, j:j + 1])
                dts.append(dtc[:, j:j + 1])
            xdt = xs * _per_head(dts, SSM_GW)
            for k in range(SSM_K):
                j = d * SSM_K + k
                seg = cols[k] - cs_row[j:j + 1, :]
                m = (cb * jnp.where(causal, jnp.exp(seg), 0.0)).astype(BF16)
                y = y + jnp.dot(m, jnp.where(head_w == k, xdt, 0.0).astype(BF16), preferred_element_type=F32)
            col_w = _per_head(cols, SSM_GW)
            tot_w = _per_head(tots, SSM_GW)
            state = st_ref[...]
            y = y + jnp.dot(c_lo, state.astype(BF16), preferred_element_type=F32) * jnp.exp(col_w)
            xd = (xdt * jnp.exp(tot_w - col_w)).astype(BF16)
            new = jnp.exp(tot_w) * state + jnp.dot(tr.astype(BF16), xd, preferred_element_type=F32)
            st_ref[...] = jnp.where(srow < SSM_N, new, 0.0)
            if reverse:
                y_ref[pl.ds(r0, ck), :] = y_ref[pl.ds(r0, ck), :] + y
            else:
                y_ref[pl.ds(r0, ck), :] = y
            return carry

        st_ref[...] = jnp.zeros_like(st_ref)
        lax.fori_loop(0, ctx_chunks, functools.partial(chunk, lo=0, n_steps=ctx_chunks), 0)
        lax.fori_loop(0, n_chunks - ctx_chunks, functools.partial(chunk, lo=ctx_chunks, n_steps=n_chunks - ctx_chunks), 0)

    y = y_ref[...] + dsk_ref[0] * u_ref[:, 0:SSM_GW]
    g = y * jax.nn.silu(z_ref[0])
    g = g * lax.rsqrt(jnp.mean(g * g, -1, keepdims=True) + 1e-5)
    o_ref[0] = (g * nw_ref[0]).astype(BF16)


def ssd(proj, conv_w, conv_b, dt_bias, a_log, d_skip, norm_w, n_ctx):
    b, t, _ = proj.shape
    n_chunks = t // SSM_CHUNK
    z_per = BRANCH_W // SSM_GW
    return pl.pallas_call(
        functools.partial(_ssd_kernel, n_ctx=n_ctx),
        out_shape=jax.ShapeDtypeStruct((b, t, SSM_W), BF16),
        grid=(b, SSM_G),
        in_specs=[
            pl.BlockSpec((1, t, BRANCH_W), lambda bi, g: (bi, 0, COL_SSD + g)),
            pl.BlockSpec((1, t, SSM_GW), lambda bi, g: (bi, 0, COL_Z * z_per + g)),
            pl.BlockSpec((1, CONV_W, SSD_XBC), lambda bi, g: (g, 0, 0)),
            pl.BlockSpec((1, 1, SSD_XBC), lambda bi, g: (g, 0, 0)),
            pl.BlockSpec((1, 1, LANES), lambda bi, g: (g, 0, 0)),
            pl.BlockSpec((1, 1, LANES), lambda bi, g: (g, 0, 0)),
            pl.BlockSpec((1, 1, SSM_GW), lambda bi, g: (g, 0, 0)),
            pl.BlockSpec((1, 1, SSM_GW), lambda bi, g: (g, 0, 0)),
        ],
        out_specs=pl.BlockSpec((1, t, SSM_GW), lambda bi, g: (bi, 0, g)),
        scratch_shapes=[
            pltpu.VMEM((t, SSD_XBC), F32),
            pltpu.VMEM((t, LANES), F32),
            pltpu.VMEM((t, LANES), F32),
            pltpu.VMEM((n_chunks, 2 * SSM_CHUNK, SSM_CHUNK), F32),
            pltpu.VMEM((t, SSM_GW), F32),
            pltpu.VMEM((SSM_CHUNK, SSM_GW), F32),
        ],
        compiler_params=_params("parallel", "parallel"),
        name="ssd",
    )(proj, proj, conv_w, conv_b, dt_bias, a_log, d_skip, norm_w)


def _rope(x, cos, sin):
    lane = lax.broadcasted_iota(jnp.int32, (1, x.shape[1]), 1)
    partner = jnp.where(lane % (2 * ROPE_F) < ROPE_F, pltpu.roll(x, x.shape[1] - ROPE_F, 1), pltpu.roll(x, ROPE_F, 1))
    return x * cos + partner * sin


def _diff_attn_kernel(par_ref, q_ref, k_ref, v_ref, cq_ref, sq_ref, ck_ref, sk_ref, g_ref, o_ref, kb_ref, vb_ref,
                      *, n_ctx, first_tile):
    lam = par_ref[0]
    out_scale = par_ref[1]

    @pl.when(pl.program_id(2) == 0)
    def _():
        kb_ref[...] = _rope(k_ref[0], ck_ref[...], sk_ref[...]).astype(BF16)
        vb_ref[...] = v_ref[0].astype(BF16)

    q = _rope(q_ref[0], cq_ref[...], sq_ref[...]).astype(BF16)
    lane = lax.broadcasted_iota(jnp.int32, q.shape, 1)
    dims = (((1,), (1,)), ((), ()))
    q1 = jnp.where(lane < ATT_DIM, q, jnp.zeros_like(q))
    q2 = jnp.where(lane >= ATT_DIM, q, jnp.zeros_like(q))

    def attend(n_keys):
        k = kb_ref[0:n_keys, :]

        def softmax(qh):
            s = lax.dot_general(qh, k, dims, preferred_element_type=F32) * (ATT_DIM ** -0.5)
            e = jnp.exp(s - jnp.max(s, -1, keepdims=True))
            return e / jnp.sum(e, -1, keepdims=True)

        w = softmax(q1) - lam * softmax(q2)
        o = jnp.dot(w.astype(BF16), vb_ref[0:n_keys, :], preferred_element_type=F32)
        o = o * lax.rsqrt(jnp.mean(o * o, -1, keepdims=True) + 1e-5) * g_ref[...] * out_scale
        o_ref[0] = o.astype(BF16)

    is_ctx_tile = pl.program_id(2) + first_tile == 0

    @pl.when(is_ctx_tile)
    def _():
        attend(n_ctx)

    @pl.when(jnp.logical_not(is_ctx_tile))
    def _():
        attend(kb_ref.shape[0])


def diff_attention(proj, cos_t, sin_t, subln, lam, out_scale, n_ctx, need_ctx):
    b, t, _ = proj.shape
    hw = 2 * ATT_DIM
    tq = ROW_TILE
    per_block = BRANCH_W // hw
    first = 0 if need_ctx else n_ctx // tq
    par = jnp.stack([lam, out_scale]).astype(F32)
    return pl.pallas_call(
        functools.partial(_diff_attn_kernel, n_ctx=n_ctx, first_tile=first),
        out_shape=jax.ShapeDtypeStruct((b, t, ATT_V_W), BF16),
        grid=(b, ATT_HEADS, t // tq - first),
        in_specs=[
            pl.BlockSpec(memory_space=pltpu.SMEM),
            pl.BlockSpec((1, tq, hw), lambda bi, h, i: (bi, i + first, COL_Q * per_block + h)),
            pl.BlockSpec((1, t, hw), lambda bi, h, i: (bi, 0, COL_K * per_block + h)),
            pl.BlockSpec((1, t, hw), lambda bi, h, i: (bi, 0, COL_V * per_block + h)),
            pl.BlockSpec((tq, hw), lambda bi, h, i: (i + first, 0)),
            pl.BlockSpec((tq, hw), lambda bi, h, i: (i + first, 0)),
            pl.BlockSpec((t, hw), lambda bi, h, i: (0, 0)),
            pl.BlockSpec((t, hw), lambda bi, h, i: (0, 0)),
            pl.BlockSpec((1, hw), lambda bi, h, i: (0, 0)),
        ],
        out_specs=pl.BlockSpec((1, tq, hw), lambda bi, h, i: (bi, i + first, h)),
        scratch_shapes=[pltpu.VMEM((t, hw), BF16), pltpu.VMEM((t, hw), BF16)],
        compiler_params=_params("parallel", "parallel", "arbitrary"),
        name="diff_attention",
    )(par, proj, proj, proj, cos_t, sin_t, cos_t, sin_t, subln.reshape(1, hw).astype(F32))


def _merge_kernel(x_ref, mod_ref, y0_ref, y1_ref, y2_ref, y3_ref, wg_ref, wb_ref, wo_ref, ln_ref, wr_ref, br_ref,
                  x1_ref, h2_ref, lg_ref):
    x = x_ref[0]
    mod = mod_ref[0, 0]
    h = (_normalize(x, 1e-6) * (1.0 + mod[1:2]) + mod[0:1]).astype(BF16)
    m = None
    for i, y_ref in enumerate((y0_ref, y1_ref, y2_ref, y3_ref)):
        gate = jax.nn.sigmoid(jnp.dot(h, wg_ref[:, i * D_MODEL:(i + 1) * D_MODEL], preferred_element_type=F32))
        term = gate * jnp.dot(y_ref[0], wb_ref[i], preferred_element_type=F32)
        m = term if m is None else m + term
    y = jnp.dot(m.astype(BF16), wo_ref[...], preferred_element_type=F32)
    x1 = _normalize(DN_ALPHA * x + mod[2:3] * y, 1e-5) * ln_ref[0:1] + ln_ref[1:2]
    x1_ref[0] = x1
    h2 = (_normalize(x1, 1e-6) * (1.0 + mod[4:5]) + mod[3:4]).astype(BF16)
    h2_ref[0] = h2
    lg_ref[0] = jnp.dot(h2, wr_ref[...], preferred_element_type=F32) + br_ref[...]


def merge(x, mods, ys, wg, wb, wo, ln, wr, br, first_tile):
    b, t, d = x.shape
    tm = ROW_TILE
    row = lambda bi, i: (bi, i + first_tile, 0)
    const2 = lambda bi, i: (0, 0)
    y_spec = pl.BlockSpec((1, tm, BRANCH_W), row)
    return pl.pallas_call(
        _merge_kernel,
        out_shape=(jax.ShapeDtypeStruct((b, t, d), F32), jax.ShapeDtypeStruct((b, t, d), BF16),
                   jax.ShapeDtypeStruct((b, t, LANES), F32)),
        grid=(b, t // tm - first_tile),
        in_specs=[
            pl.BlockSpec((1, tm, d), row),
            pl.BlockSpec((1, 1, 6, d), lambda bi, i: (bi, jnp.minimum(i + first_tile, 1), 0, 0)),
            y_spec, y_spec, y_spec, y_spec,
            pl.BlockSpec((d, N_BRANCH * d), const2),
            pl.BlockSpec((N_BRANCH, BRANCH_W, d), lambda bi, i: (0, 0, 0)),
            pl.BlockSpec((d, d), const2),
            pl.BlockSpec((2, d), const2),
            pl.BlockSpec((d, LANES), const2),
            pl.BlockSpec((1, LANES), const2),
        ],
        out_specs=(pl.BlockSpec((1, tm, d), row), pl.BlockSpec((1, tm, d), row), pl.BlockSpec((1, tm, LANES), row)),
        compiler_params=pltpu.CompilerParams(dimension_semantics=("parallel", "parallel"),
                                             vmem_limit_bytes=56 * 1024 * 1024),
        name="merge",
    )(x, mods, *ys, wg, wb, wo, ln, wr, br)


def _expert_kernel(be_ref, x_ref, p_ref, wgu_ref, bgu_ref, wd_ref, bd_ref, o_ref):
    del be_ref
    gu = jnp.dot(x_ref[...], wgu_ref[0], preferred_element_type=F32) + bgu_ref[0]
    gate = jnp.minimum(gu[:, :D_EXPERT], SWIGLU_LIMIT)
    up = jnp.clip(gu[:, D_EXPERT:], -SWIGLU_LIMIT, SWIGLU_LIMIT)
    glu = gate * jax.nn.sigmoid(gate * SWIGLU_ALPHA)
    act = ((up + 1.0) * glu).astype(BF16)
    y = jnp.dot(act, wd_ref[0], preferred_element_type=F32) + bd_ref[0]
    o_ref[...] = y * p_ref[...]


def expert_blocks(block_e, xb, slot_p, wgu, bgu, wd, bd):
    slots, d = xb.shape
    n_blocks = slots // EXPERT_BLOCK
    grid_spec = pltpu.PrefetchScalarGridSpec(
        num_scalar_prefetch=1,
        grid=(n_blocks,),
        in_specs=[
            pl.BlockSpec((EXPERT_BLOCK, d), lambda i, be: (i, 0)),
            pl.BlockSpec((EXPERT_BLOCK, 1), lambda i, be: (i, 0)),
            pl.BlockSpec((1, d, 2 * D_EXPERT), lambda i, be: (be[i], 0, 0)),
            pl.BlockSpec((1, 1, 2 * D_EXPERT), lambda i, be: (be[i], 0, 0)),
            pl.BlockSpec((1, D_EXPERT, d), lambda i, be: (be[i], 0, 0)),
            pl.BlockSpec((1, 1, d), lambda i, be: (be[i], 0, 0)),
        ],
        out_specs=pl.BlockSpec((EXPERT_BLOCK, d), lambda i, be: (i, 0)),
    )
    return pl.pallas_call(
        _expert_kernel,
        out_shape=jax.ShapeDtypeStruct((slots, d), F32),
        grid_spec=grid_spec,
        compiler_params=_params("arbitrary"),
        name="expert_blocks",
    )(block_e, xb, slot_p.reshape(slots, 1), wgu, bgu.reshape(N_EXPERTS, 1, -1), wd, bd.reshape(N_EXPERTS, 1, -1))


def _combine_norm_kernel(x_ref, f_ref, mod_ref, ln_ref, o_ref):
    x2 = DN_ALPHA * x_ref[0] + mod_ref[0, 0, 5:6] * f_ref[0]
    o_ref[0] = _normalize(x2, 1e-5) * ln_ref[0:1] + ln_ref[1:2]


def combine_norm(x1, f, mods, ln, first_tile):
    b, t, d = x1.shape
    tm = ROW_TILE
    n_tiles = t // tm - first_tile
    return pl.pallas_call(
        _combine_norm_kernel,
        out_shape=jax.ShapeDtypeStruct((b, n_tiles * tm, d), F32),
        grid=(b, n_tiles),
        in_specs=[
            pl.BlockSpec((1, tm, d), lambda bi, i: (bi, i + first_tile, 0)),
            pl.BlockSpec((1, tm, d), lambda bi, i: (bi, i, 0)),
            pl.BlockSpec((1, 1, 6, d), lambda bi, i: (bi, jnp.minimum(i + first_tile, 1), 0, 0)),
            pl.BlockSpec((2, d), lambda bi, i: (0, 0)),
        ],
        out_specs=pl.BlockSpec((1, tm, d), lambda bi, i: (bi, i, 0)),
        compiler_params=_params("parallel", "parallel"),
        name="combine_norm",
    )(x1, f, mods, ln)


def _proj_columns():
    o = IN_OFFS
    cols = list(range(o[0], o[7]))
    xbc, dt = o[7], o[8]
    for g in range(SSM_G):
        cols += [xbc + g * SSM_GW + i for i in range(SSM_GW)]
        cols += [xbc + SSM_W + g * SSM_N + i for i in range(SSM_N)]
        cols += [xbc + SSM_W + SSM_G * SSM_N + g * SSM_N + i for i in range(SSM_N)]
        cols += [dt + d * SSM_HEADS + g * SSM_K + k for d in range(2) for k in range(SSM_K)]
        cols += [-1] * (BRANCH_W - SSD_XBC - 2 * SSM_K)
    return np.asarray(cols, np.int32)


def _ssd_conv_columns():
    cols = []
    for g in range(SSM_G):
        cols += [g * SSM_GW + i for i in range(SSM_GW)]
        cols += [SSM_W + g * SSM_N + i for i in range(SSM_N)]
        cols += [SSM_W + SSM_G * SSM_N + g * SSM_N + i for i in range(SSM_N)]
    return np.asarray(cols, np.int32)


def _rope_tables(n_lat, n_ctx):
    rows = n_lat // GRID_W
    row = jnp.repeat(jnp.arange(rows, dtype=F32), GRID_W)
    col = jnp.tile(jnp.arange(GRID_W, dtype=F32), rows)
    inv = ROPE_BASE ** (-jnp.arange(ROPE_F, dtype=F32) / ROPE_F)
    ang = jnp.stack([row[:, None] * inv, col[:, None] * inv], axis=1)
    cos = jnp.broadcast_to(jnp.cos(ang)[:, None, :, None, :], (n_lat, 2, 2, 2, ROPE_F))
    sin = jnp.sin(ang)[:, None, :, None, :] * jnp.asarray([-1.0, 1.0], F32)[None, None, None, :, None]
    sin = jnp.broadcast_to(sin, (n_lat, 2, 2, 2, ROPE_F))
    hw = 2 * ATT_DIM
    cos = jnp.concatenate([jnp.ones((n_ctx, hw), F32), cos.reshape(n_lat, hw)], axis=0)
    sin = jnp.concatenate([jnp.zeros((n_ctx, hw), F32), sin.reshape(n_lat, hw)], axis=0)
    return cos, sin


def _block_diag(w):
    n, r, c = w.shape
    eye = jnp.eye(n, dtype=w.dtype)
    return (w[:, :, None, :] * eye[:, None, :, None]).reshape(n * r, n * c)


def _lane_pad(v, width=LANES):
    return jnp.pad(v, [(0, 0)] * (v.ndim - 1) + [(0, width - v.shape[-1])])


def _route(logits, n_rows_total):
    n = logits.shape[0]
    top_logit, top_e = lax.top_k(logits, TOP_K)
    probs = jax.nn.softmax(top_logit, axis=-1)
    a = n * TOP_K
    flat_e = top_e.reshape(a)
    flat_p = probs.reshape(a)
    order = jnp.argsort(flat_e)
    inv = jnp.argsort(order)
    counts = jnp.sum((flat_e[:, None] == jnp.arange(N_EXPERTS)[None, :]).astype(jnp.int32), axis=0)
    pcounts = (counts + EXPERT_BLOCK - 1) // EXPERT_BLOCK * EXPERT_BLOCK
    start = jnp.cumsum(counts) - counts
    pend = jnp.cumsum(pcounts)
    pstart = pend - pcounts
    n_blocks = -(-a // EXPERT_BLOCK) + N_EXPERTS
    slots = n_blocks * EXPERT_BLOCK
    block_e = jnp.minimum(jnp.searchsorted(pend // EXPERT_BLOCK, jnp.arange(n_blocks), side='right'), N_EXPERTS - 1)
    slot_e = jnp.repeat(block_e, EXPERT_BLOCK)
    j = jnp.arange(slots, dtype=jnp.int32) - pstart[slot_e]
    valid = (j < counts[slot_e]) & (jnp.arange(slots) < pend[N_EXPERTS - 1])
    src = order[jnp.clip(start[slot_e] + j, 0, a - 1)]
    slot_tok = jnp.where(valid, src // TOP_K, n_rows_total)
    slot_p = jnp.where(valid, flat_p[src], 0.0)
    dest = pstart[flat_e] + inv - start[flat_e]
    return block_e.astype(jnp.int32), slot_tok.astype(jnp.int32), slot_p, dest.reshape(n, TOP_K)


def kernel(x, c, ctx, c_ctx, w_ada, b_ada, w_in, rnn_conv_w, rnn_conv_b, rnn_wa, rnn_ba, rnn_wx, rnn_bx, rnn_lam, att_lambda, att_subln, pool_w, pool_b, pool_scale, ssm_conv_w, ssm_conv_b, ssm_dt_bias, ssm_a_log, ssm_d, ssm_norm, w_branch, w_out, ln1_g, ln1_b, ln2_g, ln2_b, router_w, router_b, w_gate_up, b_gate_up, w_down, b_down):
    bsz, n_lat, d = x.shape
    n_ctx = ctx.shape[1]
    total = n_ctx + n_lat
    assert n_ctx == ROW_TILE and n_lat % ROW_TILE == 0 and n_lat % GRID_W == 0
    ctx_tiles = n_ctx // ROW_TILE
    cos_t, sin_t = _rope_tables(n_lat, n_ctx)

    pcols = _proj_columns()
    w_main = jnp.where(pcols[None, None, :] >= 0, jnp.take(w_in, jnp.maximum(pcols, 0), axis=2), 0.0).astype(BF16)
    w_gates = w_in[:, :, IN_OFFS[9]:].astype(BF16)
    w_branch_b = w_branch.astype(BF16)
    w_out_b = w_out.astype(BF16)
    wgu_b = w_gate_up.astype(BF16)
    wd_b = w_down.astype(BF16)
    w_router = _lane_pad(router_w).astype(BF16)
    b_router = _lane_pad(router_b)[:, None, :]
    n_cg = RNN_W // LANES
    bpg = LANES // RNN_BLOCK
    gate_blocks = jnp.stack([rnn_wa[:, 0], rnn_wx[:, 0], rnn_wa[:, 1], rnn_wx[:, 1]], axis=1)
    gate_blocks = gate_blocks.reshape(DEPTH, 4, n_cg, bpg, RNN_BLOCK, RNN_BLOCK)
    rnn_wg = jax.vmap(jax.vmap(jax.vmap(_block_diag)))(gate_blocks)
    rnn_wg = rnn_wg.transpose(0, 2, 3, 1, 4).reshape(DEPTH, n_cg, LANES, 4 * LANES).astype(BF16)
    gate_bias = jnp.stack([rnn_ba[:, 0], rnn_bx[:, 0], rnn_ba[:, 1], rnn_bx[:, 1]], axis=1)
    rnn_bg = gate_bias.reshape(DEPTH, 4, n_cg, LANES).transpose(0, 2, 1, 3).reshape(DEPTH, n_cg, 1, 4 * LANES)
    rnn_sp = jax.nn.softplus(-rnn_lam).reshape(DEPTH, 2, n_cg, LANES).transpose(0, 2, 1, 3).reshape(DEPTH, n_cg, 1, 2 * LANES)
    ccols = _ssd_conv_columns()
    ssd_cw = jnp.take(ssm_conv_w, ccols, axis=2).reshape(DEPTH, CONV_W, SSM_G, SSD_XBC).transpose(0, 2, 1, 3)
    ssd_cb = jnp.take(ssm_conv_b, ccols, axis=1).reshape(DEPTH, SSM_G, 1, SSD_XBC)
    per_group = lambda v: v.reshape(DEPTH, 2, SSM_G, SSM_K).transpose(0, 2, 1, 3).reshape(DEPTH, SSM_G, 1, 2 * SSM_K)
    ssd_dtb = _lane_pad(per_group(ssm_dt_bias))
    ssd_alog = _lane_pad(per_group(ssm_a_log))
    ssd_dsk = jnp.repeat((ssm_d[:, 0] + ssm_d[:, 1]).reshape(DEPTH, SSM_G, 1, SSM_K), SSM_P, axis=-1)
    ssd_nw = ssm_norm.reshape(DEPTH, SSM_G, 1, SSM_GW)
    pool_wb = pool_w.astype(BF16)

    xs = jnp.concatenate([ctx, x], axis=1)
    out = None
    for li in range(DEPTH):
        need_ctx = li < DEPTH - 1
        first = 0 if need_ctx else ctx_tiles
        mod_l = jax.nn.silu(c) @ w_ada[li] + b_ada[li]
        mod_c = jnp.broadcast_to(jax.nn.silu(c_ctx) @ w_ada[li] + b_ada[li], mod_l.shape)
        mods = jnp.stack([mod_c, mod_l], axis=1).reshape(bsz, 2, 6, d)

        proj = in_proj(xs, mods, w_main[li], n_ctx, 768 if total % 768 == 0 else ROW_TILE, BRANCH_W * 3 if PROJ_W % (BRANCH_W * 3) == 0 else BRANCH_W)
        y_rnn = rglru(proj, rnn_conv_w[li], rnn_conv_b[li][None], rnn_wg[li], rnn_bg[li], rnn_sp[li], n_ctx)
        lam_init = 0.8 - 0.6 * math.exp(-0.3 * li)
        lv = att_lambda[li]
        lam = jnp.exp(jnp.sum(lv[0] * lv[1])) - jnp.exp(jnp.sum(lv[2] * lv[3])) + lam_init
        y_att = diff_attention(proj, cos_t, sin_t, att_subln[li], lam, jnp.asarray(1.0 - lam_init, F32), n_ctx, need_ctx)
        y_pool = pool(proj, pool_wb[li], pool_b[li][None], pool_scale[li][None], n_ctx)
        y_ssd = ssd(proj, ssd_cw[li], ssd_cb[li], ssd_dtb[li], ssd_alog[li], ssd_dsk[li], ssd_nw[li], n_ctx)
        ln1 = jnp.stack([ln1_g[li], ln1_b[li]])
        x1, h2, logits = merge(xs, mods, (y_rnn, y_att, y_pool, y_ssd), w_gates[li], w_branch_b[li], w_out_b[li], ln1,
                               w_router[li], b_router[li], first)

        rows = jnp.arange(bsz * total, dtype=jnp.int32).reshape(bsz, total)[:, first * ROW_TILE:].reshape(-1)
        lg = logits[:, first * ROW_TILE:, :N_EXPERTS].reshape(-1, N_EXPERTS)
        block_e, slot_tok, slot_p, dest = _route(lg, rows.shape[0])
        h2_rows = jnp.concatenate([h2.reshape(bsz * total, d), jnp.zeros((1, d), BF16)], axis=0)
        src_row = jnp.concatenate([rows, jnp.full((1,), bsz * total, jnp.int32)])[slot_tok]
        xb = h2_rows[src_row]
        yb = expert_blocks(block_e, xb, slot_p, wgu_b[li], b_gate_up[li], wd_b[li], b_down[li])
        f = jnp.sum(yb[dest], axis=1).reshape(bsz, total - first * ROW_TILE, d)
        ln2 = jnp.stack([ln2_g[li], ln2_b[li]])
        out = combine_norm(x1, f, mods, ln2, first)
        xs = out
    return out
```

```python
import functools
import math

import jax
import jax.numpy as jnp
import numpy as np
from jax import lax
from jax.experimental import pallas as pl
from jax.experimental.pallas import tpu as pltpu

D_MODEL = 1024
DEPTH = 4
GRID_W = 64
N_BRANCH = 4
BRANCH_W = D_MODEL // 2
RNN_W = BRANCH_W
RNN_BLOCK = 64
RG_C = 8.0
CONV_W = 4
ATT_DIM = 64
ATT_HEADS = BRANCH_W // (2 * ATT_DIM)
ATT_V_W = ATT_HEADS * 2 * ATT_DIM
ROPE_F = ATT_DIM // 4
ROPE_BASE = 10000.0
POOL_W = BRANCH_W
POOL_WINDOWS = (2, 4, 8, 16)
POOL_GROUPS = len(POOL_WINDOWS)
POOL_GW = POOL_W // POOL_GROUPS
SSM_W = BRANCH_W
SSM_P = 64
SSM_HEADS = SSM_W // SSM_P
SSM_G = 2
SSM_K = SSM_HEADS // SSM_G
SSM_N = 64
SSM_GW = SSM_W // SSM_G
SSM_CHUNK = 128
N_EXPERTS = 32
TOP_K = 4
D_EXPERT = D_MODEL
SWIGLU_LIMIT = 7.0
SWIGLU_ALPHA = 1.702
EXPERT_BLOCK = 256
DN_ALPHA = (2 * DEPTH) ** 0.25
IN_WIDTHS = (RNN_W, RNN_W, BRANCH_W, BRANCH_W, ATT_V_W, POOL_W, SSM_W, SSM_W + 2 * SSM_G * SSM_N, 2 * SSM_HEADS, N_BRANCH * D_MODEL)
IN_OFFS = tuple(int(v) for v in np.cumsum((0,) + IN_WIDTHS))

LANES = 128
SUBLANES = 8
VMEM_LIMIT = 48 * 1024 * 1024
ROW_TILE = 256

COL_RX, COL_RG, COL_Q, COL_K, COL_V, COL_PU, COL_Z, COL_SSD = 0, 1, 2, 3, 4, 5, 6, 7
PROJ_W = (COL_SSD + SSM_G) * BRANCH_W
SSD_XBC = SSM_GW + 2 * SSM_N

BF16 = jnp.bfloat16
F32 = jnp.float32


def _params(*sem):
    return pltpu.CompilerParams(dimension_semantics=sem, vmem_limit_bytes=VMEM_LIMIT)


def _normalize(x, eps):
    mu = jnp.mean(x, -1, keepdims=True)
    xc = x - mu
    var = jnp.mean(xc * xc, -1, keepdims=True)
    return xc * lax.rsqrt(var + eps)


def _segment_valid(t, off, n_ctx, total):
    lo = jnp.where(t < n_ctx, 0, n_ctx)
    hi = jnp.where(t < n_ctx, n_ctx, total)
    return (t + off >= lo) & (t + off < hi)


def _shift_rows(u, off, t, n_ctx):
    total = u.shape[0]
    rolled = pltpu.roll(u, (-off) % total, 0)
    return jnp.where(_segment_valid(t, off, n_ctx, total), rolled, 0.0)


def _centred_conv(u, w_ref, b_ref, n_ctx):
    t = lax.broadcasted_iota(jnp.int32, (u.shape[0], 1), 0)
    left = CONV_W // 2
    out = b_ref[...] + _shift_rows(u, -left, t, n_ctx) * w_ref[0:1, :]
    for k in range(1, CONV_W):
        tap = u if k == left else _shift_rows(u, k - left, t, n_ctx)
        out = out + tap * w_ref[k:k + 1, :]
    return out


def _in_proj_kernel(x_ref, mod_ref, w_ref, o_ref, h_ref, *, n_ctx):
    @pl.when(pl.program_id(2) == 0)
    def _():
        tm = x_ref.shape[1]
        row = pl.program_id(1) * tm + lax.broadcasted_iota(jnp.int32, (tm, 1), 0)
        is_ctx = row < n_ctx
        shift = jnp.where(is_ctx, mod_ref[0, 0, 0:1, :], mod_ref[0, 1, 0:1, :])
        scale = jnp.where(is_ctx, mod_ref[0, 0, 1:2, :], mod_ref[0, 1, 1:2, :])
        h_ref[...] = (_normalize(x_ref[0], 1e-6) * (1.0 + scale) + shift).astype(BF16)

    o_ref[0] = jnp.dot(h_ref[...], w_ref[...], preferred_element_type=F32)


def in_proj(x, mods, w, n_ctx, tm, tn):
    b, t, d = x.shape
    n = w.shape[1]
    return pl.pallas_call(
        functools.partial(_in_proj_kernel, n_ctx=n_ctx),
        out_shape=jax.ShapeDtypeStruct((b, t, n), F32),
        grid=(b, t // tm, n // tn),
        in_specs=[
            pl.BlockSpec((1, tm, d), lambda bi, i, j: (bi, i, 0)),
            pl.BlockSpec((1, 2, 6, d), lambda bi, i, j: (bi, 0, 0, 0)),
            pl.BlockSpec((d, tn), lambda bi, i, j: (0, j)),
        ],
        out_specs=pl.BlockSpec((1, tm, tn), lambda bi, i, j: (bi, i, j)),
        scratch_shapes=[pltpu.VMEM((tm, d), BF16)],
        compiler_params=_params("parallel", "parallel", "arbitrary"),
        name="in_proj",
    )(x, mods, w)


SCAN_ROWS = 64


def _tile_scan(a, b, reverse):
    rows = a.shape[0]
    sub = lax.broadcasted_iota(jnp.int32, (rows, 1), 0) % SUBLANES
    s = 1
    while s < SUBLANES:
        shift = (rows - s) if reverse else s
        keep = (sub + s < SUBLANES) if reverse else (sub >= s)
        a_sh = pltpu.roll(a, shift, 0)
        b_sh = pltpu.roll(b, shift, 0)
        b = jnp.where(keep, a * b_sh + b, b)
        a = jnp.where(keep, a * a_sh, a)
        s *= 2
    return a, b


def _rglru_kernel(rx_ref, rg_ref, cw_ref, cb_ref, wg_ref, bg_ref, sp_ref, o_ref, a_ref, b_ref, y_ref, *, n_ctx):
    total = rx_ref.shape[1]
    cw = rx_ref.shape[2]
    u = _centred_conv(rx_ref[0], cw_ref, cb_ref, n_ctx)
    g = jnp.dot(u.astype(BF16), wg_ref[0], preferred_element_type=F32) + bg_ref[0]
    n_tiles = SCAN_ROWS // SUBLANES

    for d in range(2):
        reverse = d == 1
        r = jax.nn.sigmoid(g[:, (2 * d) * cw:(2 * d + 1) * cw])
        i = jax.nn.sigmoid(g[:, (2 * d + 1) * cw:(2 * d + 2) * cw])
        log_a = -RG_C * r * sp_ref[0, :, d * cw:(d + 1) * cw]
        a = jnp.exp(log_a)
        a_ref[...] = a
        b_ref[...] = jnp.sqrt((1.0 - a) * (1.0 + a)) * (i * u)

        def step(it, h, lo, n_steps, reverse=reverse):
            blk = (n_steps - 1 - it) if reverse else it
            r0 = pl.multiple_of(lo + blk * SCAN_ROWS, SCAN_ROWS)
            a, b = _tile_scan(a_ref[pl.ds(r0, SCAN_ROWS), :], b_ref[pl.ds(r0, SCAN_ROWS), :], reverse)
            outs = [None] * n_tiles
            for j in (range(n_tiles - 1, -1, -1) if reverse else range(n_tiles)):
                sl = slice(j * SUBLANES, (j + 1) * SUBLANES)
                hj = b[sl] + a[sl] * h
                h = hj[0:1] if reverse else hj[SUBLANES - 1:SUBLANES]
                outs[j] = hj
            hs = jnp.concatenate(outs, axis=0)
            if reverse:
                y_ref[pl.ds(r0, SCAN_ROWS), :] = y_ref[pl.ds(r0, SCAN_ROWS), :] + hs
            else:
                y_ref[pl.ds(r0, SCAN_ROWS), :] = hs
            return h

        h0 = jnp.zeros((1, cw), F32)
        h_c = lax.fori_loop(0, n_ctx // SCAN_ROWS, functools.partial(step, lo=0, n_steps=n_ctx // SCAN_ROWS), h0)
        n_lat_steps = (total - n_ctx) // SCAN_ROWS
        lax.fori_loop(0, n_lat_steps, functools.partial(step, lo=n_ctx, n_steps=n_lat_steps), h_c)

    o_ref[0] = (jax.nn.gelu(rg_ref[0]) * y_ref[...]).astype(BF16)


def rglru(proj, conv_w, conv_b, wg, bg, sp, n_ctx):
    b, t, _ = proj.shape
    cw = LANES
    n_cg = RNN_W // cw
    per_block = BRANCH_W // cw
    return pl.pallas_call(
        functools.partial(_rglru_kernel, n_ctx=n_ctx),
        out_shape=jax.ShapeDtypeStruct((b, t, RNN_W), BF16),
        grid=(b, n_cg),
        in_specs=[
            pl.BlockSpec((1, t, cw), lambda bi, c: (bi, 0, COL_RX * per_block + c)),
            pl.BlockSpec((1, t, cw), lambda bi, c: (bi, 0, COL_RG * per_block + c)),
            pl.BlockSpec((CONV_W, cw), lambda bi, c: (0, c)),
            pl.BlockSpec((1, cw), lambda bi, c: (0, c)),
            pl.BlockSpec((1, cw, 4 * cw), lambda bi, c: (c, 0, 0)),
            pl.BlockSpec((1, 1, 4 * cw), lambda bi, c: (c, 0, 0)),
            pl.BlockSpec((1, 1, 2 * cw), lambda bi, c: (c, 0, 0)),
        ],
        out_specs=pl.BlockSpec((1, t, cw), lambda bi, c: (bi, 0, c)),
        scratch_shapes=[pltpu.VMEM((t, cw), F32), pltpu.VMEM((t, cw), F32), pltpu.VMEM((t, cw), F32)],
        compiler_params=_params("parallel", "parallel"),
        name="rglru",
    )(proj, proj, conv_w, conv_b, wg, bg, sp)


def _pool_kernel(u_ref, w_ref, b_ref, s_ref, o_ref, *, n_ctx):
    total = u_ref.shape[1]
    t = lax.broadcasted_iota(jnp.int32, (total, 1), 0)
    for gi, win in enumerate(POOL_WINDOWS):
        cols = slice(gi * POOL_GW, (gi + 1) * POOL_GW)
        u = u_ref[0, :, cols]
        acc = u
        cnt = jnp.ones((total, 1), F32)
        for off in range(-(win // 2), win - win // 2):
            if off == 0:
                continue
            acc = acc + _shift_rows(u, off, t, n_ctx)
            cnt = cnt + _segment_valid(t, off, n_ctx, total).astype(F32)
        mix = acc / cnt - u
        y = jnp.dot(mix.astype(BF16), w_ref[gi], preferred_element_type=F32) + b_ref[:, cols]
        o_ref[0, :, cols] = (y * s_ref[:, cols]).astype(BF16)


def pool(proj, w, bias, scale, n_ctx):
    b, t, _ = proj.shape
    return pl.pallas_call(
        functools.partial(_pool_kernel, n_ctx=n_ctx),
        out_shape=jax.ShapeDtypeStruct((b, t, POOL_W), BF16),
        grid=(b,),
        in_specs=[
            pl.BlockSpec((1, t, POOL_W), lambda bi: (bi, 0, COL_PU)),
            pl.BlockSpec((POOL_GROUPS, POOL_GW, POOL_GW), lambda bi: (0, 0, 0)),
            pl.BlockSpec((1, POOL_W), lambda bi: (0, 0)),
            pl.BlockSpec((1, POOL_W), lambda bi: (0, 0)),
        ],
        out_specs=pl.BlockSpec((1, t, POOL_W), lambda bi: (bi, 0, 0)),
        compiler_params=_params("parallel"),
        name="pool",
    )(proj, w, bias, scale)


def _split_dot(tri, v):
    hi = v.astype(BF16)
    r1 = v - hi.astype(F32)
    mid = r1.astype(BF16)
    lo = (r1 - mid.astype(F32)).astype(BF16)
    return (jnp.dot(tri, hi, preferred_element_type=F32) + jnp.dot(tri, mid, preferred_element_type=F32)
            + jnp.dot(tri, lo, preferred_element_type=F32))


def _split_dot_lhs(v, tri):
    hi = v.astype(BF16)
    r1 = v - hi.astype(F32)
    mid = r1.astype(BF16)
    lo = (r1 - mid.astype(F32)).astype(BF16)
    return (jnp.dot(hi, tri, preferred_element_type=F32) + jnp.dot(mid, tri, preferred_element_type=F32)
            + jnp.dot(lo, tri, preferred_element_type=F32))


def _per_head(cols, width):
    head = lax.broadcasted_iota(jnp.int32, (1, width), 1) // SSM_P
    out = cols[SSM_K - 1]
    for k in range(SSM_K - 2, -1, -1):
        out = jnp.where(head == k, cols[k], out)
    return out


def _ssd_kernel(blk_ref, z_ref, cw_ref, cb_ref, dtb_ref, alog_ref, dsk_ref, nw_ref, o_ref,
                u_ref, dt_ref, adt_ref, tr_ref, y_ref, st_ref, *, n_ctx):
    total = blk_ref.shape[1]
    ck = SSM_CHUNK
    n_chunks = total // ck
    ctx_chunks = n_ctx // ck
    u_ref[...] = jax.nn.silu(_centred_conv(blk_ref[0, :, 0:SSD_XBC], cw_ref.at[0], cb_ref.at[0], n_ctx))
    dt = jax.nn.softplus(blk_ref[0, :, SSD_XBC:SSD_XBC + LANES] + dtb_ref[0])
    dt_ref[...] = dt
    adt_ref[...] = dt * (-jnp.exp(alog_ref[0]))
    for c in range(n_chunks):
        rows = slice(c * ck, (c + 1) * ck)
        tr_ref[c, 0:ck, :] = u_ref[rows, SSM_GW:SSM_GW + 2 * SSM_N].T
        tr_ref[c, ck:2 * ck, :] = adt_ref[rows, :].T

    ri = lax.broadcasted_iota(jnp.int32, (ck, ck), 0)
    ci = lax.broadcasted_iota(jnp.int32, (ck, ck), 1)
    lower = (ci <= ri)
    lower_b = lower.astype(BF16)
    upper_b = (ci >= ri).astype(BF16)
    lane = lax.broadcasted_iota(jnp.int32, (1, LANES), 1)
    head_w = lax.broadcasted_iota(jnp.int32, (1, SSM_GW), 1) // SSM_P
    srow = lax.broadcasted_iota(jnp.int32, (ck, 1), 0)

    for d in range(2):
        reverse = d == 1
        tri_col, tri_row = (upper_b, lower_b) if reverse else (lower_b, upper_b)
        causal = (ci >= ri) if reverse else lower

        def chunk(it, carry, lo, n_steps, reverse=reverse, d=d, tri_col=tri_col, tri_row=tri_row, causal=causal):
            c = lo + ((n_steps - 1 - it) if reverse else it)
            r0 = pl.multiple_of(c * ck, ck)
            xs = u_ref[pl.ds(r0, ck), 0:SSM_GW]
            bc = u_ref[pl.ds(r0, ck), SSM_GW:SSM_GW + 2 * SSM_N]
            tr = tr_ref[c, 0:ck, :]
            c_lo = jnp.where(lane < SSM_N, pltpu.roll(bc, SSM_N, 1), 0.0).astype(BF16)
            cb = jnp.dot(c_lo, tr.astype(BF16), preferred_element_type=F32)
            cs_col = _split_dot(tri_col, adt_ref[pl.ds(r0, ck), :])
            cs_row = _split_dot_lhs(tr_ref[c, ck:2 * ck, :], tri_row)
            edge = ck - 1 if not reverse else 0
            dtc = dt_ref[pl.ds(r0, ck), :]
            cols, tots, dts = [], [], []
            y = jnp.zeros((ck, SSM_GW), F32)
            for k in range(SSM_K):
                j = d * SSM_K + k
                cols.append(cs_col[:, j:j + 1])
                tots.append(cs_col[edge:edge + 1, j:j + 1])
                dts.append(dtc[:, j:j + 1])
            xdt = xs * _per_head(dts, SSM_GW)
            for k in range(SSM_K):
                j = d * SSM_K + k
                seg = cols[k] - cs_row[j:j + 1, :]
                m = (cb * jnp.where(causal, jnp.exp(seg), 0.0)).astype(BF16)
                y = y + jnp.dot(m, jnp.where(head_w == k, xdt, 0.0).astype(BF16), preferred_element_type=F32)
            col_w = _per_head(cols, SSM_GW)
            tot_w = _per_head(tots, SSM_GW)
            state = st_ref[...]
            y = y + jnp.dot(c_lo, state.astype(BF16), preferred_element_type=F32) * jnp.exp(col_w)
            xd = (xdt * jnp.exp(tot_w - col_w)).astype(BF16)
            new = jnp.exp(tot_w) * state + jnp.dot(tr.astype(BF16), xd, preferred_element_type=F32)
            st_ref[...] = jnp.where(srow < SSM_N, new, 0.0)
            if reverse:
                y_ref[pl.ds(r0, ck), :] = y_ref[pl.ds(r0, ck), :] + y
            else:
                y_ref[pl.ds(r0, ck), :] = y
            return carry

        st_ref[...] = jnp.zeros_like(st_ref)
        lax.fori_loop(0, ctx_chunks, functools.partial(chunk, lo=0, n_steps=ctx_chunks), 0)
        lax.fori_loop(0, n_chunks - ctx_chunks, functools.partial(chunk, lo=ctx_chunks, n_steps=n_chunks - ctx_chunks), 0)

    y = y_ref[...] + dsk_ref[0] * u_ref[:, 0:SSM_GW]
    g = y * jax.nn.silu(z_ref[0])
    g = g * lax.rsqrt(jnp.mean(g * g, -1, keepdims=True) + 1e-5)
    o_ref[0] = (g * nw_ref[0]).astype(BF16)


def ssd(proj, conv_w, conv_b, dt_bias, a_log, d_skip, norm_w, n_ctx):
    b, t, _ = proj.shape
    n_chunks = t // SSM_CHUNK
    z_per = BRANCH_W // SSM_GW
    return pl.pallas_call(
        functools.partial(_ssd_kernel, n_ctx=n_ctx),
        out_shape=jax.ShapeDtypeStruct((b, t, SSM_W), BF16),
        grid=(b, SSM_G),
        in_specs=[
            pl.BlockSpec((1, t, BRANCH_W), lambda bi, g: (bi, 0, COL_SSD + g)),
            pl.BlockSpec((1, t, SSM_GW), lambda bi, g: (bi, 0, COL_Z * z_per + g)),
            pl.BlockSpec((1, CONV_W, SSD_XBC), lambda bi, g: (g, 0, 0)),
            pl.BlockSpec((1, 1, SSD_XBC), lambda bi, g: (g, 0, 0)),
            pl.BlockSpec((1, 1, LANES), lambda bi, g: (g, 0, 0)),
            pl.BlockSpec((1, 1, LANES), lambda bi, g: (g, 0, 0)),
            pl.BlockSpec((1, 1, SSM_GW), lambda bi, g: (g, 0, 0)),
            pl.BlockSpec((1, 1, SSM_GW), lambda bi, g: (g, 0, 0)),
        ],
        out_specs=pl.BlockSpec((1, t, SSM_GW), lambda bi, g: (bi, 0, g)),
        scratch_shapes=[
            pltpu.VMEM((t, SSD_XBC), F32),
            pltpu.VMEM((t, LANES), F32),
            pltpu.VMEM((t, LANES), F32),
            pltpu.VMEM((n_chunks, 2 * SSM_CHUNK, SSM_CHUNK), F32),
            pltpu.VMEM((t, SSM_GW), F32),
            pltpu.VMEM((SSM_CHUNK, SSM_GW), F32),
        ],
        compiler_params=_params("parallel", "parallel"),
        name="ssd",
    )(proj, proj, conv_w, conv_b, dt_bias, a_log, d_skip, norm_w)


def _rope(x, cos, sin):
    lane = lax.broadcasted_iota(jnp.int32, (1, x.shape[1]), 1)
    partner = jnp.where(lane % (2 * ROPE_F) < ROPE_F, pltpu.roll(x, x.shape[1] - ROPE_F, 1), pltpu.roll(x, ROPE_F, 1))
    return x * cos + partner * sin


def _diff_attn_kernel(par_ref, q_ref, k_ref, v_ref, cq_ref, sq_ref, ck_ref, sk_ref, g_ref, o_ref, kb_ref, vb_ref,
                      *, n_ctx, first_tile):
    lam = par_ref[0]
    out_scale = par_ref[1]

    @pl.when(pl.program_id(2) == 0)
    def _():
        kb_ref[...] = _rope(k_ref[0], ck_ref[...], sk_ref[...]).astype(BF16)
        vb_ref[...] = v_ref[0].astype(BF16)

    q = (_rope(q_ref[0], cq_ref[...], sq_ref[...]) * (ATT_DIM ** -0.5)).astype(BF16)
    lane = lax.broadcasted_iota(jnp.int32, q.shape, 1)
    dims = (((1,), (1,)), ((), ()))
    q1 = jnp.where(lane < ATT_DIM, q, jnp.zeros_like(q))
    q2 = jnp.where(lane >= ATT_DIM, q, jnp.zeros_like(q))

    def attend(n_keys):
        k = kb_ref[0:n_keys, :]
        v = vb_ref[0:n_keys, :]

        def softmax_v(qh):
            s = lax.dot_general(qh, k, dims, preferred_element_type=F32)
            e = jnp.exp(s - jnp.max(s, -1, keepdims=True))
            return jnp.dot(e.astype(BF16), v, preferred_element_type=F32) / jnp.sum(e, -1, keepdims=True)

        o = softmax_v(q1) - lam * softmax_v(q2)
        o = o * lax.rsqrt(jnp.mean(o * o, -1, keepdims=True) + 1e-5) * g_ref[...] * out_scale
        o_ref[0] = o.astype(BF16)

    is_ctx_tile = pl.program_id(2) + first_tile == 0

    @pl.when(is_ctx_tile)
    def _():
        attend(n_ctx)

    @pl.when(jnp.logical_not(is_ctx_tile))
    def _():
        attend(kb_ref.shape[0])


def diff_attention(proj, cos_t, sin_t, subln, lam, out_scale, n_ctx, need_ctx):
    b, t, _ = proj.shape
    hw = 2 * ATT_DIM
    tq = ROW_TILE
    per_block = BRANCH_W // hw
    first = 0 if need_ctx else n_ctx // tq
    par = jnp.stack([lam, out_scale]).astype(F32)
    return pl.pallas_call(
        functools.partial(_diff_attn_kernel, n_ctx=n_ctx, first_tile=first),
        out_shape=jax.ShapeDtypeStruct((b, t, ATT_V_W), BF16),
        grid=(b, ATT_HEADS, t // tq - first),
        in_specs=[
            pl.BlockSpec(memory_space=pltpu.SMEM),
            pl.BlockSpec((1, tq, hw), lambda bi, h, i: (bi, i + first, COL_Q * per_block + h)),
            pl.BlockSpec((1, t, hw), lambda bi, h, i: (bi, 0, COL_K * per_block + h)),
            pl.BlockSpec((1, t, hw), lambda bi, h, i: (bi, 0, COL_V * per_block + h)),
            pl.BlockSpec((tq, hw), lambda bi, h, i: (i + first, 0)),
            pl.BlockSpec((tq, hw), lambda bi, h, i: (i + first, 0)),
            pl.BlockSpec((t, hw), lambda bi, h, i: (0, 0)),
            pl.BlockSpec((t, hw), lambda bi, h, i: (0, 0)),
            pl.BlockSpec((1, hw), lambda bi, h, i: (0, 0)),
        ],
        out_specs=pl.BlockSpec((1, tq, hw), lambda bi, h, i: (bi, i + first, h)),
        scratch_shapes=[pltpu.VMEM((t, hw), BF16), pltpu.VMEM((t, hw), BF16)],
        compiler_params=_params("parallel", "parallel", "arbitrary"),
        name="diff_attention",
    )(par, proj, proj, proj, cos_t, sin_t, cos_t, sin_t, subln.reshape(1, hw).astype(F32))


def _merge_kernel(x_ref, mod_ref, y0_ref, y1_ref, y2_ref, y3_ref, wg_ref, wb_ref, wo_ref, ln_ref, wr_ref, br_ref,
                  x1_ref, h2_ref, lg_ref):
    x = x_ref[0]
    mod = mod_ref[0, 0]
    h = (_normalize(x, 1e-6) * (1.0 + mod[1:2]) + mod[0:1]).astype(BF16)
    m = None
    for i, y_ref in enumerate((y0_ref, y1_ref, y2_ref, y3_ref)):
        gate = jax.nn.sigmoid(jnp.dot(h, wg_ref[:, i * D_MODEL:(i + 1) * D_MODEL], preferred_element_type=F32))
        term = gate * jnp.dot(y_ref[0], wb_ref[i], preferred_element_type=F32)
        m = term if m is None else m + term
    y = jnp.dot(m.astype(BF16), wo_ref[...], preferred_element_type=F32)
    x1 = _normalize(DN_ALPHA * x + mod[2:3] * y, 1e-5) * ln_ref[0:1] + ln_ref[1:2]
    x1_ref[0] = x1
    h2 = (_normalize(x1, 1e-6) * (1.0 + mod[4:5]) + mod[3:4]).astype(BF16)
    h2_ref[0] = h2
    lg_ref[0] = jnp.dot(h2, wr_ref[...], preferred_element_type=F32) + br_ref[...]


def merge(x, mods, ys, wg, wb, wo, ln, wr, br, first_tile):
    b, t, d = x.shape
    tm = ROW_TILE
    row = lambda bi, i: (bi, i + first_tile, 0)
    const2 = lambda bi, i: (0, 0)
    y_spec = pl.BlockSpec((1, tm, BRANCH_W), row)
    return pl.pallas_call(
        _merge_kernel,
        out_shape=(jax.ShapeDtypeStruct((b, t, d), F32), jax.ShapeDtypeStruct((b, t, d), BF16),
                   jax.ShapeDtypeStruct((b, t, LANES), F32)),
        grid=(b, t // tm - first_tile),
        in_specs=[
            pl.BlockSpec((1, tm, d), row),
            pl.BlockSpec((1, 1, 6, d), lambda bi, i: (bi, jnp.minimum(i + first_tile, 1), 0, 0)),
            y_spec, y_spec, y_spec, y_spec,
            pl.BlockSpec((d, N_BRANCH * d), const2),
            pl.BlockSpec((N_BRANCH, BRANCH_W, d), lambda bi, i: (0, 0, 0)),
            pl.BlockSpec((d, d), const2),
            pl.BlockSpec((2, d), const2),
            pl.BlockSpec((d, LANES), const2),
            pl.BlockSpec((1, LANES), const2),
        ],
        out_specs=(pl.BlockSpec((1, tm, d), row), pl.BlockSpec((1, tm, d), row), pl.BlockSpec((1, tm, LANES), row)),
        compiler_params=pltpu.CompilerParams(dimension_semantics=("parallel", "parallel"),
                                             vmem_limit_bytes=56 * 1024 * 1024),
        name="merge",
    )(x, mods, *ys, wg, wb, wo, ln, wr, br)


def _expert_kernel(be_ref, nb_ref, x_ref, wgu_ref, bgu_ref, wd_ref, bd_ref, o_ref):
    del be_ref

    @pl.when(pl.program_id(0) < nb_ref[0])
    def _():
        gu = jnp.dot(x_ref[...], wgu_ref[0], preferred_element_type=F32) + bgu_ref[0]
        gate = jnp.minimum(gu[:, :D_EXPERT], SWIGLU_LIMIT)
        up = jnp.clip(gu[:, D_EXPERT:], -SWIGLU_LIMIT, SWIGLU_LIMIT)
        glu = gate * jax.nn.sigmoid(gate * SWIGLU_ALPHA)
        act = ((up + 1.0) * glu).astype(BF16)
        y = jnp.dot(act, wd_ref[0], preferred_element_type=F32) + bd_ref[0]
        o_ref[...] = y.astype(BF16)


def expert_blocks(block_e, n_used, xb, wgu, bgu, wd, bd):
    slots, d = xb.shape
    n_blocks = slots // EXPERT_BLOCK
    blk = lambda i, be, nb: jnp.minimum(i, nb[0] - 1)
    grid_spec = pltpu.PrefetchScalarGridSpec(
        num_scalar_prefetch=2,
        grid=(n_blocks,),
        in_specs=[
            pl.BlockSpec((EXPERT_BLOCK, d), lambda i, be, nb: (blk(i, be, nb), 0)),
            pl.BlockSpec((1, d, 2 * D_EXPERT), lambda i, be, nb: (be[blk(i, be, nb)], 0, 0)),
            pl.BlockSpec((1, 1, 2 * D_EXPERT), lambda i, be, nb: (be[blk(i, be, nb)], 0, 0)),
            pl.BlockSpec((1, D_EXPERT, d), lambda i, be, nb: (be[blk(i, be, nb)], 0, 0)),
            pl.BlockSpec((1, 1, d), lambda i, be, nb: (be[blk(i, be, nb)], 0, 0)),
        ],
        out_specs=pl.BlockSpec((EXPERT_BLOCK, d), lambda i, be, nb: (blk(i, be, nb), 0)),
    )
    return pl.pallas_call(
        _expert_kernel,
        out_shape=jax.ShapeDtypeStruct((slots, d), BF16),
        grid_spec=grid_spec,
        compiler_params=_params("arbitrary"),
        name="expert_blocks",
    )(block_e, n_used, xb, wgu, bgu.reshape(N_EXPERTS, 1, -1), wd, bd.reshape(N_EXPERTS, 1, -1))


ROUTE_DEST, ROUTE_PROB = 0, TOP_K


def _route_kernel(lg_ref, out_ref, cnt_ref, run_ref, base_ref):
    phase = pl.program_id(0)
    i = pl.program_id(1)
    tm = lg_ref.shape[0]
    lane = lax.broadcasted_iota(jnp.int32, (1, LANES), 1).astype(F32)
    cur = jnp.where(lane < N_EXPERTS, lg_ref[...], -jnp.inf)
    vals, picks = [], []
    member = jnp.zeros((tm, LANES), F32)
    for _ in range(TOP_K):
        v = jnp.max(cur, -1, keepdims=True)
        pick = lane == jnp.min(jnp.where(cur == v, lane, float(LANES)), -1, keepdims=True)
        member = member + pick.astype(F32)
        cur = jnp.where(pick, -jnp.inf, cur)
        vals.append(v)
        picks.append(pick)
    tile_counts = jnp.sum(member, 0, keepdims=True)

    @pl.when(phase == 0)
    def _():
        @pl.when(i == 0)
        def _():
            cnt_ref[...] = jnp.zeros_like(cnt_ref)
        cnt_ref[...] = cnt_ref[...] + tile_counts

    @pl.when(phase == 1)
    def _():
        @pl.when(i == 0)
        def _():
            blocks = jnp.floor((cnt_ref[...] + (EXPERT_BLOCK - 1)) * (1.0 / EXPERT_BLOCK))
            r = lax.broadcasted_iota(jnp.int32, (LANES, LANES), 0)
            c = lax.broadcasted_iota(jnp.int32, (LANES, LANES), 1)
            before = _split_dot_lhs(jnp.broadcast_to(blocks, (SUBLANES, LANES)), (r < c).astype(BF16))
            base_ref[...] = before[0:1] * float(EXPERT_BLOCK)
            run_ref[...] = jnp.zeros_like(run_ref)

        r = lax.broadcasted_iota(jnp.int32, (tm, tm), 0)
        c = lax.broadcasted_iota(jnp.int32, (tm, tm), 1)
        earlier = jnp.dot((c < r).astype(BF16), member.astype(BF16), preferred_element_type=F32)
        slot = base_ref[...] + run_ref[...] + earlier
        denom = 1.0
        exps = [1.0]
        for k in range(1, TOP_K):
            exps.append(jnp.exp(vals[k] - vals[0]))
            denom = denom + exps[k]
        out_lane = lax.broadcasted_iota(jnp.int32, (1, LANES), 1)
        row = jnp.zeros((tm, LANES), F32)
        for k in range(TOP_K):
            dest = jnp.sum(jnp.where(picks[k], slot, 0.0), -1, keepdims=True)
            row = jnp.where(out_lane == ROUTE_DEST + k, dest, row)
            row = jnp.where(out_lane == ROUTE_PROB + k, exps[k] / denom, row)
        out_ref[...] = row
        run_ref[...] = run_ref[...] + tile_counts


def route(logits):
    n = logits.shape[0]
    tm = ROW_TILE
    return pl.pallas_call(
        _route_kernel,
        out_shape=(jax.ShapeDtypeStruct((n, LANES), F32), jax.ShapeDtypeStruct((1, LANES), F32)),
        grid=(2, n // tm),
        in_specs=[pl.BlockSpec((tm, LANES), lambda ph, i: (i, 0))],
        out_specs=(pl.BlockSpec((tm, LANES), lambda ph, i: (i * ph, 0)), pl.BlockSpec((1, LANES), lambda ph, i: (0, 0))),
        scratch_shapes=[pltpu.VMEM((1, LANES), F32), pltpu.VMEM((1, LANES), F32)],
        compiler_params=_params("arbitrary", "arbitrary"),
        name="route",
    )(logits)


def _combine_norm_kernel(x_ref, y0_ref, y1_ref, y2_ref, y3_ref, rt_ref, mod_ref, ln_ref, o_ref):
    f = None
    for k, y_ref in enumerate((y0_ref, y1_ref, y2_ref, y3_ref)):
        term = rt_ref[0, :, ROUTE_PROB + k:ROUTE_PROB + k + 1] * y_ref[0].astype(F32)
        f = term if f is None else f + term
    x2 = DN_ALPHA * x_ref[0] + mod_ref[0, 0, 5:6] * f
    o_ref[0] = _normalize(x2, 1e-5) * ln_ref[0:1] + ln_ref[1:2]


def combine_norm(x1, ys, table, mods, ln, first_tile):
    b, t, d = x1.shape
    tm = ROW_TILE
    n_tiles = t // tm - first_tile
    y_spec = pl.BlockSpec((1, tm, d), lambda bi, i: (bi, i, 0))
    return pl.pallas_call(
        _combine_norm_kernel,
        out_shape=jax.ShapeDtypeStruct((b, n_tiles * tm, d), F32),
        grid=(b, n_tiles),
        in_specs=[
            pl.BlockSpec((1, tm, d), lambda bi, i: (bi, i + first_tile, 0)),
            y_spec, y_spec, y_spec, y_spec,
            pl.BlockSpec((1, tm, LANES), lambda bi, i: (bi, i, 0)),
            pl.BlockSpec((1, 1, 6, d), lambda bi, i: (bi, jnp.minimum(i + first_tile, 1), 0, 0)),
            pl.BlockSpec((2, d), lambda bi, i: (0, 0)),
        ],
        out_specs=pl.BlockSpec((1, tm, d), lambda bi, i: (bi, i, 0)),
        compiler_params=_params("parallel", "parallel"),
        name="combine_norm",
    )(x1, *ys, table, mods, ln)


def _proj_columns():
    o = IN_OFFS
    cols = list(range(o[0], o[7]))
    xbc, dt = o[7], o[8]
    for g in range(SSM_G):
        cols += [xbc + g * SSM_GW + i for i in range(SSM_GW)]
        cols += [xbc + SSM_W + g * SSM_N + i for i in range(SSM_N)]
        cols += [xbc + SSM_W + SSM_G * SSM_N + g * SSM_N + i for i in range(SSM_N)]
        cols += [dt + d * SSM_HEADS + g * SSM_K + k for d in range(2) for k in range(SSM_K)]
        cols += [-1] * (BRANCH_W - SSD_XBC - 2 * SSM_K)
    return np.asarray(cols, np.int32)


def _ssd_conv_columns():
    cols = []
    for g in range(SSM_G):
        cols += [g * SSM_GW + i for i in range(SSM_GW)]
        cols += [SSM_W + g * SSM_N + i for i in range(SSM_N)]
        cols += [SSM_W + SSM_G * SSM_N + g * SSM_N + i for i in range(SSM_N)]
    return np.asarray(cols, np.int32)


def _rope_tables(n_lat, n_ctx):
    rows = n_lat // GRID_W
    row = jnp.repeat(jnp.arange(rows, dtype=F32), GRID_W)
    col = jnp.tile(jnp.arange(GRID_W, dtype=F32), rows)
    inv = ROPE_BASE ** (-jnp.arange(ROPE_F, dtype=F32) / ROPE_F)
    ang = jnp.stack([row[:, None] * inv, col[:, None] * inv], axis=1)
    cos = jnp.broadcast_to(jnp.cos(ang)[:, None, :, None, :], (n_lat, 2, 2, 2, ROPE_F))
    sin = jnp.sin(ang)[:, None, :, None, :] * jnp.asarray([-1.0, 1.0], F32)[None, None, None, :, None]
    sin = jnp.broadcast_to(sin, (n_lat, 2, 2, 2, ROPE_F))
    hw = 2 * ATT_DIM
    cos = jnp.concatenate([jnp.ones((n_ctx, hw), F32), cos.reshape(n_lat, hw)], axis=0)
    sin = jnp.concatenate([jnp.zeros((n_ctx, hw), F32), sin.reshape(n_lat, hw)], axis=0)
    return cos, sin


def _block_diag(w):
    n, r, c = w.shape
    eye = jnp.eye(n, dtype=w.dtype)
    return (w[:, :, None, :] * eye[:, None, :, None]).reshape(n * r, n * c)


def _lane_pad(v, width=LANES):
    return jnp.pad(v, [(0, 0)] * (v.ndim - 1) + [(0, width - v.shape[-1])])


def _slot_tables(dest, counts):
    n = dest.shape[0]
    a = n * TOP_K
    n_blocks = -(-a // EXPERT_BLOCK) + N_EXPERTS
    pblocks = (counts + EXPERT_BLOCK - 1) // EXPERT_BLOCK
    pend_b = jnp.cumsum(pblocks)
    start = jnp.cumsum(counts) - counts
    blk = jnp.arange(n_blocks, dtype=jnp.int32)
    block_e = jnp.minimum(jnp.sum((blk[:, None] >= pend_b[None, :]).astype(jnp.int32), axis=1), N_EXPERTS - 1)
    n_used = pend_b[N_EXPERTS - 1:].astype(jnp.int32)
    tok = jnp.arange(a, dtype=jnp.int32) // TOP_K
    _, sorted_tok = lax.sort_key_val(dest.reshape(a), tok)
    sorted_tok = jnp.concatenate([sorted_tok, jnp.zeros((EXPERT_BLOCK,), jnp.int32)])
    onehot = (block_e[:, None] == jnp.arange(N_EXPERTS)[None, :]).astype(jnp.int32)
    first_blk = jnp.sum(onehot * (pend_b - pblocks)[None, :], axis=1)
    win = jnp.sum(onehot * start[None, :], axis=1) + (blk - first_blk) * EXPERT_BLOCK
    n_valid = jnp.sum(onehot * counts[None, :], axis=1) - (blk - first_blk) * EXPERT_BLOCK
    used = blk < n_used[0]
    win = jnp.where(used, jnp.clip(win, 0, a), 0)
    window = jax.vmap(lambda w: lax.dynamic_slice(sorted_tok, (w,), (EXPERT_BLOCK,)))(win)
    valid = used[:, None] & (jnp.arange(EXPERT_BLOCK)[None, :] < n_valid[:, None])
    slot_tok = jnp.where(valid, window, 0).reshape(-1)
    return block_e, n_used, slot_tok


def kernel(x, c, ctx, c_ctx, w_ada, b_ada, w_in, rnn_conv_w, rnn_conv_b, rnn_wa, rnn_ba, rnn_wx, rnn_bx, rnn_lam, att_lambda, att_subln, pool_w, pool_b, pool_scale, ssm_conv_w, ssm_conv_b, ssm_dt_bias, ssm_a_log, ssm_d, ssm_norm, w_branch, w_out, ln1_g, ln1_b, ln2_g, ln2_b, router_w, router_b, w_gate_up, b_gate_up, w_down, b_down):
    bsz, n_lat, d = x.shape
    n_ctx = ctx.shape[1]
    total = n_ctx + n_lat
    assert n_ctx == ROW_TILE and n_lat % ROW_TILE == 0 and n_lat % GRID_W == 0
    ctx_tiles = n_ctx // ROW_TILE
    cos_t, sin_t = _rope_tables(n_lat, n_ctx)

    pcols = _proj_columns()
    w_main = jnp.where(pcols[None, None, :] >= 0, jnp.take(w_in, jnp.maximum(pcols, 0), axis=2), 0.0).astype(BF16)
    w_gates = w_in[:, :, IN_OFFS[9]:].astype(BF16)
    w_branch_b = w_branch.astype(BF16)
    w_out_b = w_out.astype(BF16)
    wgu_b = w_gate_up.astype(BF16)
    wd_b = w_down.astype(BF16)
    w_router = _lane_pad(router_w).astype(BF16)
    b_router = _lane_pad(router_b)[:, None, :]
    n_cg = RNN_W // LANES
    bpg = LANES // RNN_BLOCK
    gate_blocks = jnp.stack([rnn_wa[:, 0], rnn_wx[:, 0], rnn_wa[:, 1], rnn_wx[:, 1]], axis=1)
    gate_blocks = gate_blocks.reshape(DEPTH, 4, n_cg, bpg, RNN_BLOCK, RNN_BLOCK)
    rnn_wg = jax.vmap(jax.vmap(jax.vmap(_block_diag)))(gate_blocks)
    rnn_wg = rnn_wg.transpose(0, 2, 3, 1, 4).reshape(DEPTH, n_cg, LANES, 4 * LANES).astype(BF16)
    gate_bias = jnp.stack([rnn_ba[:, 0], rnn_bx[:, 0], rnn_ba[:, 1], rnn_bx[:, 1]], axis=1)
    rnn_bg = gate_bias.reshape(DEPTH, 4, n_cg, LANES).transpose(0, 2, 1, 3).reshape(DEPTH, n_cg, 1, 4 * LANES)
    rnn_sp = jax.nn.softplus(-rnn_lam).reshape(DEPTH, 2, n_cg, LANES).transpose(0, 2, 1, 3).reshape(DEPTH, n_cg, 1, 2 * LANES)
    ccols = _ssd_conv_columns()
    ssd_cw = jnp.take(ssm_conv_w, ccols, axis=2).reshape(DEPTH, CONV_W, SSM_G, SSD_XBC).transpose(0, 2, 1, 3)
    ssd_cb = jnp.take(ssm_conv_b, ccols, axis=1).reshape(DEPTH, SSM_G, 1, SSD_XBC)
    per_group = lambda v: v.reshape(DEPTH, 2, SSM_G, SSM_K).transpose(0, 2, 1, 3).reshape(DEPTH, SSM_G, 1, 2 * SSM_K)
    ssd_dtb = _lane_pad(per_group(ssm_dt_bias))
    ssd_alog = _lane_pad(per_group(ssm_a_log))
    ssd_dsk = jnp.repeat((ssm_d[:, 0] + ssm_d[:, 1]).reshape(DEPTH, SSM_G, 1, SSM_K), SSM_P, axis=-1)
    ssd_nw = ssm_norm.reshape(DEPTH, SSM_G, 1, SSM_GW)
    pool_wb = pool_w.astype(BF16)

    xs = jnp.concatenate([ctx, x], axis=1)
    out = None
    for li in range(DEPTH):
        need_ctx = li < DEPTH - 1
        first = 0 if need_ctx else ctx_tiles
        mod_l = jax.nn.silu(c) @ w_ada[li] + b_ada[li]
        mod_c = jnp.broadcast_to(jax.nn.silu(c_ctx) @ w_ada[li] + b_ada[li], mod_l.shape)
        mods = jnp.stack([mod_c, mod_l], axis=1).reshape(bsz, 2, 6, d)

        proj = in_proj(xs, mods, w_main[li], n_ctx, 768 if total % 768 == 0 else ROW_TILE, BRANCH_W * 3 if PROJ_W % (BRANCH_W * 3) == 0 else BRANCH_W)
        y_rnn = rglru(proj, rnn_conv_w[li], rnn_conv_b[li][None], rnn_wg[li], rnn_bg[li], rnn_sp[li], n_ctx)
        lam_init = 0.8 - 0.6 * math.exp(-0.3 * li)
        lv = att_lambda[li]
        lam = jnp.exp(jnp.sum(lv[0] * lv[1])) - jnp.exp(jnp.sum(lv[2] * lv[3])) + lam_init
        y_att = diff_attention(proj, cos_t, sin_t, att_subln[li], lam, jnp.asarray(1.0 - lam_init, F32), n_ctx, need_ctx)
        y_pool = pool(proj, pool_wb[li], pool_b[li][None], pool_scale[li][None], n_ctx)
        y_ssd = ssd(proj, ssd_cw[li], ssd_cb[li], ssd_dtb[li], ssd_alog[li], ssd_dsk[li], ssd_nw[li], n_ctx)
        ln1 = jnp.stack([ln1_g[li], ln1_b[li]])
        x1, h2, logits = merge(xs, mods, (y_rnn, y_att, y_pool, y_ssd), w_gates[li], w_branch_b[li], w_out_b[li], ln1,
                               w_router[li], b_router[li], first)

        n_rout = total - first * ROW_TILE
        table, counts = route(logits[:, first * ROW_TILE:].reshape(bsz * n_rout, LANES))
        dest = table[:, ROUTE_DEST:ROUTE_DEST + TOP_K].astype(jnp.int32)
        block_e, n_used, slot_tok = _slot_tables(dest, counts[0, :N_EXPERTS].astype(jnp.int32))
        slot_row = slot_tok + (slot_tok // n_rout + 1) * (first * ROW_TILE)
        xb = h2.reshape(bsz * total, d)[slot_row]
        yb = expert_blocks(block_e, n_used, xb, wgu_b[li], b_gate_up[li], wd_b[li], b_down[li])
        ys = [yb[dest[:, k]].reshape(bsz, n_rout, d) for k in range(TOP_K)]
        ln2 = jnp.stack([ln2_g[li], ln2_b[li]])
        out = combine_norm(x1, ys, table.reshape(bsz, n_rout, LANES), mods, ln2, first)
        xs = out
    return out
```

```python
import functools
import math

import jax
import jax.numpy as jnp
import numpy as np
from jax import lax
from jax.experimental import pallas as pl
from jax.experimental.pallas import tpu as pltpu

D_MODEL = 1024
DEPTH = 4
GRID_W = 64
N_BRANCH = 4
BRANCH_W = D_MODEL // 2
RNN_W = BRANCH_W
RNN_BLOCK = 64
RG_C = 8.0
CONV_W = 4
ATT_DIM = 64
ATT_HEADS = BRANCH_W // (2 * ATT_DIM)
ATT_V_W = ATT_HEADS * 2 * ATT_DIM
ROPE_F = ATT_DIM // 4
ROPE_BASE = 10000.0
POOL_W = BRANCH_W
POOL_WINDOWS = (2, 4, 8, 16)
POOL_GROUPS = len(POOL_WINDOWS)
POOL_GW = POOL_W // POOL_GROUPS
SSM_W = BRANCH_W
SSM_P = 64
SSM_HEADS = SSM_W // SSM_P
SSM_G = 2
SSM_K = SSM_HEADS // SSM_G
SSM_N = 64
SSM_GW = SSM_W // SSM_G
SSM_CHUNK = 128
N_EXPERTS = 32
TOP_K = 4
D_EXPERT = D_MODEL
SWIGLU_LIMIT = 7.0
SWIGLU_ALPHA = 1.702
EXPERT_BLOCK = 256
DN_ALPHA = (2 * DEPTH) ** 0.25
IN_WIDTHS = (RNN_W, RNN_W, BRANCH_W, BRANCH_W, ATT_V_W, POOL_W, SSM_W, SSM_W + 2 * SSM_G * SSM_N, 2 * SSM_HEADS, N_BRANCH * D_MODEL)
IN_OFFS = tuple(int(v) for v in np.cumsum((0,) + IN_WIDTHS))

LANES = 128
SUBLANES = 8
VMEM_LIMIT = 48 * 1024 * 1024
ROW_TILE = 256
EXPERT_VMEM_LIMIT = 54 * 1024 * 1024

COL_RX, COL_RG, COL_Q, COL_K, COL_V, COL_PU, COL_Z, COL_SSD = 0, 1, 2, 3, 4, 5, 6, 7
PROJ_W = (COL_SSD + SSM_G) * BRANCH_W
SSD_XBC = SSM_GW + 2 * SSM_N

BF16 = jnp.bfloat16
F32 = jnp.float32


def _params(*sem):
    return pltpu.CompilerParams(dimension_semantics=sem, vmem_limit_bytes=VMEM_LIMIT)


def _normalize(x, eps):
    mu = jnp.mean(x, -1, keepdims=True)
    xc = x - mu
    var = jnp.mean(xc * xc, -1, keepdims=True)
    return xc * lax.rsqrt(var + eps)


def _segment_valid(t, off, n_ctx, total):
    lo = jnp.where(t < n_ctx, 0, n_ctx)
    hi = jnp.where(t < n_ctx, n_ctx, total)
    return (t + off >= lo) & (t + off < hi)


def _shift_rows(u, off, t, n_ctx):
    total = u.shape[0]
    rolled = pltpu.roll(u, (-off) % total, 0)
    return jnp.where(_segment_valid(t, off, n_ctx, total), rolled, 0.0)


def _centred_conv(u, w_ref, b_ref, n_ctx):
    t = lax.broadcasted_iota(jnp.int32, (u.shape[0], 1), 0)
    left = CONV_W // 2
    out = b_ref[...] + _shift_rows(u, -left, t, n_ctx) * w_ref[0:1, :]
    for k in range(1, CONV_W):
        tap = u if k == left else _shift_rows(u, k - left, t, n_ctx)
        out = out + tap * w_ref[k:k + 1, :]
    return out


def _in_proj_kernel(x_ref, mod_ref, w_ref, o_ref, h_ref, *, n_ctx):
    @pl.when(pl.program_id(2) == 0)
    def _():
        tm = x_ref.shape[1]
        row = pl.program_id(1) * tm + lax.broadcasted_iota(jnp.int32, (tm, 1), 0)
        is_ctx = row < n_ctx
        shift = jnp.where(is_ctx, mod_ref[0, 0, 0:1, :], mod_ref[0, 1, 0:1, :])
        scale = jnp.where(is_ctx, mod_ref[0, 0, 1:2, :], mod_ref[0, 1, 1:2, :])
        h_ref[...] = (_normalize(x_ref[0], 1e-6) * (1.0 + scale) + shift).astype(BF16)

    o_ref[0] = jnp.dot(h_ref[...], w_ref[...], preferred_element_type=F32)


def in_proj(x, mods, w, n_ctx, tm, tn):
    b, t, d = x.shape
    n = w.shape[1]
    return pl.pallas_call(
        functools.partial(_in_proj_kernel, n_ctx=n_ctx),
        out_shape=jax.ShapeDtypeStruct((b, t, n), F32),
        grid=(b, t // tm, n // tn),
        in_specs=[
            pl.BlockSpec((1, tm, d), lambda bi, i, j: (bi, i, 0)),
            pl.BlockSpec((1, 2, 6, d), lambda bi, i, j: (bi, 0, 0, 0)),
            pl.BlockSpec((d, tn), lambda bi, i, j: (0, j)),
        ],
        out_specs=pl.BlockSpec((1, tm, tn), lambda bi, i, j: (bi, i, j)),
        scratch_shapes=[pltpu.VMEM((tm, d), BF16)],
        compiler_params=_params("parallel", "parallel", "arbitrary"),
        name="in_proj",
    )(x, mods, w)


SCAN_ROWS = 64


def _tile_scan(a, b, reverse):
    rows = a.shape[0]
    sub = lax.broadcasted_iota(jnp.int32, (rows, 1), 0) % SUBLANES
    s = 1
    while s < SUBLANES:
        shift = (rows - s) if reverse else s
        keep = (sub + s < SUBLANES) if reverse else (sub >= s)
        a_sh = pltpu.roll(a, shift, 0)
        b_sh = pltpu.roll(b, shift, 0)
        b = jnp.where(keep, a * b_sh + b, b)
        a = jnp.where(keep, a * a_sh, a)
        s *= 2
    return a, b


def _rglru_kernel(rx_ref, rg_ref, cw_ref, cb_ref, wg_ref, bg_ref, sp_ref, o_ref, a_ref, b_ref, y_ref, *, n_ctx):
    total = rx_ref.shape[1]
    cw = rx_ref.shape[2]
    u = _centred_conv(rx_ref[0], cw_ref, cb_ref, n_ctx)
    g = jnp.dot(u.astype(BF16), wg_ref[0], preferred_element_type=F32) + bg_ref[0]
    n_tiles = SCAN_ROWS // SUBLANES

    for d in range(2):
        reverse = d == 1
        r = jax.nn.sigmoid(g[:, (2 * d) * cw:(2 * d + 1) * cw])
        i = jax.nn.sigmoid(g[:, (2 * d + 1) * cw:(2 * d + 2) * cw])
        log_a = -RG_C * r * sp_ref[0, :, d * cw:(d + 1) * cw]
        a = jnp.exp(log_a)
        a_ref[...] = a
        b_ref[...] = jnp.sqrt((1.0 - a) * (1.0 + a)) * (i * u)

        def step(it, h, lo, n_steps, reverse=reverse):
            blk = (n_steps - 1 - it) if reverse else it
            r0 = pl.multiple_of(lo + blk * SCAN_ROWS, SCAN_ROWS)
            a, b = _tile_scan(a_ref[pl.ds(r0, SCAN_ROWS), :], b_ref[pl.ds(r0, SCAN_ROWS), :], reverse)
            outs = [None] * n_tiles
            for j in (range(n_tiles - 1, -1, -1) if reverse else range(n_tiles)):
                sl = slice(j * SUBLANES, (j + 1) * SUBLANES)
                hj = b[sl] + a[sl] * h
                h = hj[0:1] if reverse else hj[SUBLANES - 1:SUBLANES]
                outs[j] = hj
            hs = jnp.concatenate(outs, axis=0)
            if reverse:
                y_ref[pl.ds(r0, SCAN_ROWS), :] = y_ref[pl.ds(r0, SCAN_ROWS), :] + hs
            else:
                y_ref[pl.ds(r0, SCAN_ROWS), :] = hs
            return h

        h0 = jnp.zeros((1, cw), F32)
        h_c = lax.fori_loop(0, n_ctx // SCAN_ROWS, functools.partial(step, lo=0, n_steps=n_ctx // SCAN_ROWS), h0)
        n_lat_steps = (total - n_ctx) // SCAN_ROWS
        lax.fori_loop(0, n_lat_steps, functools.partial(step, lo=n_ctx, n_steps=n_lat_steps), h_c)

    o_ref[0] = (jax.nn.gelu(rg_ref[0]) * y_ref[...]).astype(BF16)


def rglru(proj, conv_w, conv_b, wg, bg, sp, n_ctx):
    b, t, _ = proj.shape
    cw = LANES
    n_cg = RNN_W // cw
    per_block = BRANCH_W // cw
    return pl.pallas_call(
        functools.partial(_rglru_kernel, n_ctx=n_ctx),
        out_shape=jax.ShapeDtypeStruct((b, t, RNN_W), BF16),
        grid=(b, n_cg),
        in_specs=[
            pl.BlockSpec((1, t, cw), lambda bi, c: (bi, 0, COL_RX * per_block + c)),
            pl.BlockSpec((1, t, cw), lambda bi, c: (bi, 0, COL_RG * per_block + c)),
            pl.BlockSpec((CONV_W, cw), lambda bi, c: (0, c)),
            pl.BlockSpec((1, cw), lambda bi, c: (0, c)),
            pl.BlockSpec((1, cw, 4 * cw), lambda bi, c: (c, 0, 0)),
            pl.BlockSpec((1, 1, 4 * cw), lambda bi, c: (c, 0, 0)),
            pl.BlockSpec((1, 1, 2 * cw), lambda bi, c: (c, 0, 0)),
        ],
        out_specs=pl.BlockSpec((1, t, cw), lambda bi, c: (bi, 0, c)),
        scratch_shapes=[pltpu.VMEM((t, cw), F32), pltpu.VMEM((t, cw), F32), pltpu.VMEM((t, cw), F32)],
        compiler_params=_params("parallel", "parallel"),
        name="rglru",
    )(proj, proj, conv_w, conv_b, wg, bg, sp)


def _pool_kernel(u_ref, w_ref, b_ref, s_ref, o_ref, *, n_ctx):
    total = u_ref.shape[1]
    t = lax.broadcasted_iota(jnp.int32, (total, 1), 0)
    for gi, win in enumerate(POOL_WINDOWS):
        cols = slice(gi * POOL_GW, (gi + 1) * POOL_GW)
        u = u_ref[0, :, cols]
        acc = u
        cnt = jnp.ones((total, 1), F32)
        for off in range(-(win // 2), win - win // 2):
            if off == 0:
                continue
            acc = acc + _shift_rows(u, off, t, n_ctx)
            cnt = cnt + _segment_valid(t, off, n_ctx, total).astype(F32)
        mix = acc / cnt - u
        y = jnp.dot(mix.astype(BF16), w_ref[gi], preferred_element_type=F32) + b_ref[:, cols]
        o_ref[0, :, cols] = (y * s_ref[:, cols]).astype(BF16)


def pool(proj, w, bias, scale, n_ctx):
    b, t, _ = proj.shape
    return pl.pallas_call(
        functools.partial(_pool_kernel, n_ctx=n_ctx),
        out_shape=jax.ShapeDtypeStruct((b, t, POOL_W), BF16),
        grid=(b,),
        in_specs=[
            pl.BlockSpec((1, t, POOL_W), lambda bi: (bi, 0, COL_PU)),
            pl.BlockSpec((POOL_GROUPS, POOL_GW, POOL_GW), lambda bi: (0, 0, 0)),
            pl.BlockSpec((1, POOL_W), lambda bi: (0, 0)),
            pl.BlockSpec((1, POOL_W), lambda bi: (0, 0)),
        ],
        out_specs=pl.BlockSpec((1, t, POOL_W), lambda bi: (bi, 0, 0)),
        compiler_params=_params("parallel"),
        name="pool",
    )(proj, w, bias, scale)


def _split_dot(tri, v):
    hi = v.astype(BF16)
    r1 = v - hi.astype(F32)
    mid = r1.astype(BF16)
    lo = (r1 - mid.astype(F32)).astype(BF16)
    return (jnp.dot(tri, hi, preferred_element_type=F32) + jnp.dot(tri, mid, preferred_element_type=F32)
            + jnp.dot(tri, lo, preferred_element_type=F32))


def _split_dot_lhs(v, tri):
    hi = v.astype(BF16)
    r1 = v - hi.astype(F32)
    mid = r1.astype(BF16)
    lo = (r1 - mid.astype(F32)).astype(BF16)
    return (jnp.dot(hi, tri, preferred_element_type=F32) + jnp.dot(mid, tri, preferred_element_type=F32)
            + jnp.dot(lo, tri, preferred_element_type=F32))


def _per_head(cols, width):
    head = lax.broadcasted_iota(jnp.int32, (1, width), 1) // SSM_P
    out = cols[SSM_K - 1]
    for k in range(SSM_K - 2, -1, -1):
        out = jnp.where(head == k, cols[k], out)
    return out


def _ssd_kernel(blk_ref, z_ref, cw_ref, cb_ref, dtb_ref, alog_ref, dsk_ref, nw_ref, o_ref,
                u_ref, dt_ref, adt_ref, tr_ref, y_ref, st_ref, *, n_ctx):
    total = blk_ref.shape[1]
    ck = SSM_CHUNK
    n_chunks = total // ck
    ctx_chunks = n_ctx // ck
    u_ref[...] = jax.nn.silu(_centred_conv(blk_ref[0, :, 0:SSD_XBC], cw_ref.at[0], cb_ref.at[0], n_ctx))
    dt = jax.nn.softplus(blk_ref[0, :, SSD_XBC:SSD_XBC + LANES] + dtb_ref[0])
    dt_ref[...] = dt
    adt_ref[...] = dt * (-jnp.exp(alog_ref[0]))
    for c in range(n_chunks):
        rows = slice(c * ck, (c + 1) * ck)
        tr_ref[c, 0:ck, :] = u_ref[rows, SSM_GW:SSM_GW + 2 * SSM_N].T
        tr_ref[c, ck:2 * ck, :] = adt_ref[rows, :].T

    ri = lax.broadcasted_iota(jnp.int32, (ck, ck), 0)
    ci = lax.broadcasted_iota(jnp.int32, (ck, ck), 1)
    lower = (ci <= ri)
    lower_b = lower.astype(BF16)
    upper_b = (ci >= ri).astype(BF16)
    lane = lax.broadcasted_iota(jnp.int32, (1, LANES), 1)
    head_w = lax.broadcasted_iota(jnp.int32, (1, SSM_GW), 1) // SSM_P
    srow = lax.broadcasted_iota(jnp.int32, (ck, 1), 0)

    def chunk(c, d):
        reverse = d == 1
        tri_col, tri_row = (upper_b, lower_b) if reverse else (lower_b, upper_b)
        causal = (ci >= ri) if reverse else lower
        r0 = pl.multiple_of(c * ck, ck)
        xs = u_ref[pl.ds(r0, ck), 0:SSM_GW]
        bc = u_ref[pl.ds(r0, ck), SSM_GW:SSM_GW + 2 * SSM_N]
        tr = tr_ref[c, 0:ck, :]
        c_lo = jnp.where(lane < SSM_N, pltpu.roll(bc, SSM_N, 1), 0.0).astype(BF16)
        cb = jnp.dot(c_lo, tr.astype(BF16), preferred_element_type=F32)
        cs_col = _split_dot(tri_col, adt_ref[pl.ds(r0, ck), :])
        cs_row = _split_dot_lhs(tr_ref[c, ck:2 * ck, :], tri_row)
        edge = ck - 1 if not reverse else 0
        dtc = dt_ref[pl.ds(r0, ck), :]
        cols, tots, dts = [], [], []
        y = jnp.zeros((ck, SSM_GW), F32)
        for k in range(SSM_K):
            j = d * SSM_K + k
            cols.append(cs_col[:, j:j + 1])
            tots.append(cs_col[edge:edge + 1, j:j + 1])
            dts.append(dtc[:, j:j + 1])
        xdt = xs * _per_head(dts, SSM_GW)
        for k in range(SSM_K):
            j = d * SSM_K + k
            seg = cols[k] - cs_row[j:j + 1, :]
            m = (cb * jnp.where(causal, jnp.exp(seg), 0.0)).astype(BF16)
            y = y + jnp.dot(m, jnp.where(head_w == k, xdt, 0.0).astype(BF16), preferred_element_type=F32)
        col_w = _per_head(cols, SSM_GW)
        tot_w = _per_head(tots, SSM_GW)
        state = st_ref[d]
        y = y + jnp.dot(c_lo, state.astype(BF16), preferred_element_type=F32) * jnp.exp(col_w)
        xd = (xdt * jnp.exp(tot_w - col_w)).astype(BF16)
        new = jnp.exp(tot_w) * state + jnp.dot(tr.astype(BF16), xd, preferred_element_type=F32)
        st_ref[d] = jnp.where(srow < SSM_N, new, 0.0)
        y_ref[d, pl.ds(r0, ck), :] = y

    def both(it, carry, lo, n_steps):
        chunk(lo + it, 0)
        chunk(lo + n_steps - 1 - it, 1)
        return carry

    st_ref[...] = jnp.zeros_like(st_ref)
    lax.fori_loop(0, ctx_chunks, functools.partial(both, lo=0, n_steps=ctx_chunks), 0)
    lax.fori_loop(0, n_chunks - ctx_chunks, functools.partial(both, lo=ctx_chunks, n_steps=n_chunks - ctx_chunks), 0)

    y = y_ref[0] + y_ref[1] + dsk_ref[0] * u_ref[:, 0:SSM_GW]
    g = y * jax.nn.silu(z_ref[0])
    g = g * lax.rsqrt(jnp.mean(g * g, -1, keepdims=True) + 1e-5)
    o_ref[0] = (g * nw_ref[0]).astype(BF16)


def ssd(proj, conv_w, conv_b, dt_bias, a_log, d_skip, norm_w, n_ctx):
    b, t, _ = proj.shape
    n_chunks = t // SSM_CHUNK
    z_per = BRANCH_W // SSM_GW
    return pl.pallas_call(
        functools.partial(_ssd_kernel, n_ctx=n_ctx),
        out_shape=jax.ShapeDtypeStruct((b, t, SSM_W), BF16),
        grid=(b, SSM_G),
        in_specs=[
            pl.BlockSpec((1, t, BRANCH_W), lambda bi, g: (bi, 0, COL_SSD + g)),
            pl.BlockSpec((1, t, SSM_GW), lambda bi, g: (bi, 0, COL_Z * z_per + g)),
            pl.BlockSpec((1, CONV_W, SSD_XBC), lambda bi, g: (g, 0, 0)),
            pl.BlockSpec((1, 1, SSD_XBC), lambda bi, g: (g, 0, 0)),
            pl.BlockSpec((1, 1, LANES), lambda bi, g: (g, 0, 0)),
            pl.BlockSpec((1, 1, LANES), lambda bi, g: (g, 0, 0)),
            pl.BlockSpec((1, 1, SSM_GW), lambda bi, g: (g, 0, 0)),
            pl.BlockSpec((1, 1, SSM_GW), lambda bi, g: (g, 0, 0)),
        ],
        out_specs=pl.BlockSpec((1, t, SSM_GW), lambda bi, g: (bi, 0, g)),
        scratch_shapes=[
            pltpu.VMEM((t, SSD_XBC), F32),
            pltpu.VMEM((t, LANES), F32),
            pltpu.VMEM((t, LANES), F32),
            pltpu.VMEM((n_chunks, 2 * SSM_CHUNK, SSM_CHUNK), F32),
            pltpu.VMEM((2, t, SSM_GW), F32),
            pltpu.VMEM((2, SSM_CHUNK, SSM_GW), F32),
        ],
        compiler_params=_params("parallel", "parallel"),
        name="ssd",
    )(proj, proj, conv_w, conv_b, dt_bias, a_log, d_skip, norm_w)


KEY_CHUNK = 384


def _rope(x, cos, sin):
    lane = lax.broadcasted_iota(jnp.int32, (1, x.shape[1]), 1)
    partner = jnp.where(lane % (2 * ROPE_F) < ROPE_F, pltpu.roll(x, x.shape[1] - ROPE_F, 1), pltpu.roll(x, ROPE_F, 1))
    return x * cos + partner * sin


def _diff_attn_kernel(par_ref, q_ref, k_ref, v_ref, cq_ref, sq_ref, ck_ref, sk_ref, g_ref, o_ref, kb_ref, vb_ref,
                      *, n_ctx, first_tile):
    lam = par_ref[0]
    out_scale = par_ref[1]

    @pl.when(pl.program_id(2) == 0)
    def _():
        kb_ref[...] = _rope(k_ref[0], ck_ref[...], sk_ref[...]).astype(BF16)
        vb_ref[...] = v_ref[0].astype(BF16)

    q = (_rope(q_ref[0], cq_ref[...], sq_ref[...]) * (ATT_DIM ** -0.5 * math.log2(math.e))).astype(BF16)
    lane = lax.broadcasted_iota(jnp.int32, q.shape, 1)
    dims = (((1,), (1,)), ((), ()))
    q1 = jnp.where(lane < ATT_DIM, q, jnp.zeros_like(q))
    q2 = jnp.where(lane >= ATT_DIM, q, jnp.zeros_like(q))

    def attend(n_keys):
        kc = min(KEY_CHUNK, n_keys)

        def softmax_v(qh):
            m = l = acc = None
            for c0 in range(0, n_keys, kc):
                s = lax.dot_general(qh, kb_ref[c0:c0 + kc, :], dims, preferred_element_type=F32)
                mc = jnp.max(s, -1, keepdims=True)
                m_new = mc if m is None else jnp.maximum(m, mc)
                e = jnp.exp2(s - m_new)
                pv = jnp.dot(e.astype(BF16), vb_ref[c0:c0 + kc, :], preferred_element_type=F32)
                if m is None:
                    l, acc = jnp.sum(e, -1, keepdims=True), pv
                else:
                    alpha = jnp.exp2(m - m_new)
                    l, acc = alpha * l + jnp.sum(e, -1, keepdims=True), alpha * acc + pv
                m = m_new
            return acc / l

        o = softmax_v(q1) - lam * softmax_v(q2)
        o = o * lax.rsqrt(jnp.mean(o * o, -1, keepdims=True) + 1e-5) * g_ref[...] * out_scale
        o_ref[0] = o.astype(BF16)

    is_ctx_tile = pl.program_id(2) + first_tile == 0

    @pl.when(is_ctx_tile)
    def _():
        attend(n_ctx)

    @pl.when(jnp.logical_not(is_ctx_tile))
    def _():
        attend(kb_ref.shape[0])


def diff_attention(proj, cos_t, sin_t, subln, lam, out_scale, n_ctx, need_ctx):
    b, t, _ = proj.shape
    hw = 2 * ATT_DIM
    tq = ROW_TILE
    per_block = BRANCH_W // hw
    first = 0 if need_ctx else n_ctx // tq
    par = jnp.stack([lam, out_scale]).astype(F32)
    return pl.pallas_call(
        functools.partial(_diff_attn_kernel, n_ctx=n_ctx, first_tile=first),
        out_shape=jax.ShapeDtypeStruct((b, t, ATT_V_W), BF16),
        grid=(b, ATT_HEADS, t // tq - first),
        in_specs=[
            pl.BlockSpec(memory_space=pltpu.SMEM),
            pl.BlockSpec((1, tq, hw), lambda bi, h, i: (bi, i + first, COL_Q * per_block + h)),
            pl.BlockSpec((1, t, hw), lambda bi, h, i: (bi, 0, COL_K * per_block + h)),
            pl.BlockSpec((1, t, hw), lambda bi, h, i: (bi, 0, COL_V * per_block + h)),
            pl.BlockSpec((tq, hw), lambda bi, h, i: (i + first, 0)),
            pl.BlockSpec((tq, hw), lambda bi, h, i: (i + first, 0)),
            pl.BlockSpec((t, hw), lambda bi, h, i: (0, 0)),
            pl.BlockSpec((t, hw), lambda bi, h, i: (0, 0)),
            pl.BlockSpec((1, hw), lambda bi, h, i: (0, 0)),
        ],
        out_specs=pl.BlockSpec((1, tq, hw), lambda bi, h, i: (bi, i + first, h)),
        scratch_shapes=[pltpu.VMEM((t, hw), BF16), pltpu.VMEM((t, hw), BF16)],
        compiler_params=_params("parallel", "parallel", "arbitrary"),
        name="diff_attention",
    )(par, proj, proj, proj, cos_t, sin_t, cos_t, sin_t, subln.reshape(1, hw).astype(F32))


def _merge_kernel(x_ref, mod_ref, y0_ref, y1_ref, y2_ref, y3_ref, wg_ref, wb_ref, wo_ref, ln_ref, wr_ref, br_ref,
                  x1_ref, h2_ref, lg_ref):
    x = x_ref[0]
    mod = mod_ref[0, 0]
    h = (_normalize(x, 1e-6) * (1.0 + mod[1:2]) + mod[0:1]).astype(BF16)
    m = None
    for i, y_ref in enumerate((y0_ref, y1_ref, y2_ref, y3_ref)):
        gate = jax.nn.sigmoid(jnp.dot(h, wg_ref[:, i * D_MODEL:(i + 1) * D_MODEL], preferred_element_type=F32))
        term = gate * jnp.dot(y_ref[0], wb_ref[i], preferred_element_type=F32)
        m = term if m is None else m + term
    y = jnp.dot(m.astype(BF16), wo_ref[...], preferred_element_type=F32)
    x1 = _normalize(DN_ALPHA * x + mod[2:3] * y, 1e-5) * ln_ref[0:1] + ln_ref[1:2]
    x1_ref[0] = x1
    h2 = (_normalize(x1, 1e-6) * (1.0 + mod[4:5]) + mod[3:4]).astype(BF16)
    h2_ref[0] = h2
    lg_ref[0] = jnp.dot(h2, wr_ref[...], preferred_element_type=F32) + br_ref[...]


def merge(x, mods, ys, wg, wb, wo, ln, wr, br, first_tile):
    b, t, d = x.shape
    tm = ROW_TILE
    row = lambda bi, i: (bi, i + first_tile, 0)
    const2 = lambda bi, i: (0, 0)
    y_spec = pl.BlockSpec((1, tm, BRANCH_W), row)
    return pl.pallas_call(
        _merge_kernel,
        out_shape=(jax.ShapeDtypeStruct((b, t, d), F32), jax.ShapeDtypeStruct((b, t, d), BF16),
                   jax.ShapeDtypeStruct((b, t, LANES), F32)),
        grid=(b, t // tm - first_tile),
        in_specs=[
            pl.BlockSpec((1, tm, d), row),
            pl.BlockSpec((1, 1, 6, d), lambda bi, i: (bi, jnp.minimum(i + first_tile, 1), 0, 0)),
            y_spec, y_spec, y_spec, y_spec,
            pl.BlockSpec((d, N_BRANCH * d), const2),
            pl.BlockSpec((N_BRANCH, BRANCH_W, d), lambda bi, i: (0, 0, 0)),
            pl.BlockSpec((d, d), const2),
            pl.BlockSpec((2, d), const2),
            pl.BlockSpec((d, LANES), const2),
            pl.BlockSpec((1, LANES), const2),
        ],
        out_specs=(pl.BlockSpec((1, tm, d), row), pl.BlockSpec((1, tm, d), row), pl.BlockSpec((1, tm, LANES), row)),
        compiler_params=pltpu.CompilerParams(dimension_semantics=("parallel", "parallel"),
                                             vmem_limit_bytes=56 * 1024 * 1024),
        name="merge",
    )(x, mods, *ys, wg, wb, wo, ln, wr, br)


def _expert_kernel(be_ref, nb_ref, x_ref, wgu_ref, bgu_ref, wd_ref, bd_ref, o_ref, wgu_b, wd_b):
    i = pl.program_id(0)

    @pl.when(i < nb_ref[0])
    def _():
        @pl.when((i == 0) | (be_ref[i] != be_ref[jnp.maximum(i - 1, 0)]))
        def _():
            wgu_b[...] = wgu_ref[0, 0].astype(BF16)
            wd_b[...] = wd_ref[0, 0].astype(BF16)

        gu = jnp.dot(x_ref[...], wgu_b[...], preferred_element_type=F32) + bgu_ref[0, 0]
        gate = jnp.minimum(gu[:, :D_EXPERT], SWIGLU_LIMIT)
        up = jnp.clip(gu[:, D_EXPERT:], -SWIGLU_LIMIT, SWIGLU_LIMIT)
        glu = gate * jax.nn.sigmoid(gate * SWIGLU_ALPHA)
        act = ((up + 1.0) * glu).astype(BF16)
        y = jnp.dot(act, wd_b[...], preferred_element_type=F32) + bd_ref[0, 0]
        o_ref[...] = y.astype(BF16)


def expert_blocks(layer, block_e, n_used, xb, wgu, bgu, wd, bd):
    slots, d = xb.shape
    n_blocks = slots // EXPERT_BLOCK
    blk = lambda i, be, nb: jnp.minimum(i, nb[0] - 1)
    grid_spec = pltpu.PrefetchScalarGridSpec(
        num_scalar_prefetch=2,
        grid=(n_blocks,),
        in_specs=[
            pl.BlockSpec((EXPERT_BLOCK, d), lambda i, be, nb: (blk(i, be, nb), 0)),
            pl.BlockSpec((1, 1, d, 2 * D_EXPERT), lambda i, be, nb: (layer, be[blk(i, be, nb)], 0, 0)),
            pl.BlockSpec((1, 1, 1, 2 * D_EXPERT), lambda i, be, nb: (layer, be[blk(i, be, nb)], 0, 0)),
            pl.BlockSpec((1, 1, D_EXPERT, d), lambda i, be, nb: (layer, be[blk(i, be, nb)], 0, 0)),
            pl.BlockSpec((1, 1, 1, d), lambda i, be, nb: (layer, be[blk(i, be, nb)], 0, 0)),
        ],
        out_specs=pl.BlockSpec((EXPERT_BLOCK, d), lambda i, be, nb: (blk(i, be, nb), 0)),
        scratch_shapes=[pltpu.VMEM((d, 2 * D_EXPERT), BF16), pltpu.VMEM((D_EXPERT, d), BF16)],
    )
    return pl.pallas_call(
        _expert_kernel,
        out_shape=jax.ShapeDtypeStruct((slots, d), BF16),
        grid_spec=grid_spec,
        compiler_params=pltpu.CompilerParams(dimension_semantics=("arbitrary",), vmem_limit_bytes=EXPERT_VMEM_LIMIT),
        name="expert_blocks",
    )(block_e, n_used, xb, wgu, bgu[:, :, None, :], wd, bd[:, :, None, :])


ROUTE_DEST, ROUTE_PROB = 0, TOP_K


def _route_kernel(lg_ref, out_ref, cnt_ref, run_ref, base_ref):
    phase = pl.program_id(0)
    i = pl.program_id(1)
    tm = lg_ref.shape[0]
    lane = lax.broadcasted_iota(jnp.int32, (1, LANES), 1).astype(F32)
    cur = jnp.where(lane < N_EXPERTS, lg_ref[...], -jnp.inf)
    vals, picks = [], []
    member = jnp.zeros((tm, LANES), F32)
    for _ in range(TOP_K):
        v = jnp.max(cur, -1, keepdims=True)
        pick = lane == jnp.min(jnp.where(cur == v, lane, float(LANES)), -1, keepdims=True)
        member = member + pick.astype(F32)
        cur = jnp.where(pick, -jnp.inf, cur)
        vals.append(v)
        picks.append(pick)
    tile_counts = jnp.sum(member, 0, keepdims=True)

    @pl.when(phase == 0)
    def _():
        @pl.when(i == 0)
        def _():
            cnt_ref[...] = jnp.zeros_like(cnt_ref)
        cnt_ref[...] = cnt_ref[...] + tile_counts

    @pl.when(phase == 1)
    def _():
        @pl.when(i == 0)
        def _():
            blocks = jnp.floor((cnt_ref[...] + (EXPERT_BLOCK - 1)) * (1.0 / EXPERT_BLOCK))
            r = lax.broadcasted_iota(jnp.int32, (LANES, LANES), 0)
            c = lax.broadcasted_iota(jnp.int32, (LANES, LANES), 1)
            before = _split_dot_lhs(jnp.broadcast_to(blocks, (SUBLANES, LANES)), (r < c).astype(BF16))
            base_ref[...] = before[0:1] * float(EXPERT_BLOCK)
            run_ref[...] = jnp.zeros_like(run_ref)

        r = lax.broadcasted_iota(jnp.int32, (tm, tm), 0)
        c = lax.broadcasted_iota(jnp.int32, (tm, tm), 1)
        earlier = jnp.dot((c < r).astype(BF16), member.astype(BF16), preferred_element_type=F32)
        slot = base_ref[...] + run_ref[...] + earlier
        denom = 1.0
        exps = [1.0]
        for k in range(1, TOP_K):
            exps.append(jnp.exp(vals[k] - vals[0]))
            denom = denom + exps[k]
        out_lane = lax.broadcasted_iota(jnp.int32, (1, LANES), 1)
        row = jnp.zeros((tm, LANES), F32)
        for k in range(TOP_K):
            dest = jnp.sum(jnp.where(picks[k], slot, 0.0), -1, keepdims=True)
            row = jnp.where(out_lane == ROUTE_DEST + k, dest, row)
            row = jnp.where(out_lane == ROUTE_PROB + k, exps[k] / denom, row)
        out_ref[...] = row
        run_ref[...] = run_ref[...] + tile_counts


def route(logits):
    n = logits.shape[0]
    tm = ROW_TILE
    return pl.pallas_call(
        _route_kernel,
        out_shape=(jax.ShapeDtypeStruct((n, LANES), F32), jax.ShapeDtypeStruct((1, LANES), F32)),
        grid=(2, n // tm),
        in_specs=[pl.BlockSpec((tm, LANES), lambda ph, i: (i, 0))],
        out_specs=(pl.BlockSpec((tm, LANES), lambda ph, i: (i * ph, 0)), pl.BlockSpec((1, LANES), lambda ph, i: (0, 0))),
        scratch_shapes=[pltpu.VMEM((1, LANES), F32), pltpu.VMEM((1, LANES), F32)],
        compiler_params=_params("arbitrary", "arbitrary"),
        name="route",
    )(logits)


def _combine_norm_kernel(x_ref, y0_ref, y1_ref, y2_ref, y3_ref, rt_ref, mod_ref, ln_ref, o_ref):
    f = None
    for k, y_ref in enumerate((y0_ref, y1_ref, y2_ref, y3_ref)):
        term = rt_ref[0, :, ROUTE_PROB + k:ROUTE_PROB + k + 1] * y_ref[0].astype(F32)
        f = term if f is None else f + term
    x2 = DN_ALPHA * x_ref[0] + mod_ref[0, 0, 5:6] * f
    o_ref[0] = _normalize(x2, 1e-5) * ln_ref[0:1] + ln_ref[1:2]


def combine_norm(x1, ys, table, mods, ln, first_tile):
    b, t, d = x1.shape
    tm = ROW_TILE
    n_tiles = t // tm - first_tile
    y_spec = pl.BlockSpec((1, tm, d), lambda bi, i: (bi, i, 0))
    return pl.pallas_call(
        _combine_norm_kernel,
        out_shape=jax.ShapeDtypeStruct((b, n_tiles * tm, d), F32),
        grid=(b, n_tiles),
        in_specs=[
            pl.BlockSpec((1, tm, d), lambda bi, i: (bi, i + first_tile, 0)),
            y_spec, y_spec, y_spec, y_spec,
            pl.BlockSpec((1, tm, LANES), lambda bi, i: (bi, i, 0)),
            pl.BlockSpec((1, 1, 6, d), lambda bi, i: (bi, jnp.minimum(i + first_tile, 1), 0, 0)),
            pl.BlockSpec((2, d), lambda bi, i: (0, 0)),
        ],
        out_specs=pl.BlockSpec((1, tm, d), lambda bi, i: (bi, i, 0)),
        compiler_params=_params("parallel", "parallel"),
        name="combine_norm",
    )(x1, *ys, table, mods, ln)


def _proj_columns():
    o = IN_OFFS
    cols = list(range(o[0], o[7]))
    xbc, dt = o[7], o[8]
    for g in range(SSM_G):
        cols += [xbc + g * SSM_GW + i for i in range(SSM_GW)]
        cols += [xbc + SSM_W + g * SSM_N + i for i in range(SSM_N)]
        cols += [xbc + SSM_W + SSM_G * SSM_N + g * SSM_N + i for i in range(SSM_N)]
        cols += [dt + d * SSM_HEADS + g * SSM_K + k for d in range(2) for k in range(SSM_K)]
        cols += [-1] * (BRANCH_W - SSD_XBC - 2 * SSM_K)
    return np.asarray(cols, np.int32)


def _ssd_conv_columns():
    cols = []
    for g in range(SSM_G):
        cols += [g * SSM_GW + i for i in range(SSM_GW)]
        cols += [SSM_W + g * SSM_N + i for i in range(SSM_N)]
        cols += [SSM_W + SSM_G * SSM_N + g * SSM_N + i for i in range(SSM_N)]
    return np.asarray(cols, np.int32)


def _rope_tables(n_lat, n_ctx):
    rows = n_lat // GRID_W
    row = jnp.repeat(jnp.arange(rows, dtype=F32), GRID_W)
    col = jnp.tile(jnp.arange(GRID_W, dtype=F32), rows)
    inv = ROPE_BASE ** (-jnp.arange(ROPE_F, dtype=F32) / ROPE_F)
    ang = jnp.stack([row[:, None] * inv, col[:, None] * inv], axis=1)
    cos = jnp.broadcast_to(jnp.cos(ang)[:, None, :, None, :], (n_lat, 2, 2, 2, ROPE_F))
    sin = jnp.sin(ang)[:, None, :, None, :] * jnp.asarray([-1.0, 1.0], F32)[None, None, None, :, None]
    sin = jnp.broadcast_to(sin, (n_lat, 2, 2, 2, ROPE_F))
    hw = 2 * ATT_DIM
    cos = jnp.concatenate([jnp.ones((n_ctx, hw), F32), cos.reshape(n_lat, hw)], axis=0)
    sin = jnp.concatenate([jnp.zeros((n_ctx, hw), F32), sin.reshape(n_lat, hw)], axis=0)
    return cos, sin


def _block_diag(w):
    n, r, c = w.shape
    eye = jnp.eye(n, dtype=w.dtype)
    return (w[:, :, None, :] * eye[:, None, :, None]).reshape(n * r, n * c)


def _lane_pad(v, width=LANES):
    return jnp.pad(v, [(0, 0)] * (v.ndim - 1) + [(0, width - v.shape[-1])])


def _slot_tables(dest, counts):
    n = dest.shape[0]
    a = n * TOP_K
    n_blocks = -(-a // EXPERT_BLOCK) + N_EXPERTS
    pblocks = (counts + EXPERT_BLOCK - 1) // EXPERT_BLOCK
    pend_b = jnp.cumsum(pblocks)
    start = jnp.cumsum(counts) - counts
    blk = jnp.arange(n_blocks, dtype=jnp.int32)
    block_e = jnp.minimum(jnp.sum((blk[:, None] >= pend_b[None, :]).astype(jnp.int32), axis=1), N_EXPERTS - 1)
    n_used = pend_b[N_EXPERTS - 1:].astype(jnp.int32)
    tok = jnp.arange(a, dtype=jnp.int32) // TOP_K
    _, sorted_tok = lax.sort_key_val(dest.reshape(a), tok)
    slots = n_blocks * EXPERT_BLOCK
    max_pad = N_EXPERTS * EXPERT_BLOCK
    padded = jnp.concatenate([jnp.zeros((max_pad,), jnp.int32), sorted_tok, jnp.zeros((slots - a,), jnp.int32)])
    shift = (pend_b - pblocks) * EXPERT_BLOCK - start
    onehot = block_e[:, None] == jnp.arange(N_EXPERTS)[None, :]
    first_blk = jnp.sum(jnp.where(onehot, (pend_b - pblocks)[None, :], 0), axis=1)
    n_valid = jnp.sum(jnp.where(onehot, counts[None, :], 0), axis=1) - (blk - first_blk) * EXPERT_BLOCK
    valid = (blk < n_used[0])[:, None] & (jnp.arange(EXPERT_BLOCK)[None, :] < n_valid[:, None])
    slot_tok = jnp.zeros((n_blocks, EXPERT_BLOCK), jnp.int32)
    for e in range(N_EXPERTS):
        moved = lax.dynamic_slice(padded, (max_pad - shift[e],), (slots,)).reshape(n_blocks, EXPERT_BLOCK)
        slot_tok = jnp.where(valid & (block_e == e)[:, None], moved, slot_tok)
    return block_e, n_used, slot_tok.reshape(-1)


def kernel(x, c, ctx, c_ctx, w_ada, b_ada, w_in, rnn_conv_w, rnn_conv_b, rnn_wa, rnn_ba, rnn_wx, rnn_bx, rnn_lam, att_lambda, att_subln, pool_w, pool_b, pool_scale, ssm_conv_w, ssm_conv_b, ssm_dt_bias, ssm_a_log, ssm_d, ssm_norm, w_branch, w_out, ln1_g, ln1_b, ln2_g, ln2_b, router_w, router_b, w_gate_up, b_gate_up, w_down, b_down):
    bsz, n_lat, d = x.shape
    n_ctx = ctx.shape[1]
    total = n_ctx + n_lat
    assert n_ctx == ROW_TILE and n_lat % ROW_TILE == 0 and n_lat % GRID_W == 0
    ctx_tiles = n_ctx // ROW_TILE
    cos_t, sin_t = _rope_tables(n_lat, n_ctx)

    pcols = _proj_columns()
    w_main = jnp.where(pcols[None, None, :] >= 0, jnp.take(w_in, jnp.maximum(pcols, 0), axis=2), 0.0).astype(BF16)
    w_gates = w_in[:, :, IN_OFFS[9]:].astype(BF16)
    w_branch_b = w_branch.astype(BF16)
    w_out_b = w_out.astype(BF16)
    w_router = _lane_pad(router_w).astype(BF16)
    b_router = _lane_pad(router_b)[:, None, :]
    n_cg = RNN_W // LANES
    bpg = LANES // RNN_BLOCK
    gate_blocks = jnp.stack([rnn_wa[:, 0], rnn_wx[:, 0], rnn_wa[:, 1], rnn_wx[:, 1]], axis=1)
    gate_blocks = gate_blocks.reshape(DEPTH, 4, n_cg, bpg, RNN_BLOCK, RNN_BLOCK)
    rnn_wg = jax.vmap(jax.vmap(jax.vmap(_block_diag)))(gate_blocks)
    rnn_wg = rnn_wg.transpose(0, 2, 3, 1, 4).reshape(DEPTH, n_cg, LANES, 4 * LANES).astype(BF16)
    gate_bias = jnp.stack([rnn_ba[:, 0], rnn_bx[:, 0], rnn_ba[:, 1], rnn_bx[:, 1]], axis=1)
    rnn_bg = gate_bias.reshape(DEPTH, 4, n_cg, LANES).transpose(0, 2, 1, 3).reshape(DEPTH, n_cg, 1, 4 * LANES)
    rnn_sp = jax.nn.softplus(-rnn_lam).reshape(DEPTH, 2, n_cg, LANES).transpose(0, 2, 1, 3).reshape(DEPTH, n_cg, 1, 2 * LANES)
    ccols = _ssd_conv_columns()
    ssd_cw = jnp.take(ssm_conv_w, ccols, axis=2).reshape(DEPTH, CONV_W, SSM_G, SSD_XBC).transpose(0, 2, 1, 3)
    ssd_cb = jnp.take(ssm_conv_b, ccols, axis=1).reshape(DEPTH, SSM_G, 1, SSD_XBC)
    per_group = lambda v: v.reshape(DEPTH, 2, SSM_G, SSM_K).transpose(0, 2, 1, 3).reshape(DEPTH, SSM_G, 1, 2 * SSM_K)
    ssd_dtb = _lane_pad(per_group(ssm_dt_bias))
    ssd_alog = _lane_pad(per_group(ssm_a_log))
    ssd_dsk = jnp.repeat((ssm_d[:, 0] + ssm_d[:, 1]).reshape(DEPTH, SSM_G, 1, SSM_K), SSM_P, axis=-1)
    ssd_nw = ssm_norm.reshape(DEPTH, SSM_G, 1, SSM_GW)
    pool_wb = pool_w.astype(BF16)

    xs = jnp.concatenate([ctx, x], axis=1)
    out = None
    for li in range(DEPTH):
        need_ctx = li < DEPTH - 1
        first = 0 if need_ctx else ctx_tiles
        mod_l = jax.nn.silu(c) @ w_ada[li] + b_ada[li]
        mod_c = jnp.broadcast_to(jax.nn.silu(c_ctx) @ w_ada[li] + b_ada[li], mod_l.shape)
        mods = jnp.stack([mod_c, mod_l], axis=1).reshape(bsz, 2, 6, d)

        proj = in_proj(xs, mods, w_main[li], n_ctx, 768 if total % 768 == 0 else ROW_TILE, BRANCH_W * 3 if PROJ_W % (BRANCH_W * 3) == 0 else BRANCH_W)
        y_rnn = rglru(proj, rnn_conv_w[li], rnn_conv_b[li][None], rnn_wg[li], rnn_bg[li], rnn_sp[li], n_ctx)
        lam_init = 0.8 - 0.6 * math.exp(-0.3 * li)
        lv = att_lambda[li]
        lam = jnp.exp(jnp.sum(lv[0] * lv[1])) - jnp.exp(jnp.sum(lv[2] * lv[3])) + lam_init
        y_att = diff_attention(proj, cos_t, sin_t, att_subln[li], lam, jnp.asarray(1.0 - lam_init, F32), n_ctx, need_ctx)
        y_pool = pool(proj, pool_wb[li], pool_b[li][None], pool_scale[li][None], n_ctx)
        y_ssd = ssd(proj, ssd_cw[li], ssd_cb[li], ssd_dtb[li], ssd_alog[li], ssd_dsk[li], ssd_nw[li], n_ctx)
        ln1 = jnp.stack([ln1_g[li], ln1_b[li]])
        x1, h2, logits = merge(xs, mods, (y_rnn, y_att, y_pool, y_ssd), w_gates[li], w_branch_b[li], w_out_b[li], ln1,
                               w_router[li], b_router[li], first)

        n_rout = total - first * ROW_TILE
        table, counts = route(logits[:, first * ROW_TILE:].reshape(bsz * n_rout, LANES))
        dest = table[:, ROUTE_DEST:ROUTE_DEST + TOP_K].astype(jnp.int32)
        block_e, n_used, slot_tok = _slot_tables(dest, counts[0, :N_EXPERTS].astype(jnp.int32))
        slot_row = slot_tok + (slot_tok // n_rout + 1) * (first * ROW_TILE)
        xb = h2.reshape(bsz * total, d)[slot_row]
        yb = expert_blocks(li, block_e, n_used, xb, w_gate_up, b_gate_up, w_down, b_down)
        ys = [yb[dest[:, k]].reshape(bsz, n_rout, d) for k in range(TOP_K)]
        ln2 = jnp.stack([ln2_g[li], ln2_b[li]])
        out = combine_norm(x1, ys, table.reshape(bsz, n_rout, LANES), mods, ln2, first)
        xs = out
    return out
```

```python
import functools
import math

import jax
import jax.numpy as jnp
import numpy as np
from jax import lax
from jax.experimental import pallas as pl
from jax.experimental.pallas import tpu as pltpu

D_MODEL = 1024
DEPTH = 4
GRID_W = 64
N_BRANCH = 4
BRANCH_W = D_MODEL // 2
RNN_W = BRANCH_W
RNN_BLOCK = 64
RG_C = 8.0
CONV_W = 4
ATT_DIM = 64
ATT_HEADS = BRANCH_W // (2 * ATT_DIM)
ATT_V_W = ATT_HEADS * 2 * ATT_DIM
ROPE_F = ATT_DIM // 4
ROPE_BASE = 10000.0
POOL_W = BRANCH_W
POOL_WINDOWS = (2, 4, 8, 16)
POOL_GROUPS = len(POOL_WINDOWS)
POOL_GW = POOL_W // POOL_GROUPS
SSM_W = BRANCH_W
SSM_P = 64
SSM_HEADS = SSM_W // SSM_P
SSM_G = 2
SSM_K = SSM_HEADS // SSM_G
SSM_N = 64
SSM_GW = SSM_W // SSM_G
SSM_CHUNK = 128
N_EXPERTS = 32
TOP_K = 4
D_EXPERT = D_MODEL
SWIGLU_LIMIT = 7.0
SWIGLU_ALPHA = 1.702
EXPERT_BLOCK = 256
DN_ALPHA = (2 * DEPTH) ** 0.25
IN_WIDTHS = (RNN_W, RNN_W, BRANCH_W, BRANCH_W, ATT_V_W, POOL_W, SSM_W, SSM_W + 2 * SSM_G * SSM_N, 2 * SSM_HEADS, N_BRANCH * D_MODEL)
IN_OFFS = tuple(int(v) for v in np.cumsum((0,) + IN_WIDTHS))

LANES = 128
SUBLANES = 8
VMEM_LIMIT = 48 * 1024 * 1024
ROW_TILE = 256
EXPERT_VMEM_LIMIT = 54 * 1024 * 1024
MERGE_VMEM_LIMIT = 56 * 1024 * 1024

COL_RX, COL_RG, COL_Q, COL_K, COL_V, COL_PU, COL_Z, COL_SSD = 0, 1, 2, 3, 4, 5, 6, 7
PROJ_W = (COL_SSD + SSM_G) * BRANCH_W
SSD_XBC = SSM_GW + 2 * SSM_N

BF16 = jnp.bfloat16
F32 = jnp.float32


def _params(*sem):
    return pltpu.CompilerParams(dimension_semantics=sem, vmem_limit_bytes=VMEM_LIMIT)


def _normalize(x, eps):
    mu = jnp.mean(x, -1, keepdims=True)
    xc = x - mu
    var = jnp.mean(xc * xc, -1, keepdims=True)
    return xc * lax.rsqrt(var + eps)


def _segment_valid(t, off, n_ctx, total):
    lo = jnp.where(t < n_ctx, 0, n_ctx)
    hi = jnp.where(t < n_ctx, n_ctx, total)
    return (t + off >= lo) & (t + off < hi)


def _shift_rows(u, off, t, n_ctx):
    total = u.shape[0]
    rolled = pltpu.roll(u, (-off) % total, 0)
    return jnp.where(_segment_valid(t, off, n_ctx, total), rolled, 0.0)


def _centred_conv(u, w_ref, b_ref, n_ctx):
    t = lax.broadcasted_iota(jnp.int32, (u.shape[0], 1), 0)
    left = CONV_W // 2
    out = b_ref[...] + _shift_rows(u, -left, t, n_ctx) * w_ref[0:1, :]
    for k in range(1, CONV_W):
        tap = u if k == left else _shift_rows(u, k - left, t, n_ctx)
        out = out + tap * w_ref[k:k + 1, :]
    return out


def _in_proj_kernel(x_ref, mod_ref, w_ref, o_ref, h_ref, *, n_ctx):
    @pl.when(pl.program_id(2) == 0)
    def _():
        tm = x_ref.shape[1]
        row = pl.program_id(1) * tm + lax.broadcasted_iota(jnp.int32, (tm, 1), 0)
        is_ctx = row < n_ctx
        shift = jnp.where(is_ctx, mod_ref[0, 0, 0:1, :], mod_ref[0, 1, 0:1, :])
        scale = jnp.where(is_ctx, mod_ref[0, 0, 1:2, :], mod_ref[0, 1, 1:2, :])
        h_ref[...] = (_normalize(x_ref[0], 1e-6) * (1.0 + scale) + shift).astype(BF16)

    o_ref[0] = jnp.dot(h_ref[...], w_ref[...], preferred_element_type=F32)


def in_proj(x, mods, w, n_ctx, tm, tn):
    b, t, d = x.shape
    n = w.shape[1]
    return pl.pallas_call(
        functools.partial(_in_proj_kernel, n_ctx=n_ctx),
        out_shape=jax.ShapeDtypeStruct((b, t, n), F32),
        grid=(b, t // tm, n // tn),
        in_specs=[
            pl.BlockSpec((1, tm, d), lambda bi, i, j: (bi, i, 0)),
            pl.BlockSpec((1, 2, 6, d), lambda bi, i, j: (bi, 0, 0, 0)),
            pl.BlockSpec((d, tn), lambda bi, i, j: (0, j)),
        ],
        out_specs=pl.BlockSpec((1, tm, tn), lambda bi, i, j: (bi, i, j)),
        scratch_shapes=[pltpu.VMEM((tm, d), BF16)],
        compiler_params=_params("parallel", "parallel", "arbitrary"),
        name="in_proj",
    )(x, mods, w)


SCAN_ROWS = 64


def _tile_scan(a, b, reverse):
    rows = a.shape[0]
    sub = lax.broadcasted_iota(jnp.int32, (rows, 1), 0) % SUBLANES
    s = 1
    while s < SUBLANES:
        shift = (rows - s) if reverse else s
        keep = (sub + s < SUBLANES) if reverse else (sub >= s)
        a_sh = pltpu.roll(a, shift, 0)
        b_sh = pltpu.roll(b, shift, 0)
        b = jnp.where(keep, a * b_sh + b, b)
        a = jnp.where(keep, a * a_sh, a)
        s *= 2
    return a, b


def _rglru_kernel(rx_ref, rg_ref, cw_ref, cb_ref, wg_ref, bg_ref, sp_ref, o_ref, a_ref, b_ref, y_ref, *, n_ctx):
    total = rx_ref.shape[1]
    cw = rx_ref.shape[2]
    u = _centred_conv(rx_ref[0], cw_ref, cb_ref, n_ctx)
    g = jnp.dot(u.astype(BF16), wg_ref[0], preferred_element_type=F32) + bg_ref[0]
    n_tiles = SCAN_ROWS // SUBLANES

    for d in range(2):
        r = jax.nn.sigmoid(g[:, (2 * d) * cw:(2 * d + 1) * cw])
        i = jax.nn.sigmoid(g[:, (2 * d + 1) * cw:(2 * d + 2) * cw])
        log_a = -RG_C * r * sp_ref[0, :, d * cw:(d + 1) * cw]
        a = jnp.exp(log_a)
        a_ref[d] = a
        b_ref[d] = jnp.sqrt((1.0 - a) * (1.0 + a)) * (i * u)

    def scan_rows(r0, h, d):
        reverse = d == 1
        a, b = _tile_scan(a_ref[d, pl.ds(r0, SCAN_ROWS), :], b_ref[d, pl.ds(r0, SCAN_ROWS), :], reverse)
        outs = [None] * n_tiles
        for j in (range(n_tiles - 1, -1, -1) if reverse else range(n_tiles)):
            sl = slice(j * SUBLANES, (j + 1) * SUBLANES)
            hj = b[sl] + a[sl] * h
            h = hj[0:1] if reverse else hj[SUBLANES - 1:SUBLANES]
            outs[j] = hj
        y_ref[d, pl.ds(r0, SCAN_ROWS), :] = jnp.concatenate(outs, axis=0)
        return h

    def step(it, hs, lo, n_steps):
        h_f = scan_rows(pl.multiple_of(lo + it * SCAN_ROWS, SCAN_ROWS), hs[0], 0)
        h_b = scan_rows(pl.multiple_of(lo + (n_steps - 1 - it) * SCAN_ROWS, SCAN_ROWS), hs[1], 1)
        return h_f, h_b

    h0 = jnp.zeros((1, cw), F32)
    hs = lax.fori_loop(0, n_ctx // SCAN_ROWS, functools.partial(step, lo=0, n_steps=n_ctx // SCAN_ROWS), (h0, h0))
    n_lat_steps = (total - n_ctx) // SCAN_ROWS
    lax.fori_loop(0, n_lat_steps, functools.partial(step, lo=n_ctx, n_steps=n_lat_steps), hs)

    o_ref[0] = (jax.nn.gelu(rg_ref[0]) * (y_ref[0] + y_ref[1])).astype(BF16)


def rglru(proj, conv_w, conv_b, wg, bg, sp, n_ctx):
    b, t, _ = proj.shape
    cw = LANES
    n_cg = RNN_W // cw
    per_block = BRANCH_W // cw
    return pl.pallas_call(
        functools.partial(_rglru_kernel, n_ctx=n_ctx),
        out_shape=jax.ShapeDtypeStruct((b, t, RNN_W), BF16),
        grid=(b, n_cg),
        in_specs=[
            pl.BlockSpec((1, t, cw), lambda bi, c: (bi, 0, COL_RX * per_block + c)),
            pl.BlockSpec((1, t, cw), lambda bi, c: (bi, 0, COL_RG * per_block + c)),
            pl.BlockSpec((CONV_W, cw), lambda bi, c: (0, c)),
            pl.BlockSpec((1, cw), lambda bi, c: (0, c)),
            pl.BlockSpec((1, cw, 4 * cw), lambda bi, c: (c, 0, 0)),
            pl.BlockSpec((1, 1, 4 * cw), lambda bi, c: (c, 0, 0)),
            pl.BlockSpec((1, 1, 2 * cw), lambda bi, c: (c, 0, 0)),
        ],
        out_specs=pl.BlockSpec((1, t, cw), lambda bi, c: (bi, 0, c)),
        scratch_shapes=[pltpu.VMEM((2, t, cw), F32), pltpu.VMEM((2, t, cw), F32), pltpu.VMEM((2, t, cw), F32)],
        compiler_params=_params("parallel", "parallel"),
        name="rglru",
    )(proj, proj, conv_w, conv_b, wg, bg, sp)


def _pool_kernel(u_ref, w_ref, b_ref, s_ref, o_ref, *, n_ctx):
    total = u_ref.shape[1]
    t = lax.broadcasted_iota(jnp.int32, (total, 1), 0)
    for gi, win in enumerate(POOL_WINDOWS):
        cols = slice(gi * POOL_GW, (gi + 1) * POOL_GW)
        u = u_ref[0, :, cols]
        acc = u
        cnt = jnp.ones((total, 1), F32)
        for off in range(-(win // 2), win - win // 2):
            if off == 0:
                continue
            acc = acc + _shift_rows(u, off, t, n_ctx)
            cnt = cnt + _segment_valid(t, off, n_ctx, total).astype(F32)
        mix = acc / cnt - u
        y = jnp.dot(mix.astype(BF16), w_ref[gi], preferred_element_type=F32) + b_ref[:, cols]
        o_ref[0, :, cols] = (y * s_ref[:, cols]).astype(BF16)


def pool(proj, w, bias, scale, n_ctx):
    b, t, _ = proj.shape
    return pl.pallas_call(
        functools.partial(_pool_kernel, n_ctx=n_ctx),
        out_shape=jax.ShapeDtypeStruct((b, t, POOL_W), BF16),
        grid=(b,),
        in_specs=[
            pl.BlockSpec((1, t, POOL_W), lambda bi: (bi, 0, COL_PU)),
            pl.BlockSpec((POOL_GROUPS, POOL_GW, POOL_GW), lambda bi: (0, 0, 0)),
            pl.BlockSpec((1, POOL_W), lambda bi: (0, 0)),
            pl.BlockSpec((1, POOL_W), lambda bi: (0, 0)),
        ],
        out_specs=pl.BlockSpec((1, t, POOL_W), lambda bi: (bi, 0, 0)),
        compiler_params=_params("parallel"),
        name="pool",
    )(proj, w, bias, scale)


def _bf16_pieces(v):
    hi = v.astype(BF16)
    r1 = v - hi.astype(F32)
    mid = r1.astype(BF16)
    lo = (r1 - mid.astype(F32)).astype(BF16)
    return hi, mid, lo


def _split_dot(tri, v):
    return sum(jnp.dot(tri, p, preferred_element_type=F32) for p in _bf16_pieces(v))


def _split_dot_lhs(v, tri):
    return sum(jnp.dot(p, tri, preferred_element_type=F32) for p in _bf16_pieces(v))


def _per_head(cols, width):
    head = lax.broadcasted_iota(jnp.int32, (1, width), 1) // SSM_P
    out = cols[SSM_K - 1]
    for k in range(SSM_K - 2, -1, -1):
        out = jnp.where(head == k, cols[k], out)
    return out


def _ssd_kernel(blk_ref, z_ref, cw_ref, cb_ref, dtb_ref, alog_ref, dsk_ref, nw_ref, o_ref,
                u_ref, dt_ref, adt_ref, tr_ref, y_ref, st_ref, *, n_ctx):
    total = blk_ref.shape[1]
    ck = SSM_CHUNK
    n_chunks = total // ck
    ctx_chunks = n_ctx // ck
    u_ref[...] = jax.nn.silu(_centred_conv(blk_ref[0, :, 0:SSD_XBC], cw_ref.at[0], cb_ref.at[0], n_ctx))
    dt = jax.nn.softplus(blk_ref[0, :, SSD_XBC:SSD_XBC + LANES] + dtb_ref[0])
    dt_ref[...] = dt
    adt_ref[...] = dt * (-jnp.exp(alog_ref[0]))
    for c in range(n_chunks):
        rows = slice(c * ck, (c + 1) * ck)
        tr_ref[c, 0:ck, :] = u_ref[rows, SSM_GW:SSM_GW + 2 * SSM_N].T
        tr_ref[c, ck:2 * ck, :] = adt_ref[rows, :].T

    ri = lax.broadcasted_iota(jnp.int32, (ck, ck), 0)
    ci = lax.broadcasted_iota(jnp.int32, (ck, ck), 1)
    lower = (ci <= ri)
    lower_b = lower.astype(BF16)
    upper_b = (ci >= ri).astype(BF16)
    lane = lax.broadcasted_iota(jnp.int32, (1, LANES), 1)
    head_w = lax.broadcasted_iota(jnp.int32, (1, SSM_GW), 1) // SSM_P
    srow = lax.broadcasted_iota(jnp.int32, (ck, 1), 0)

    def chunk(c, d):
        reverse = d == 1
        tri_col, tri_row = (upper_b, lower_b) if reverse else (lower_b, upper_b)
        causal = (ci >= ri) if reverse else lower
        r0 = pl.multiple_of(c * ck, ck)
        xs = u_ref[pl.ds(r0, ck), 0:SSM_GW]
        bc = u_ref[pl.ds(r0, ck), SSM_GW:SSM_GW + 2 * SSM_N]
        tr = tr_ref[c, 0:ck, :]
        c_lo = jnp.where(lane < SSM_N, pltpu.roll(bc, SSM_N, 1), 0.0).astype(BF16)
        cb = jnp.dot(c_lo, tr.astype(BF16), preferred_element_type=F32)
        cs_col = _split_dot(tri_col, adt_ref[pl.ds(r0, ck), :])
        cs_row = _split_dot_lhs(tr_ref[c, ck:2 * ck, :], tri_row)
        edge = ck - 1 if not reverse else 0
        dtc = dt_ref[pl.ds(r0, ck), :]
        cols, tots, dts = [], [], []
        y = jnp.zeros((ck, SSM_GW), F32)
        for k in range(SSM_K):
            j = d * SSM_K + k
            cols.append(cs_col[:, j:j + 1])
            tots.append(cs_col[edge:edge + 1, j:j + 1])
            dts.append(dtc[:, j:j + 1])
        xdt = xs * _per_head(dts, SSM_GW)
        for k in range(SSM_K):
            j = d * SSM_K + k
            seg = cols[k] - cs_row[j:j + 1, :]
            m = (cb * jnp.where(causal, jnp.exp(seg), 0.0)).astype(BF16)
            y = y + jnp.dot(m, jnp.where(head_w == k, xdt, 0.0).astype(BF16), preferred_element_type=F32)
        col_w = _per_head(cols, SSM_GW)
        tot_w = _per_head(tots, SSM_GW)
        state = st_ref[d]
        y = y + jnp.dot(c_lo, state.astype(BF16), preferred_element_type=F32) * jnp.exp(col_w)
        xd = (xdt * jnp.exp(tot_w - col_w)).astype(BF16)
        new = jnp.exp(tot_w) * state + jnp.dot(tr.astype(BF16), xd, preferred_element_type=F32)
        st_ref[d] = jnp.where(srow < SSM_N, new, 0.0)
        y_ref[d, pl.ds(r0, ck), :] = y

    def both(it, carry, lo, n_steps):
        chunk(lo + it, 0)
        chunk(lo + n_steps - 1 - it, 1)
        return carry

    st_ref[...] = jnp.zeros_like(st_ref)
    lax.fori_loop(0, ctx_chunks, functools.partial(both, lo=0, n_steps=ctx_chunks), 0)
    lat_chunks = n_chunks - ctx_chunks
    lax.fori_loop(0, lat_chunks, functools.partial(both, lo=ctx_chunks, n_steps=lat_chunks), 0,
                  unroll=2 if lat_chunks % 2 == 0 else 1)

    y = y_ref[0] + y_ref[1] + dsk_ref[0] * u_ref[:, 0:SSM_GW]
    g = y * jax.nn.silu(z_ref[0])
    g = g * lax.rsqrt(jnp.mean(g * g, -1, keepdims=True) + 1e-5)
    o_ref[0] = (g * nw_ref[0]).astype(BF16)


def ssd(proj, conv_w, conv_b, dt_bias, a_log, d_skip, norm_w, n_ctx):
    b, t, _ = proj.shape
    n_chunks = t // SSM_CHUNK
    z_per = BRANCH_W // SSM_GW
    return pl.pallas_call(
        functools.partial(_ssd_kernel, n_ctx=n_ctx),
        out_shape=jax.ShapeDtypeStruct((b, t, SSM_W), BF16),
        grid=(b, SSM_G),
        in_specs=[
            pl.BlockSpec((1, t, BRANCH_W), lambda bi, g: (bi, 0, COL_SSD + g)),
            pl.BlockSpec((1, t, SSM_GW), lambda bi, g: (bi, 0, COL_Z * z_per + g)),
            pl.BlockSpec((1, CONV_W, SSD_XBC), lambda bi, g: (g, 0, 0)),
            pl.BlockSpec((1, 1, SSD_XBC), lambda bi, g: (g, 0, 0)),
            pl.BlockSpec((1, 1, LANES), lambda bi, g: (g, 0, 0)),
            pl.BlockSpec((1, 1, LANES), lambda bi, g: (g, 0, 0)),
            pl.BlockSpec((1, 1, SSM_GW), lambda bi, g: (g, 0, 0)),
            pl.BlockSpec((1, 1, SSM_GW), lambda bi, g: (g, 0, 0)),
        ],
        out_specs=pl.BlockSpec((1, t, SSM_GW), lambda bi, g: (bi, 0, g)),
        scratch_shapes=[
            pltpu.VMEM((t, SSD_XBC), F32),
            pltpu.VMEM((t, LANES), F32),
            pltpu.VMEM((t, LANES), F32),
            pltpu.VMEM((n_chunks, 2 * SSM_CHUNK, SSM_CHUNK), F32),
            pltpu.VMEM((2, t, SSM_GW), F32),
            pltpu.VMEM((2, SSM_CHUNK, SSM_GW), F32),
        ],
        compiler_params=_params("parallel", "parallel"),
        name="ssd",
    )(proj, proj, conv_w, conv_b, dt_bias, a_log, d_skip, norm_w)


def _rope(x, cos, sin):
    lane = lax.broadcasted_iota(jnp.int32, (1, x.shape[1]), 1)
    partner = jnp.where(lane % (2 * ROPE_F) < ROPE_F, pltpu.roll(x, x.shape[1] - ROPE_F, 1), pltpu.roll(x, ROPE_F, 1))
    return x * cos + partner * sin


def _diff_attn_kernel(par_ref, q_ref, k_ref, v_ref, cq_ref, sq_ref, ck_ref, sk_ref, g_ref, o_ref, kb_ref, vb_ref,
                      *, n_ctx, first_tile):
    lam = par_ref[0]
    out_scale = par_ref[1]

    @pl.when(pl.program_id(2) == 0)
    def _():
        kb_ref[...] = _rope(k_ref[0], ck_ref[...], sk_ref[...]).astype(BF16)
        vb_ref[...] = v_ref[0].astype(BF16)

    q = (_rope(q_ref[0], cq_ref[...], sq_ref[...]) * (ATT_DIM ** -0.5 * math.log2(math.e))).astype(BF16)
    lane = lax.broadcasted_iota(jnp.int32, q.shape, 1)
    dims = (((1,), (1,)), ((), ()))
    q1 = jnp.where(lane < ATT_DIM, q, jnp.zeros_like(q))
    q2 = jnp.where(lane >= ATT_DIM, q, jnp.zeros_like(q))

    def attend(n_keys):
        k = kb_ref[0:n_keys, :]
        v = vb_ref[0:n_keys, :]

        def softmax_v(qh):
            s = lax.dot_general(qh, k, dims, preferred_element_type=F32)
            e = jnp.exp2(s - jnp.max(s, -1, keepdims=True))
            return jnp.dot(e.astype(BF16), v, preferred_element_type=F32) / jnp.sum(e, -1, keepdims=True)

        o = softmax_v(q1) - lam * softmax_v(q2)
        o = o * lax.rsqrt(jnp.mean(o * o, -1, keepdims=True) + 1e-5) * g_ref[...] * out_scale
        o_ref[0] = o.astype(BF16)

    is_ctx_tile = pl.program_id(2) + first_tile == 0

    @pl.when(is_ctx_tile)
    def _():
        attend(n_ctx)

    @pl.when(jnp.logical_not(is_ctx_tile))
    def _():
        attend(kb_ref.shape[0])


def diff_attention(proj, cos_t, sin_t, subln, lam, out_scale, n_ctx, need_ctx):
    b, t, _ = proj.shape
    hw = 2 * ATT_DIM
    tq = ROW_TILE
    per_block = BRANCH_W // hw
    first = 0 if need_ctx else n_ctx // tq
    par = jnp.stack([lam, out_scale]).astype(F32)
    return pl.pallas_call(
        functools.partial(_diff_attn_kernel, n_ctx=n_ctx, first_tile=first),
        out_shape=jax.ShapeDtypeStruct((b, t, ATT_V_W), BF16),
        grid=(b, ATT_HEADS, t // tq - first),
        in_specs=[
            pl.BlockSpec(memory_space=pltpu.SMEM),
            pl.BlockSpec((1, tq, hw), lambda bi, h, i: (bi, i + first, COL_Q * per_block + h)),
            pl.BlockSpec((1, t, hw), lambda bi, h, i: (bi, 0, COL_K * per_block + h)),
            pl.BlockSpec((1, t, hw), lambda bi, h, i: (bi, 0, COL_V * per_block + h)),
            pl.BlockSpec((tq, hw), lambda bi, h, i: (i + first, 0)),
            pl.BlockSpec((tq, hw), lambda bi, h, i: (i + first, 0)),
            pl.BlockSpec((t, hw), lambda bi, h, i: (0, 0)),
            pl.BlockSpec((t, hw), lambda bi, h, i: (0, 0)),
            pl.BlockSpec((1, hw), lambda bi, h, i: (0, 0)),
        ],
        out_specs=pl.BlockSpec((1, tq, hw), lambda bi, h, i: (bi, i + first, h)),
        scratch_shapes=[pltpu.VMEM((t, hw), BF16), pltpu.VMEM((t, hw), BF16)],
        compiler_params=_params("parallel", "parallel", "arbitrary"),
        name="diff_attention",
    )(par, proj, proj, proj, cos_t, sin_t, cos_t, sin_t, subln.reshape(1, hw).astype(F32))


def _merge_kernel(x_ref, mod_ref, y0_ref, y1_ref, y2_ref, y3_ref, wg_ref, wb_ref, wo_ref, ln_ref, wr_ref, br_ref,
                  x1_ref, h2_ref, lg_ref):
    x = x_ref[0]
    mod = mod_ref[0, 0]
    h = (_normalize(x, 1e-6) * (1.0 + mod[1:2]) + mod[0:1]).astype(BF16)
    m = None
    for i, y_ref in enumerate((y0_ref, y1_ref, y2_ref, y3_ref)):
        gate = jax.nn.sigmoid(jnp.dot(h, wg_ref[:, i * D_MODEL:(i + 1) * D_MODEL], preferred_element_type=F32))
        term = gate * jnp.dot(y_ref[0], wb_ref[i], preferred_element_type=F32)
        m = term if m is None else m + term
    y = jnp.dot(m.astype(BF16), wo_ref[...], preferred_element_type=F32)
    x1 = _normalize(DN_ALPHA * x + mod[2:3] * y, 1e-5) * ln_ref[0:1] + ln_ref[1:2]
    x1_ref[0] = x1
    h2 = (_normalize(x1, 1e-6) * (1.0 + mod[4:5]) + mod[3:4]).astype(BF16)
    h2_ref[0] = h2
    lg_ref[0] = jnp.dot(h2, wr_ref[...], preferred_element_type=F32) + br_ref[...]


def merge(x, mods, ys, wg, wb, wo, ln, wr, br, first_tile):
    b, t, d = x.shape
    tm = ROW_TILE
    row = lambda bi, i: (bi, i + first_tile, 0)
    const2 = lambda bi, i: (0, 0)
    y_spec = pl.BlockSpec((1, tm, BRANCH_W), row)
    return pl.pallas_call(
        _merge_kernel,
        out_shape=(jax.ShapeDtypeStruct((b, t, d), F32), jax.ShapeDtypeStruct((b, t, d), BF16),
                   jax.ShapeDtypeStruct((b, t, LANES), F32)),
        grid=(b, t // tm - first_tile),
        in_specs=[
            pl.BlockSpec((1, tm, d), row),
            pl.BlockSpec((1, 1, 6, d), lambda bi, i: (bi, jnp.minimum(i + first_tile, 1), 0, 0)),
            y_spec, y_spec, y_spec, y_spec,
            pl.BlockSpec((d, N_BRANCH * d), const2),
            pl.BlockSpec((N_BRANCH, BRANCH_W, d), lambda bi, i: (0, 0, 0)),
            pl.BlockSpec((d, d), const2),
            pl.BlockSpec((2, d), const2),
            pl.BlockSpec((d, LANES), const2),
            pl.BlockSpec((1, LANES), const2),
        ],
        out_specs=(pl.BlockSpec((1, tm, d), row), pl.BlockSpec((1, tm, d), row), pl.BlockSpec((1, tm, LANES), row)),
        compiler_params=pltpu.CompilerParams(dimension_semantics=("parallel", "parallel"),
                                             vmem_limit_bytes=MERGE_VMEM_LIMIT),
        name="merge",
    )(x, mods, *ys, wg, wb, wo, ln, wr, br)


def _expert_kernel(be_ref, nb_ref, x_ref, wgu_ref, bgu_ref, wd_ref, bd_ref, *rest, first_block):
    o_ref, wgu_b, wd_b = rest[-3:]
    i = pl.program_id(0)
    blk = first_block + i

    @pl.when(blk < nb_ref[0])
    def _():
        @pl.when((i == 0) | (be_ref[blk] != be_ref[jnp.maximum(blk - 1, 0)]))
        def _():
            wgu_b[...] = wgu_ref[0, 0].astype(BF16)
            wd_b[...] = wd_ref[0, 0].astype(BF16)

        gu = jnp.dot(x_ref[...], wgu_b[...], preferred_element_type=F32) + bgu_ref[0, 0]
        gate = jnp.minimum(gu[:, :D_EXPERT], SWIGLU_LIMIT)
        up = jnp.clip(gu[:, D_EXPERT:], -SWIGLU_LIMIT, SWIGLU_LIMIT)
        glu = gate * jax.nn.sigmoid(gate * SWIGLU_ALPHA)
        act = ((up + 1.0) * glu).astype(BF16)
        y = jnp.dot(act, wd_b[...], preferred_element_type=F32) + bd_ref[0, 0]
        o_ref[...] = y.astype(BF16)


def expert_blocks(layer, block_e, n_used, xb, yb_prev, first_block, n_blocks_total, wgu, bgu, wd, bd):
    rows, d = xb.shape
    n_piece = rows // EXPERT_BLOCK

    def local(i, nb):
        return jnp.maximum(jnp.minimum(i, nb[0] - 1 - first_block), 0)

    expert = lambda i, be, nb: (layer, be[first_block + local(i, nb)], 0, 0)
    in_specs = [
        pl.BlockSpec((EXPERT_BLOCK, d), lambda i, be, nb: (local(i, nb), 0)),
        pl.BlockSpec((1, 1, d, 2 * D_EXPERT), expert),
        pl.BlockSpec((1, 1, 1, 2 * D_EXPERT), expert),
        pl.BlockSpec((1, 1, D_EXPERT, d), expert),
        pl.BlockSpec((1, 1, 1, d), expert),
    ]
    args = [block_e, n_used, xb, wgu, bgu[:, :, None, :], wd, bd[:, :, None, :]]
    aliases = {}
    if yb_prev is not None:
        in_specs.append(pl.BlockSpec(memory_space=pl.ANY))
        aliases = {len(args): 0}
        args.append(yb_prev)
    grid_spec = pltpu.PrefetchScalarGridSpec(
        num_scalar_prefetch=2,
        grid=(n_piece,),
        in_specs=in_specs,
        out_specs=pl.BlockSpec((EXPERT_BLOCK, d), lambda i, be, nb: (first_block + local(i, nb), 0)),
        scratch_shapes=[pltpu.VMEM((d, 2 * D_EXPERT), BF16), pltpu.VMEM((D_EXPERT, d), BF16)],
    )
    return pl.pallas_call(
        functools.partial(_expert_kernel, first_block=first_block),
        out_shape=jax.ShapeDtypeStruct((n_blocks_total * EXPERT_BLOCK, d), BF16),
        grid_spec=grid_spec,
        input_output_aliases=aliases,
        compiler_params=pltpu.CompilerParams(dimension_semantics=("arbitrary",), vmem_limit_bytes=EXPERT_VMEM_LIMIT),
        name="expert_blocks",
    )(*args)


ROUTE_DEST, ROUTE_PROB = 0, TOP_K


def _route_kernel(lg_ref, out_ref, cnt_ref, run_ref, base_ref):
    phase = pl.program_id(0)
    i = pl.program_id(1)
    tm = lg_ref.shape[0]
    lane = lax.broadcasted_iota(jnp.int32, (1, LANES), 1).astype(F32)
    cur = jnp.where(lane < N_EXPERTS, lg_ref[...], -jnp.inf)
    vals, picks = [], []
    member = jnp.zeros((tm, LANES), F32)
    for _ in range(TOP_K):
        v = jnp.max(cur, -1, keepdims=True)
        pick = lane == jnp.min(jnp.where(cur == v, lane, float(LANES)), -1, keepdims=True)
        member = member + pick.astype(F32)
        cur = jnp.where(pick, -jnp.inf, cur)
        vals.append(v)
        picks.append(pick)
    tile_counts = jnp.sum(member, 0, keepdims=True)

    @pl.when(phase == 0)
    def _():
        @pl.when(i == 0)
        def _():
            cnt_ref[...] = jnp.zeros_like(cnt_ref)
        cnt_ref[...] = cnt_ref[...] + tile_counts

    @pl.when(phase == 1)
    def _():
        @pl.when(i == 0)
        def _():
            blocks = jnp.floor((cnt_ref[...] + (EXPERT_BLOCK - 1)) * (1.0 / EXPERT_BLOCK))
            r = lax.broadcasted_iota(jnp.int32, (LANES, LANES), 0)
            c = lax.broadcasted_iota(jnp.int32, (LANES, LANES), 1)
            before = _split_dot_lhs(jnp.broadcast_to(blocks, (SUBLANES, LANES)), (r < c).astype(BF16))
            base_ref[...] = before[0:1] * float(EXPERT_BLOCK)
            run_ref[...] = jnp.zeros_like(run_ref)

        r = lax.broadcasted_iota(jnp.int32, (tm, tm), 0)
        c = lax.broadcasted_iota(jnp.int32, (tm, tm), 1)
        earlier = jnp.dot((c < r).astype(BF16), member.astype(BF16), preferred_element_type=F32)
        slot = base_ref[...] + run_ref[...] + earlier
        denom = 1.0
        exps = [1.0]
        for k in range(1, TOP_K):
            exps.append(jnp.exp(vals[k] - vals[0]))
            denom = denom + exps[k]
        out_lane = lax.broadcasted_iota(jnp.int32, (1, LANES), 1)
        row = jnp.zeros((tm, LANES), F32)
        for k in range(TOP_K):
            dest = jnp.sum(jnp.where(picks[k], slot, 0.0), -1, keepdims=True)
            row = jnp.where(out_lane == ROUTE_DEST + k, dest, row)
            row = jnp.where(out_lane == ROUTE_PROB + k, exps[k] / denom, row)
        out_ref[...] = row
        run_ref[...] = run_ref[...] + tile_counts


def route(logits):
    n = logits.shape[0]
    tm = ROW_TILE
    return pl.pallas_call(
        _route_kernel,
        out_shape=(jax.ShapeDtypeStruct((n, LANES), F32), jax.ShapeDtypeStruct((1, LANES), F32)),
        grid=(2, n // tm),
        in_specs=[pl.BlockSpec((tm, LANES), lambda ph, i: (i, 0))],
        out_specs=(pl.BlockSpec((tm, LANES), lambda ph, i: (i * ph, 0)), pl.BlockSpec((1, LANES), lambda ph, i: (0, 0))),
        scratch_shapes=[pltpu.VMEM((1, LANES), F32), pltpu.VMEM((1, LANES), F32)],
        compiler_params=_params("arbitrary", "arbitrary"),
        name="route",
    )(logits)


def _combine_norm_kernel(x_ref, y0_ref, y1_ref, y2_ref, y3_ref, rt_ref, mod_ref, ln_ref, o_ref):
    f = None
    for k, y_ref in enumerate((y0_ref, y1_ref, y2_ref, y3_ref)):
        term = rt_ref[0, :, ROUTE_PROB + k:ROUTE_PROB + k + 1] * y_ref[0].astype(F32)
        f = term if f is None else f + term
    x2 = DN_ALPHA * x_ref[0] + mod_ref[0, 0, 5:6] * f
    o_ref[0] = _normalize(x2, 1e-5) * ln_ref[0:1] + ln_ref[1:2]


def combine_norm(x1, ys, table, mods, ln, first_tile):
    b, t, d = x1.shape
    tm = ROW_TILE
    n_tiles = t // tm - first_tile
    y_spec = pl.BlockSpec((1, tm, d), lambda bi, i: (bi, i, 0))
    return pl.pallas_call(
        _combine_norm_kernel,
        out_shape=jax.ShapeDtypeStruct((b, n_tiles * tm, d), F32),
        grid=(b, n_tiles),
        in_specs=[
            pl.BlockSpec((1, tm, d), lambda bi, i: (bi, i + first_tile, 0)),
            y_spec, y_spec, y_spec, y_spec,
            pl.BlockSpec((1, tm, LANES), lambda bi, i: (bi, i, 0)),
            pl.BlockSpec((1, 1, 6, d), lambda bi, i: (bi, jnp.minimum(i + first_tile, 1), 0, 0)),
            pl.BlockSpec((2, d), lambda bi, i: (0, 0)),
        ],
        out_specs=pl.BlockSpec((1, tm, d), lambda bi, i: (bi, i, 0)),
        compiler_params=_params("parallel", "parallel"),
        name="combine_norm",
    )(x1, *ys, table, mods, ln)


def _proj_columns():
    o = IN_OFFS
    cols = list(range(o[0], o[7]))
    xbc, dt = o[7], o[8]
    for g in range(SSM_G):
        cols += [xbc + g * SSM_GW + i for i in range(SSM_GW)]
        cols += [xbc + SSM_W + g * SSM_N + i for i in range(SSM_N)]
        cols += [xbc + SSM_W + SSM_G * SSM_N + g * SSM_N + i for i in range(SSM_N)]
        cols += [dt + d * SSM_HEADS + g * SSM_K + k for d in range(2) for k in range(SSM_K)]
        cols += [-1] * (BRANCH_W - SSD_XBC - 2 * SSM_K)
    return np.asarray(cols, np.int32)


def _ssd_conv_columns():
    cols = []
    for g in range(SSM_G):
        cols += [g * SSM_GW + i for i in range(SSM_GW)]
        cols += [SSM_W + g * SSM_N + i for i in range(SSM_N)]
        cols += [SSM_W + SSM_G * SSM_N + g * SSM_N + i for i in range(SSM_N)]
    return np.asarray(cols, np.int32)


def _rope_tables(n_lat, n_ctx):
    rows = n_lat // GRID_W
    row = jnp.repeat(jnp.arange(rows, dtype=F32), GRID_W)
    col = jnp.tile(jnp.arange(GRID_W, dtype=F32), rows)
    inv = ROPE_BASE ** (-jnp.arange(ROPE_F, dtype=F32) / ROPE_F)
    ang = jnp.stack([row[:, None] * inv, col[:, None] * inv], axis=1)
    cos = jnp.broadcast_to(jnp.cos(ang)[:, None, :, None, :], (n_lat, 2, 2, 2, ROPE_F))
    sin = jnp.sin(ang)[:, None, :, None, :] * jnp.asarray([-1.0, 1.0], F32)[None, None, None, :, None]
    sin = jnp.broadcast_to(sin, (n_lat, 2, 2, 2, ROPE_F))
    hw = 2 * ATT_DIM
    cos = jnp.concatenate([jnp.ones((n_ctx, hw), F32), cos.reshape(n_lat, hw)], axis=0)
    sin = jnp.concatenate([jnp.zeros((n_ctx, hw), F32), sin.reshape(n_lat, hw)], axis=0)
    return cos, sin


def _block_diag(w):
    n, r, c = w.shape
    eye = jnp.eye(n, dtype=w.dtype)
    return (w[:, :, None, :] * eye[:, None, :, None]).reshape(n * r, n * c)


def _lane_pad(v, width=LANES):
    return jnp.pad(v, [(0, 0)] * (v.ndim - 1) + [(0, width - v.shape[-1])])


def _slot_tables(dest, counts):
    n = dest.shape[0]
    a = n * TOP_K
    n_blocks = -(-a // EXPERT_BLOCK) + N_EXPERTS
    pblocks = (counts + EXPERT_BLOCK - 1) // EXPERT_BLOCK
    pend_b = jnp.cumsum(pblocks)
    start = jnp.cumsum(counts) - counts
    blk = jnp.arange(n_blocks, dtype=jnp.int32)
    block_e = jnp.minimum(jnp.sum((blk[:, None] >= pend_b[None, :]).astype(jnp.int32), axis=1), N_EXPERTS - 1)
    n_used = pend_b[N_EXPERTS - 1:].astype(jnp.int32)
    tok = jnp.arange(a, dtype=jnp.int32) // TOP_K
    _, sorted_tok = lax.sort_key_val(dest.reshape(a), tok)
    slots = n_blocks * EXPERT_BLOCK
    max_pad = N_EXPERTS * EXPERT_BLOCK
    padded = jnp.concatenate([jnp.zeros((max_pad,), jnp.int32), sorted_tok, jnp.zeros((slots - a,), jnp.int32)])
    shift = (pend_b - pblocks) * EXPERT_BLOCK - start
    onehot = block_e[:, None] == jnp.arange(N_EXPERTS)[None, :]
    first_blk = jnp.sum(jnp.where(onehot, (pend_b - pblocks)[None, :], 0), axis=1)
    n_valid = jnp.sum(jnp.where(onehot, counts[None, :], 0), axis=1) - (blk - first_blk) * EXPERT_BLOCK
    valid = (blk < n_used[0])[:, None] & (jnp.arange(EXPERT_BLOCK)[None, :] < n_valid[:, None])
    slot_tok = jnp.zeros((n_blocks, EXPERT_BLOCK), jnp.int32)
    for e in range(N_EXPERTS):
        moved = lax.dynamic_slice(padded, (max_pad - shift[e],), (slots,)).reshape(n_blocks, EXPERT_BLOCK)
        slot_tok = jnp.where(valid & (block_e == e)[:, None], moved, slot_tok)
    return block_e, n_used, slot_tok.reshape(-1)


EXPERT_PIECES = 4


def kernel(x, c, ctx, c_ctx, w_ada, b_ada, w_in, rnn_conv_w, rnn_conv_b, rnn_wa, rnn_ba, rnn_wx, rnn_bx, rnn_lam, att_lambda, att_subln, pool_w, pool_b, pool_scale, ssm_conv_w, ssm_conv_b, ssm_dt_bias, ssm_a_log, ssm_d, ssm_norm, w_branch, w_out, ln1_g, ln1_b, ln2_g, ln2_b, router_w, router_b, w_gate_up, b_gate_up, w_down, b_down):
    bsz, n_lat, d = x.shape
    n_ctx = ctx.shape[1]
    total = n_ctx + n_lat
    assert n_ctx == ROW_TILE and n_lat % ROW_TILE == 0 and n_lat % GRID_W == 0
    ctx_tiles = n_ctx // ROW_TILE
    cos_t, sin_t = _rope_tables(n_lat, n_ctx)

    pcols = _proj_columns()
    w_main = jnp.where(pcols[None, None, :] >= 0, jnp.take(w_in, jnp.maximum(pcols, 0), axis=2), 0.0).astype(BF16)
    w_gates = w_in[:, :, IN_OFFS[9]:].astype(BF16)
    w_branch_b = w_branch.astype(BF16)
    w_out_b = w_out.astype(BF16)
    w_router = _lane_pad(router_w).astype(BF16)
    b_router = _lane_pad(router_b)[:, None, :]
    n_cg = RNN_W // LANES
    bpg = LANES // RNN_BLOCK
    gate_blocks = jnp.stack([rnn_wa[:, 0], rnn_wx[:, 0], rnn_wa[:, 1], rnn_wx[:, 1]], axis=1)
    gate_blocks = gate_blocks.reshape(DEPTH, 4, n_cg, bpg, RNN_BLOCK, RNN_BLOCK)
    rnn_wg = jax.vmap(jax.vmap(jax.vmap(_block_diag)))(gate_blocks)
    rnn_wg = rnn_wg.transpose(0, 2, 3, 1, 4).reshape(DEPTH, n_cg, LANES, 4 * LANES).astype(BF16)
    gate_bias = jnp.stack([rnn_ba[:, 0], rnn_bx[:, 0], rnn_ba[:, 1], rnn_bx[:, 1]], axis=1)
    rnn_bg = gate_bias.reshape(DEPTH, 4, n_cg, LANES).transpose(0, 2, 1, 3).reshape(DEPTH, n_cg, 1, 4 * LANES)
    rnn_sp = jax.nn.softplus(-rnn_lam).reshape(DEPTH, 2, n_cg, LANES).transpose(0, 2, 1, 3).reshape(DEPTH, n_cg, 1, 2 * LANES)
    ccols = _ssd_conv_columns()
    ssd_cw = jnp.take(ssm_conv_w, ccols, axis=2).reshape(DEPTH, CONV_W, SSM_G, SSD_XBC).transpose(0, 2, 1, 3)
    ssd_cb = jnp.take(ssm_conv_b, ccols, axis=1).reshape(DEPTH, SSM_G, 1, SSD_XBC)
    per_group = lambda v: v.reshape(DEPTH, 2, SSM_G, SSM_K).transpose(0, 2, 1, 3).reshape(DEPTH, SSM_G, 1, 2 * SSM_K)
    ssd_dtb = _lane_pad(per_group(ssm_dt_bias))
    ssd_alog = _lane_pad(per_group(ssm_a_log))
    ssd_dsk = jnp.repeat((ssm_d[:, 0] + ssm_d[:, 1]).reshape(DEPTH, SSM_G, 1, SSM_K), SSM_P, axis=-1)
    ssd_nw = ssm_norm.reshape(DEPTH, SSM_G, 1, SSM_GW)
    pool_wb = pool_w.astype(BF16)

    xs = jnp.concatenate([ctx, x], axis=1)
    out = None
    for li in range(DEPTH):
        need_ctx = li < DEPTH - 1
        first = 0 if need_ctx else ctx_tiles
        mod_l = jax.nn.silu(c) @ w_ada[li] + b_ada[li]
        mod_c = jnp.broadcast_to(jax.nn.silu(c_ctx) @ w_ada[li] + b_ada[li], mod_l.shape)
        mods = jnp.stack([mod_c, mod_l], axis=1).reshape(bsz, 2, 6, d)

        proj = in_proj(xs, mods, w_main[li], n_ctx, 768 if total % 768 == 0 else ROW_TILE, BRANCH_W * 3 if PROJ_W % (BRANCH_W * 3) == 0 else BRANCH_W)
        y_rnn = rglru(proj, rnn_conv_w[li], rnn_conv_b[li][None], rnn_wg[li], rnn_bg[li], rnn_sp[li], n_ctx)
        lam_init = 0.8 - 0.6 * math.exp(-0.3 * li)
        lv = att_lambda[li]
        lam = jnp.exp(jnp.sum(lv[0] * lv[1])) - jnp.exp(jnp.sum(lv[2] * lv[3])) + lam_init
        y_att = diff_attention(proj, cos_t, sin_t, att_subln[li], lam, jnp.asarray(1.0 - lam_init, F32), n_ctx, need_ctx)
        y_pool = pool(proj, pool_wb[li], pool_b[li][None], pool_scale[li][None], n_ctx)
        y_ssd = ssd(proj, ssd_cw[li], ssd_cb[li], ssd_dtb[li], ssd_alog[li], ssd_dsk[li], ssd_nw[li], n_ctx)
        ln1 = jnp.stack([ln1_g[li], ln1_b[li]])
        x1, h2, logits = merge(xs, mods, (y_rnn, y_att, y_pool, y_ssd), w_gates[li], w_branch_b[li], w_out_b[li], ln1,
                               w_router[li], b_router[li], first)

        n_rout = total - first * ROW_TILE
        table, counts = route(logits[:, first * ROW_TILE:].reshape(bsz * n_rout, LANES))
        dest = table[:, ROUTE_DEST:ROUTE_DEST + TOP_K].astype(jnp.int32)
        block_e, n_used, slot_tok = _slot_tables(dest, counts[0, :N_EXPERTS].astype(jnp.int32))
        slot_row = slot_tok + (slot_tok // n_rout + 1) * (first * ROW_TILE)
        h2_rows = h2.reshape(bsz * total, d)
        n_blocks = block_e.shape[0]
        pieces = EXPERT_PIECES if n_blocks % EXPERT_PIECES == 0 else 1
        piece_rows = n_blocks // pieces * EXPERT_BLOCK
        yb = None
        for p in range(pieces):
            xb = h2_rows[slot_row[p * piece_rows:(p + 1) * piece_rows]]
            yb = expert_blocks(li, block_e, n_used, xb, yb, p * (n_blocks // pieces), n_blocks,
                               w_gate_up, b_gate_up, w_down, b_down)
        ys = [yb[dest[:, k]].reshape(bsz, n_rout, d) for k in range(TOP_K)]
        ln2 = jnp.stack([ln2_g[li], ln2_b[li]])
        out = combine_norm(x1, ys, table.reshape(bsz, n_rout, LANES), mods, ln2, first)
        xs = out
    return out
```

```python
import functools
import math

import jax
import jax.numpy as jnp
import numpy as np
from jax import lax
from jax.experimental import pallas as pl
from jax.experimental.pallas import tpu as pltpu

D_MODEL = 1024
DEPTH = 4
GRID_W = 64
N_BRANCH = 4
BRANCH_W = D_MODEL // 2
RNN_W = BRANCH_W
RNN_BLOCK = 64
RG_C = 8.0
CONV_W = 4
ATT_DIM = 64
ATT_HEADS = BRANCH_W // (2 * ATT_DIM)
ATT_V_W = ATT_HEADS * 2 * ATT_DIM
ROPE_F = ATT_DIM // 4
ROPE_BASE = 10000.0
POOL_W = BRANCH_W
POOL_WINDOWS = (2, 4, 8, 16)
POOL_GROUPS = len(POOL_WINDOWS)
POOL_GW = POOL_W // POOL_GROUPS
SSM_W = BRANCH_W
SSM_P = 64
SSM_HEADS = SSM_W // SSM_P
SSM_G = 2
SSM_K = SSM_HEADS // SSM_G
SSM_N = 64
SSM_GW = SSM_W // SSM_G
SSM_CHUNK = 128
N_EXPERTS = 32
TOP_K = 4
D_EXPERT = D_MODEL
SWIGLU_LIMIT = 7.0
SWIGLU_ALPHA = 1.702
EXPERT_BLOCK = 256
DN_ALPHA = (2 * DEPTH) ** 0.25
IN_WIDTHS = (RNN_W, RNN_W, BRANCH_W, BRANCH_W, ATT_V_W, POOL_W, SSM_W, SSM_W + 2 * SSM_G * SSM_N, 2 * SSM_HEADS, N_BRANCH * D_MODEL)
IN_OFFS = tuple(int(v) for v in np.cumsum((0,) + IN_WIDTHS))

LANES = 128
SUBLANES = 8
VMEM_LIMIT = 48 * 1024 * 1024
ROW_TILE = 256
EXPERT_VMEM_LIMIT = 54 * 1024 * 1024
MERGE_VMEM_LIMIT = 56 * 1024 * 1024

COL_RX, COL_RG, COL_Q, COL_K, COL_V, COL_PU, COL_Z, COL_SSD = 0, 1, 2, 3, 4, 5, 6, 7
PROJ_W = (COL_SSD + SSM_G) * BRANCH_W
SSD_XBC = SSM_GW + 2 * SSM_N

BF16 = jnp.bfloat16
F32 = jnp.float32


def _params(*sem):
    return pltpu.CompilerParams(dimension_semantics=sem, vmem_limit_bytes=VMEM_LIMIT)


def _normalize(x, eps):
    mu = jnp.mean(x, -1, keepdims=True)
    xc = x - mu
    var = jnp.mean(xc * xc, -1, keepdims=True)
    return xc * lax.rsqrt(var + eps)


def _segment_valid(t, off, n_ctx, total):
    lo = jnp.where(t < n_ctx, 0, n_ctx)
    hi = jnp.where(t < n_ctx, n_ctx, total)
    return (t + off >= lo) & (t + off < hi)


def _shift_rows(u, off, t, n_ctx):
    total = u.shape[0]
    rolled = pltpu.roll(u, (-off) % total, 0)
    return jnp.where(_segment_valid(t, off, n_ctx, total), rolled, 0.0)


def _centred_conv(u, pad_ref, w_ref, b_ref, n_ctx):
    total, c = u.shape
    gap = SUBLANES
    zeros = jnp.zeros((gap, c), F32)
    pad_ref[0:gap, :] = zeros
    pad_ref[gap:gap + n_ctx, :] = u[0:n_ctx]
    pad_ref[gap + n_ctx:2 * gap + n_ctx, :] = zeros
    pad_ref[2 * gap + n_ctx:2 * gap + total, :] = u[n_ctx:total]
    pad_ref[2 * gap + total:3 * gap + total, :] = zeros

    def tap(off):
        return jnp.concatenate([pad_ref[pl.ds(gap + off, n_ctx), :],
                                pad_ref[pl.ds(2 * gap + n_ctx + off, total - n_ctx), :]], axis=0)

    left = CONV_W // 2
    out = b_ref[...] + tap(-left) * w_ref[0:1, :]
    for k in range(1, CONV_W):
        out = out + (u if k == left else tap(k - left)) * w_ref[k:k + 1, :]
    return out


def _in_proj_kernel(x_ref, mod_ref, w_ref, o_ref, h_ref, *, n_ctx):
    @pl.when(pl.program_id(2) == 0)
    def _():
        tm = x_ref.shape[1]
        row = pl.program_id(1) * tm + lax.broadcasted_iota(jnp.int32, (tm, 1), 0)
        is_ctx = row < n_ctx
        shift = jnp.where(is_ctx, mod_ref[0, 0, 0:1, :], mod_ref[0, 1, 0:1, :])
        scale = jnp.where(is_ctx, mod_ref[0, 0, 1:2, :], mod_ref[0, 1, 1:2, :])
        h_ref[...] = (_normalize(x_ref[0], 1e-6) * (1.0 + scale) + shift).astype(BF16)

    o_ref[0] = jnp.dot(h_ref[...], w_ref[...], preferred_element_type=F32)


def in_proj(x, mods, w, n_ctx, tm, tn):
    b, t, d = x.shape
    n = w.shape[1]
    return pl.pallas_call(
        functools.partial(_in_proj_kernel, n_ctx=n_ctx),
        out_shape=jax.ShapeDtypeStruct((b, t, n), F32),
        grid=(b, t // tm, n // tn),
        in_specs=[
            pl.BlockSpec((1, tm, d), lambda bi, i, j: (bi, i, 0)),
            pl.BlockSpec((1, 2, 6, d), lambda bi, i, j: (bi, 0, 0, 0)),
            pl.BlockSpec((d, tn), lambda bi, i, j: (0, j)),
        ],
        out_specs=pl.BlockSpec((1, tm, tn), lambda bi, i, j: (bi, i, j)),
        scratch_shapes=[pltpu.VMEM((tm, d), BF16)],
        compiler_params=_params("parallel", "parallel", "arbitrary"),
        name="in_proj",
    )(x, mods, w)


SCAN_ROWS = 64


def _tile_scan(a, b, reverse):
    rows = a.shape[0]
    sub = lax.broadcasted_iota(jnp.int32, (rows, 1), 0) % SUBLANES
    s = 1
    while s < SUBLANES:
        shift = (rows - s) if reverse else s
        keep = (sub + s < SUBLANES) if reverse else (sub >= s)
        a_sh = pltpu.roll(a, shift, 0)
        b_sh = pltpu.roll(b, shift, 0)
        b = jnp.where(keep, a * b_sh + b, b)
        a = jnp.where(keep, a * a_sh, a)
        s *= 2
    return a, b


def _rglru_kernel(rx_ref, rg_ref, cw_ref, cb_ref, wg_ref, bg_ref, sp_ref, o_ref, a_ref, b_ref, y_ref, pad_ref, *, n_ctx):
    total = rx_ref.shape[1]
    cw = rx_ref.shape[2]
    u = _centred_conv(rx_ref[0], pad_ref, cw_ref, cb_ref, n_ctx)
    g = jnp.dot(u.astype(BF16), wg_ref[0], preferred_element_type=F32) + bg_ref[0]
    n_tiles = SCAN_ROWS // SUBLANES

    for d in range(2):
        r = jax.nn.sigmoid(g[:, (2 * d) * cw:(2 * d + 1) * cw])
        i = jax.nn.sigmoid(g[:, (2 * d + 1) * cw:(2 * d + 2) * cw])
        log_a = -RG_C * r * sp_ref[0, :, d * cw:(d + 1) * cw]
        a = jnp.exp(log_a)
        a_ref[d] = a
        b_ref[d] = jnp.sqrt((1.0 - a) * (1.0 + a)) * (i * u)

    def scan_rows(r0, h, d):
        reverse = d == 1
        a, b = _tile_scan(a_ref[d, pl.ds(r0, SCAN_ROWS), :], b_ref[d, pl.ds(r0, SCAN_ROWS), :], reverse)
        outs = [None] * n_tiles
        for j in (range(n_tiles - 1, -1, -1) if reverse else range(n_tiles)):
            sl = slice(j * SUBLANES, (j + 1) * SUBLANES)
            hj = b[sl] + a[sl] * h
            h = hj[0:1] if reverse else hj[SUBLANES - 1:SUBLANES]
            outs[j] = hj
        y_ref[d, pl.ds(r0, SCAN_ROWS), :] = jnp.concatenate(outs, axis=0)
        return h

    def step(it, hs, lo, n_steps):
        h_f = scan_rows(pl.multiple_of(lo + it * SCAN_ROWS, SCAN_ROWS), hs[0], 0)
        h_b = scan_rows(pl.multiple_of(lo + (n_steps - 1 - it) * SCAN_ROWS, SCAN_ROWS), hs[1], 1)
        return h_f, h_b

    h0 = jnp.zeros((1, cw), F32)
    hs = lax.fori_loop(0, n_ctx // SCAN_ROWS, functools.partial(step, lo=0, n_steps=n_ctx // SCAN_ROWS), (h0, h0))
    n_lat_steps = (total - n_ctx) // SCAN_ROWS
    lax.fori_loop(0, n_lat_steps, functools.partial(step, lo=n_ctx, n_steps=n_lat_steps), hs)

    o_ref[0] = (jax.nn.gelu(rg_ref[0]) * (y_ref[0] + y_ref[1])).astype(BF16)


def rglru(proj, conv_w, conv_b, wg, bg, sp, n_ctx):
    b, t, _ = proj.shape
    cw = LANES
    n_cg = RNN_W // cw
    per_block = BRANCH_W // cw
    return pl.pallas_call(
        functools.partial(_rglru_kernel, n_ctx=n_ctx),
        out_shape=jax.ShapeDtypeStruct((b, t, RNN_W), BF16),
        grid=(b, n_cg),
        in_specs=[
            pl.BlockSpec((1, t, cw), lambda bi, c: (bi, 0, COL_RX * per_block + c)),
            pl.BlockSpec((1, t, cw), lambda bi, c: (bi, 0, COL_RG * per_block + c)),
            pl.BlockSpec((CONV_W, cw), lambda bi, c: (0, c)),
            pl.BlockSpec((1, cw), lambda bi, c: (0, c)),
            pl.BlockSpec((1, cw, 4 * cw), lambda bi, c: (c, 0, 0)),
            pl.BlockSpec((1, 1, 4 * cw), lambda bi, c: (c, 0, 0)),
            pl.BlockSpec((1, 1, 2 * cw), lambda bi, c: (c, 0, 0)),
        ],
        out_specs=pl.BlockSpec((1, t, cw), lambda bi, c: (bi, 0, c)),
        scratch_shapes=[pltpu.VMEM((2, t, cw), F32), pltpu.VMEM((2, t, cw), F32), pltpu.VMEM((2, t, cw), F32),
                        pltpu.VMEM((t + 3 * SUBLANES, cw), F32)],
        compiler_params=_params("parallel", "parallel"),
        name="rglru",
    )(proj, proj, conv_w, conv_b, wg, bg, sp)


def _pool_kernel(u_ref, w_ref, b_ref, s_ref, o_ref, *, n_ctx):
    total = u_ref.shape[1]
    t = lax.broadcasted_iota(jnp.int32, (total, 1), 0)
    for gi, win in enumerate(POOL_WINDOWS):
        cols = slice(gi * POOL_GW, (gi + 1) * POOL_GW)
        u = u_ref[0, :, cols]
        acc = u
        cnt = jnp.ones((total, 1), F32)
        for off in range(-(win // 2), win - win // 2):
            if off == 0:
                continue
            acc = acc + _shift_rows(u, off, t, n_ctx)
            cnt = cnt + _segment_valid(t, off, n_ctx, total).astype(F32)
        mix = acc / cnt - u
        y = jnp.dot(mix.astype(BF16), w_ref[gi], preferred_element_type=F32) + b_ref[:, cols]
        o_ref[0, :, cols] = (y * s_ref[:, cols]).astype(BF16)


def pool(proj, w, bias, scale, n_ctx):
    b, t, _ = proj.shape
    return pl.pallas_call(
        functools.partial(_pool_kernel, n_ctx=n_ctx),
        out_shape=jax.ShapeDtypeStruct((b, t, POOL_W), BF16),
        grid=(b,),
        in_specs=[
            pl.BlockSpec((1, t, POOL_W), lambda bi: (bi, 0, COL_PU)),
            pl.BlockSpec((POOL_GROUPS, POOL_GW, POOL_GW), lambda bi: (0, 0, 0)),
            pl.BlockSpec((1, POOL_W), lambda bi: (0, 0)),
            pl.BlockSpec((1, POOL_W), lambda bi: (0, 0)),
        ],
        out_specs=pl.BlockSpec((1, t, POOL_W), lambda bi: (bi, 0, 0)),
        compiler_params=_params("parallel"),
        name="pool",
    )(proj, w, bias, scale)


def _bf16_pieces(v):
    hi = v.astype(BF16)
    r1 = v - hi.astype(F32)
    mid = r1.astype(BF16)
    lo = (r1 - mid.astype(F32)).astype(BF16)
    return hi, mid, lo


def _split_dot(tri, v):
    return sum(jnp.dot(tri, p, preferred_element_type=F32) for p in _bf16_pieces(v))


def _split_dot_lhs(v, tri):
    return sum(jnp.dot(p, tri, preferred_element_type=F32) for p in _bf16_pieces(v))


def _per_head(cols, width):
    head = lax.broadcasted_iota(jnp.int32, (1, width), 1) // SSM_P
    out = cols[SSM_K - 1]
    for k in range(SSM_K - 2, -1, -1):
        out = jnp.where(head == k, cols[k], out)
    return out


def _ssd_kernel(blk_ref, z_ref, cw_ref, cb_ref, dtb_ref, alog_ref, dsk_ref, nw_ref, o_ref,
                u_ref, dt_ref, adt_ref, tr_ref, y_ref, st_ref, pad_ref, *, n_ctx):
    total = blk_ref.shape[1]
    ck = SSM_CHUNK
    n_chunks = total // ck
    ctx_chunks = n_ctx // ck
    u_ref[...] = jax.nn.silu(_centred_conv(blk_ref[0, :, 0:SSD_XBC], pad_ref, cw_ref.at[0], cb_ref.at[0], n_ctx))
    dt = jax.nn.softplus(blk_ref[0, :, SSD_XBC:SSD_XBC + LANES] + dtb_ref[0])
    dt_ref[...] = dt
    adt_ref[...] = dt * (-jnp.exp(alog_ref[0]))
    for c in range(n_chunks):
        rows = slice(c * ck, (c + 1) * ck)
        tr_ref[c, 0:ck, :] = u_ref[rows, SSM_GW:SSM_GW + 2 * SSM_N].T
        tr_ref[c, ck:2 * ck, :] = adt_ref[rows, :].T

    ri = lax.broadcasted_iota(jnp.int32, (ck, ck), 0)
    ci = lax.broadcasted_iota(jnp.int32, (ck, ck), 1)
    lower = (ci <= ri)
    lower_b = lower.astype(BF16)
    upper_b = (ci >= ri).astype(BF16)
    lane = lax.broadcasted_iota(jnp.int32, (1, LANES), 1)
    head_w = lax.broadcasted_iota(jnp.int32, (1, SSM_GW), 1) // SSM_P
    srow = lax.broadcasted_iota(jnp.int32, (ck, 1), 0)

    def chunk(c, d):
        reverse = d == 1
        tri_col, tri_row = (upper_b, lower_b) if reverse else (lower_b, upper_b)
        causal = (ci >= ri) if reverse else lower
        r0 = pl.multiple_of(c * ck, ck)
        xs = u_ref[pl.ds(r0, ck), 0:SSM_GW]
        bc = u_ref[pl.ds(r0, ck), SSM_GW:SSM_GW + 2 * SSM_N]
        tr = tr_ref[c, 0:ck, :]
        c_lo = jnp.where(lane < SSM_N, pltpu.roll(bc, SSM_N, 1), 0.0).astype(BF16)
        cb = jnp.dot(c_lo, tr.astype(BF16), preferred_element_type=F32)
        cs_col = _split_dot(tri_col, adt_ref[pl.ds(r0, ck), :])
        cs_row = _split_dot_lhs(tr_ref[c, ck:2 * ck, :], tri_row)
        edge = ck - 1 if not reverse else 0
        dtc = dt_ref[pl.ds(r0, ck), :]
        cols, tots, dts = [], [], []
        y = jnp.zeros((ck, SSM_GW), F32)
        for k in range(SSM_K):
            j = d * SSM_K + k
            cols.append(cs_col[:, j:j + 1])
            tots.append(cs_col[edge:edge + 1, j:j + 1])
            dts.append(dtc[:, j:j + 1])
        xdt = xs * _per_head(dts, SSM_GW)
        for k in range(SSM_K):
            j = d * SSM_K + k
            seg = cols[k] - cs_row[j:j + 1, :]
            m = (cb * jnp.where(causal, jnp.exp(seg), 0.0)).astype(BF16)
            y = y + jnp.dot(m, jnp.where(head_w == k, xdt, 0.0).astype(BF16), preferred_element_type=F32)
        col_w = _per_head(cols, SSM_GW)
        tot_w = _per_head(tots, SSM_GW)
        state = st_ref[d]
        y = y + jnp.dot(c_lo, state.astype(BF16), preferred_element_type=F32) * jnp.exp(col_w)
        xd = (xdt * jnp.exp(tot_w - col_w)).astype(BF16)
        new = jnp.exp(tot_w) * state + jnp.dot(tr.astype(BF16), xd, preferred_element_type=F32)
        st_ref[d] = jnp.where(srow < SSM_N, new, 0.0)
        y_ref[d, pl.ds(r0, ck), :] = y

    def both(it, carry, lo, n_steps):
        chunk(lo + it, 0)
        chunk(lo + n_steps - 1 - it, 1)
        return carry

    st_ref[...] = jnp.zeros_like(st_ref)
    lax.fori_loop(0, ctx_chunks, functools.partial(both, lo=0, n_steps=ctx_chunks), 0)
    lat_chunks = n_chunks - ctx_chunks
    lax.fori_loop(0, lat_chunks, functools.partial(both, lo=ctx_chunks, n_steps=lat_chunks), 0,
                  unroll=2 if lat_chunks % 2 == 0 else 1)

    y = y_ref[0] + y_ref[1] + dsk_ref[0] * u_ref[:, 0:SSM_GW]
    g = y * jax.nn.silu(z_ref[0])
    g = g * lax.rsqrt(jnp.mean(g * g, -1, keepdims=True) + 1e-5)
    o_ref[0] = (g * nw_ref[0]).astype(BF16)


def ssd(proj, conv_w, conv_b, dt_bias, a_log, d_skip, norm_w, n_ctx):
    b, t, _ = proj.shape
    n_chunks = t // SSM_CHUNK
    z_per = BRANCH_W // SSM_GW
    return pl.pallas_call(
        functools.partial(_ssd_kernel, n_ctx=n_ctx),
        out_shape=jax.ShapeDtypeStruct((b, t, SSM_W), BF16),
        grid=(b, SSM_G),
        in_specs=[
            pl.BlockSpec((1, t, BRANCH_W), lambda bi, g: (bi, 0, COL_SSD + g)),
            pl.BlockSpec((1, t, SSM_GW), lambda bi, g: (bi, 0, COL_Z * z_per + g)),
            pl.BlockSpec((1, CONV_W, SSD_XBC), lambda bi, g: (g, 0, 0)),
            pl.BlockSpec((1, 1, SSD_XBC), lambda bi, g: (g, 0, 0)),
            pl.BlockSpec((1, 1, LANES), lambda bi, g: (g, 0, 0)),
            pl.BlockSpec((1, 1, LANES), lambda bi, g: (g, 0, 0)),
            pl.BlockSpec((1, 1, SSM_GW), lambda bi, g: (g, 0, 0)),
            pl.BlockSpec((1, 1, SSM_GW), lambda bi, g: (g, 0, 0)),
        ],
        out_specs=pl.BlockSpec((1, t, SSM_GW), lambda bi, g: (bi, 0, g)),
        scratch_shapes=[
            pltpu.VMEM((t, SSD_XBC), F32),
            pltpu.VMEM((t, LANES), F32),
            pltpu.VMEM((t, LANES), F32),
            pltpu.VMEM((n_chunks, 2 * SSM_CHUNK, SSM_CHUNK), F32),
            pltpu.VMEM((2, t, SSM_GW), F32),
            pltpu.VMEM((2, SSM_CHUNK, SSM_GW), F32),
            pltpu.VMEM((t + 3 * SUBLANES, SSD_XBC), F32),
        ],
        compiler_params=_params("parallel", "parallel"),
        name="ssd",
    )(proj, proj, conv_w, conv_b, dt_bias, a_log, d_skip, norm_w)


def _rope(x, cos, sin):
    lane = lax.broadcasted_iota(jnp.int32, (1, x.shape[1]), 1)
    partner = jnp.where(lane % (2 * ROPE_F) < ROPE_F, pltpu.roll(x, x.shape[1] - ROPE_F, 1), pltpu.roll(x, ROPE_F, 1))
    return x * cos + partner * sin


Q_TILES = (1024, 512, 256)


def _diff_attn_kernel(par_ref, q_ref, k_ref, v_ref, cos_ref, sin_ref, g_ref, o_ref, kb_ref, vb_ref,
                      *, n_ctx, need_ctx, q_tile):
    lam = par_ref[0]
    out_scale = par_ref[1]
    total = k_ref.shape[1]
    kb_ref[...] = _rope(k_ref[0], cos_ref[...], sin_ref[...]).astype(BF16)
    vb_ref[...] = v_ref[0].astype(BF16)
    dims = (((1,), (1,)), ((), ()))

    def attend(r0, rows, n_keys):
        q = _rope(q_ref[0, pl.ds(r0, rows), :], cos_ref[pl.ds(r0, rows), :], sin_ref[pl.ds(r0, rows), :])
        q = (q * (ATT_DIM ** -0.5 * math.log2(math.e))).astype(BF16)
        lane = lax.broadcasted_iota(jnp.int32, q.shape, 1)
        k = kb_ref[0:n_keys, :]
        v = vb_ref[0:n_keys, :]

        def softmax_v(qh):
            s = lax.dot_general(qh, k, dims, preferred_element_type=F32)
            e = jnp.exp2(s - jnp.max(s, -1, keepdims=True))
            return jnp.dot(e.astype(BF16), v, preferred_element_type=F32) / jnp.sum(e, -1, keepdims=True)

        o = (softmax_v(jnp.where(lane < ATT_DIM, q, jnp.zeros_like(q)))
             - lam * softmax_v(jnp.where(lane >= ATT_DIM, q, jnp.zeros_like(q))))
        o = o * lax.rsqrt(jnp.mean(o * o, -1, keepdims=True) + 1e-5) * g_ref[...] * out_scale
        o_ref[0, pl.ds(r0, rows), :] = o.astype(BF16)

    if need_ctx:
        attend(0, n_ctx, n_ctx)

    def latent_tile(i, carry):
        attend(pl.multiple_of(n_ctx + i * q_tile, ROW_TILE), q_tile, total)
        return carry

    lax.fori_loop(0, (total - n_ctx) // q_tile, latent_tile, 0)


def diff_attention(proj, cos_t, sin_t, subln, lam, out_scale, n_ctx, need_ctx):
    b, t, _ = proj.shape
    hw = 2 * ATT_DIM
    per_block = BRANCH_W // hw
    q_tile = next(q for q in Q_TILES if (t - n_ctx) % q == 0)
    par = jnp.stack([lam, out_scale]).astype(F32)
    col = lambda block: (lambda bi, h: (bi, 0, block * per_block + h))
    whole = lambda bi, h: (0, 0)
    return pl.pallas_call(
        functools.partial(_diff_attn_kernel, n_ctx=n_ctx, need_ctx=need_ctx, q_tile=q_tile),
        out_shape=jax.ShapeDtypeStruct((b, t, ATT_V_W), BF16),
        grid=(b, ATT_HEADS),
        in_specs=[
            pl.BlockSpec(memory_space=pltpu.SMEM),
            pl.BlockSpec((1, t, hw), col(COL_Q)),
            pl.BlockSpec((1, t, hw), col(COL_K)),
            pl.BlockSpec((1, t, hw), col(COL_V)),
            pl.BlockSpec((t, hw), whole),
            pl.BlockSpec((t, hw), whole),
            pl.BlockSpec((1, hw), whole),
        ],
        out_specs=pl.BlockSpec((1, t, hw), lambda bi, h: (bi, 0, h)),
        scratch_shapes=[pltpu.VMEM((t, hw), BF16), pltpu.VMEM((t, hw), BF16)],
        compiler_params=_params("parallel", "parallel"),
        name="diff_attention",
    )(par, proj, proj, proj, cos_t, sin_t, subln.reshape(1, hw).astype(F32))


def _merge_kernel(x_ref, mod_ref, y0_ref, y1_ref, y2_ref, y3_ref, wg_ref, wb_ref, wo_ref, ln_ref, wr_ref, br_ref,
                  x1_ref, h2_ref, lg_ref):
    x = x_ref[0]
    mod = mod_ref[0, 0]
    h = (_normalize(x, 1e-6) * (1.0 + mod[1:2]) + mod[0:1]).astype(BF16)
    m = None
    for i, y_ref in enumerate((y0_ref, y1_ref, y2_ref, y3_ref)):
        gate = jax.nn.sigmoid(jnp.dot(h, wg_ref[:, i * D_MODEL:(i + 1) * D_MODEL], preferred_element_type=F32))
        term = gate * jnp.dot(y_ref[0], wb_ref[i], preferred_element_type=F32)
        m = term if m is None else m + term
    y = jnp.dot(m.astype(BF16), wo_ref[...], preferred_element_type=F32)
    x1 = _normalize(DN_ALPHA * x + mod[2:3] * y, 1e-5) * ln_ref[0:1] + ln_ref[1:2]
    x1_ref[0] = x1
    h2 = (_normalize(x1, 1e-6) * (1.0 + mod[4:5]) + mod[3:4]).astype(BF16)
    h2_ref[0] = h2
    lg_ref[0] = jnp.dot(h2, wr_ref[...], preferred_element_type=F32) + br_ref[...]


def merge(x, mods, ys, wg, wb, wo, ln, wr, br, first_tile):
    b, t, d = x.shape
    tm = ROW_TILE
    row = lambda bi, i: (bi, i + first_tile, 0)
    const2 = lambda bi, i: (0, 0)
    y_spec = pl.BlockSpec((1, tm, BRANCH_W), row)
    return pl.pallas_call(
        _merge_kernel,
        out_shape=(jax.ShapeDtypeStruct((b, t, d), F32), jax.ShapeDtypeStruct((b, t, d), BF16),
                   jax.ShapeDtypeStruct((b, t, LANES), F32)),
        grid=(b, t // tm - first_tile),
        in_specs=[
            pl.BlockSpec((1, tm, d), row),
            pl.BlockSpec((1, 1, 6, d), lambda bi, i: (bi, jnp.minimum(i + first_tile, 1), 0, 0)),
            y_spec, y_spec, y_spec, y_spec,
            pl.BlockSpec((d, N_BRANCH * d), const2),
            pl.BlockSpec((N_BRANCH, BRANCH_W, d), lambda bi, i: (0, 0, 0)),
            pl.BlockSpec((d, d), const2),
            pl.BlockSpec((2, d), const2),
            pl.BlockSpec((d, LANES), const2),
            pl.BlockSpec((1, LANES), const2),
        ],
        out_specs=(pl.BlockSpec((1, tm, d), row), pl.BlockSpec((1, tm, d), row), pl.BlockSpec((1, tm, LANES), row)),
        compiler_params=pltpu.CompilerParams(dimension_semantics=("parallel", "parallel"),
                                             vmem_limit_bytes=MERGE_VMEM_LIMIT),
        name="merge",
    )(x, mods, *ys, wg, wb, wo, ln, wr, br)


def _expert_kernel(be_ref, nb_ref, x_ref, wgu_ref, bgu_ref, wd_ref, bd_ref, *rest, first_block):
    o_ref, wgu_b, wd_b = rest[-3:]
    i = pl.program_id(0)
    blk = first_block + i

    @pl.when(blk < nb_ref[0])
    def _():
        @pl.when((i == 0) | (be_ref[blk] != be_ref[jnp.maximum(blk - 1, 0)]))
        def _():
            wgu_b[...] = wgu_ref[0, 0].astype(BF16)
            wd_b[...] = wd_ref[0, 0].astype(BF16)

        gu = jnp.dot(x_ref[...], wgu_b[...], preferred_element_type=F32) + bgu_ref[0, 0]
        gate = jnp.minimum(gu[:, :D_EXPERT], SWIGLU_LIMIT)
        up = jnp.clip(gu[:, D_EXPERT:], -SWIGLU_LIMIT, SWIGLU_LIMIT)
        glu = gate * jax.nn.sigmoid(gate * SWIGLU_ALPHA)
        act = ((up + 1.0) * glu).astype(BF16)
        y = jnp.dot(act, wd_b[...], preferred_element_type=F32) + bd_ref[0, 0]
        o_ref[...] = y.astype(BF16)


def expert_blocks(layer, block_e, n_used, xb, yb_prev, first_block, n_blocks_total, wgu, bgu, wd, bd):
    rows, d = xb.shape
    n_piece = rows // EXPERT_BLOCK

    def local(i, nb):
        return jnp.maximum(jnp.minimum(i, nb[0] - 1 - first_block), 0)

    expert = lambda i, be, nb: (layer, be[first_block + local(i, nb)], 0, 0)
    in_specs = [
        pl.BlockSpec((EXPERT_BLOCK, d), lambda i, be, nb: (local(i, nb), 0)),
        pl.BlockSpec((1, 1, d, 2 * D_EXPERT), expert),
        pl.BlockSpec((1, 1, 1, 2 * D_EXPERT), expert),
        pl.BlockSpec((1, 1, D_EXPERT, d), expert),
        pl.BlockSpec((1, 1, 1, d), expert),
    ]
    args = [block_e, n_used, xb, wgu, bgu[:, :, None, :], wd, bd[:, :, None, :]]
    aliases = {}
    if yb_prev is not None:
        in_specs.append(pl.BlockSpec(memory_space=pl.ANY))
        aliases = {len(args): 0}
        args.append(yb_prev)
    grid_spec = pltpu.PrefetchScalarGridSpec(
        num_scalar_prefetch=2,
        grid=(n_piece,),
        in_specs=in_specs,
        out_specs=pl.BlockSpec((EXPERT_BLOCK, d), lambda i, be, nb: (first_block + local(i, nb), 0)),
        scratch_shapes=[pltpu.VMEM((d, 2 * D_EXPERT), BF16), pltpu.VMEM((D_EXPERT, d), BF16)],
    )
    return pl.pallas_call(
        functools.partial(_expert_kernel, first_block=first_block),
        out_shape=jax.ShapeDtypeStruct((n_blocks_total * EXPERT_BLOCK, d), BF16),
        grid_spec=grid_spec,
        input_output_aliases=aliases,
        compiler_params=pltpu.CompilerParams(dimension_semantics=("arbitrary",), vmem_limit_bytes=EXPERT_VMEM_LIMIT),
        name="expert_blocks",
    )(*args)


ROUTE_DEST, ROUTE_PROB = 0, TOP_K
ROUTE_TILES = (1024, 512, 256)


def _route_kernel(lg_ref, out_ref, cnt_ref, run_ref, base_ref):
    phase = pl.program_id(0)
    i = pl.program_id(1)
    tm = lg_ref.shape[0]
    lane = lax.broadcasted_iota(jnp.int32, (1, LANES), 1).astype(F32)
    cur = jnp.where(lane < N_EXPERTS, lg_ref[...], -jnp.inf)
    vals, picks = [], []
    member = jnp.zeros((tm, LANES), F32)
    for _ in range(TOP_K):
        v = jnp.max(cur, -1, keepdims=True)
        pick = lane == jnp.min(jnp.where(cur == v, lane, float(LANES)), -1, keepdims=True)
        member = member + pick.astype(F32)
        cur = jnp.where(pick, -jnp.inf, cur)
        vals.append(v)
        picks.append(pick)
    tile_counts = jnp.sum(member, 0, keepdims=True)

    @pl.when(phase == 0)
    def _():
        @pl.when(i == 0)
        def _():
            cnt_ref[...] = jnp.zeros_like(cnt_ref)
        cnt_ref[...] = cnt_ref[...] + tile_counts

    @pl.when(phase == 1)
    def _():
        @pl.when(i == 0)
        def _():
            blocks = jnp.floor((cnt_ref[...] + (EXPERT_BLOCK - 1)) * (1.0 / EXPERT_BLOCK))
            r = lax.broadcasted_iota(jnp.int32, (LANES, LANES), 0)
            c = lax.broadcasted_iota(jnp.int32, (LANES, LANES), 1)
            before = _split_dot_lhs(jnp.broadcast_to(blocks, (SUBLANES, LANES)), (r < c).astype(BF16))
            base_ref[...] = before[0:1] * float(EXPERT_BLOCK)
            run_ref[...] = jnp.zeros_like(run_ref)

        r = lax.broadcasted_iota(jnp.int32, (tm, tm), 0)
        c = lax.broadcasted_iota(jnp.int32, (tm, tm), 1)
        earlier = jnp.dot((c < r).astype(BF16), member.astype(BF16), preferred_element_type=F32)
        slot = base_ref[...] + run_ref[...] + earlier
        denom = 1.0
        exps = [1.0]
        for k in range(1, TOP_K):
            exps.append(jnp.exp(vals[k] - vals[0]))
            denom = denom + exps[k]
        out_lane = lax.broadcasted_iota(jnp.int32, (1, LANES), 1)
        row = jnp.zeros((tm, LANES), F32)
        for k in range(TOP_K):
            dest = jnp.sum(jnp.where(picks[k], slot, 0.0), -1, keepdims=True)
            row = jnp.where(out_lane == ROUTE_DEST + k, dest, row)
            row = jnp.where(out_lane == ROUTE_PROB + k, exps[k] / denom, row)
        out_ref[...] = row
        run_ref[...] = run_ref[...] + tile_counts


def route(logits):
    n = logits.shape[0]
    tm = next(t for t in ROUTE_TILES if n % t == 0)
    return pl.pallas_call(
        _route_kernel,
        out_shape=(jax.ShapeDtypeStruct((n, LANES), F32), jax.ShapeDtypeStruct((1, LANES), F32)),
        grid=(2, n // tm),
        in_specs=[pl.BlockSpec((tm, LANES), lambda ph, i: (i, 0))],
        out_specs=(pl.BlockSpec((tm, LANES), lambda ph, i: (i * ph, 0)), pl.BlockSpec((1, LANES), lambda ph, i: (0, 0))),
        scratch_shapes=[pltpu.VMEM((1, LANES), F32), pltpu.VMEM((1, LANES), F32)],
        compiler_params=_params("arbitrary", "arbitrary"),
        name="route",
    )(logits)


def _combine_norm_kernel(x_ref, y0_ref, y1_ref, y2_ref, y3_ref, rt_ref, mod_ref, ln_ref, o_ref):
    f = None
    for k, y_ref in enumerate((y0_ref, y1_ref, y2_ref, y3_ref)):
        term = rt_ref[0, :, ROUTE_PROB + k:ROUTE_PROB + k + 1] * y_ref[0].astype(F32)
        f = term if f is None else f + term
    x2 = DN_ALPHA * x_ref[0] + mod_ref[0, 0, 5:6] * f
    o_ref[0] = _normalize(x2, 1e-5) * ln_ref[0:1] + ln_ref[1:2]


def combine_norm(x1, ys, table, mods, ln, first_tile):
    b, t, d = x1.shape
    tm = ROW_TILE
    n_tiles = t // tm - first_tile
    y_spec = pl.BlockSpec((1, tm, d), lambda bi, i: (bi, i, 0))
    return pl.pallas_call(
        _combine_norm_kernel,
        out_shape=jax.ShapeDtypeStruct((b, n_tiles * tm, d), F32),
        grid=(b, n_tiles),
        in_specs=[
            pl.BlockSpec((1, tm, d), lambda bi, i: (bi, i + first_tile, 0)),
            y_spec, y_spec, y_spec, y_spec,
            pl.BlockSpec((1, tm, LANES), lambda bi, i: (bi, i, 0)),
            pl.BlockSpec((1, 1, 6, d), lambda bi, i: (bi, jnp.minimum(i + first_tile, 1), 0, 0)),
            pl.BlockSpec((2, d), lambda bi, i: (0, 0)),
        ],
        out_specs=pl.BlockSpec((1, tm, d), lambda bi, i: (bi, i, 0)),
        compiler_params=_params("parallel", "parallel"),
        name="combine_norm",
    )(x1, *ys, table, mods, ln)


def _proj_columns():
    o = IN_OFFS
    cols = list(range(o[0], o[7]))
    xbc, dt = o[7], o[8]
    for g in range(SSM_G):
        cols += [xbc + g * SSM_GW + i for i in range(SSM_GW)]
        cols += [xbc + SSM_W + g * SSM_N + i for i in range(SSM_N)]
        cols += [xbc + SSM_W + SSM_G * SSM_N + g * SSM_N + i for i in range(SSM_N)]
        cols += [dt + d * SSM_HEADS + g * SSM_K + k for d in range(2) for k in range(SSM_K)]
        cols += [-1] * (BRANCH_W - SSD_XBC - 2 * SSM_K)
    return np.asarray(cols, np.int32)


def _ssd_conv_columns():
    cols = []
    for g in range(SSM_G):
        cols += [g * SSM_GW + i for i in range(SSM_GW)]
        cols += [SSM_W + g * SSM_N + i for i in range(SSM_N)]
        cols += [SSM_W + SSM_G * SSM_N + g * SSM_N + i for i in range(SSM_N)]
    return np.asarray(cols, np.int32)


def _rope_tables(n_lat, n_ctx):
    rows = n_lat // GRID_W
    row = jnp.repeat(jnp.arange(rows, dtype=F32), GRID_W)
    col = jnp.tile(jnp.arange(GRID_W, dtype=F32), rows)
    inv = ROPE_BASE ** (-jnp.arange(ROPE_F, dtype=F32) / ROPE_F)
    ang = jnp.stack([row[:, None] * inv, col[:, None] * inv], axis=1)
    cos = jnp.broadcast_to(jnp.cos(ang)[:, None, :, None, :], (n_lat, 2, 2, 2, ROPE_F))
    sin = jnp.sin(ang)[:, None, :, None, :] * jnp.asarray([-1.0, 1.0], F32)[None, None, None, :, None]
    sin = jnp.broadcast_to(sin, (n_lat, 2, 2, 2, ROPE_F))
    hw = 2 * ATT_DIM
    cos = jnp.concatenate([jnp.ones((n_ctx, hw), F32), cos.reshape(n_lat, hw)], axis=0)
    sin = jnp.concatenate([jnp.zeros((n_ctx, hw), F32), sin.reshape(n_lat, hw)], axis=0)
    return cos, sin


def _block_diag(w):
    n, r, c = w.shape
    eye = jnp.eye(n, dtype=w.dtype)
    return (w[:, :, None, :] * eye[:, None, :, None]).reshape(n * r, n * c)


def _lane_pad(v, width=LANES):
    return jnp.pad(v, [(0, 0)] * (v.ndim - 1) + [(0, width - v.shape[-1])])


def _slot_tables(dest, counts):
    n = dest.shape[0]
    a = n * TOP_K
    n_blocks = -(-a // EXPERT_BLOCK) + N_EXPERTS
    pblocks = (counts + EXPERT_BLOCK - 1) // EXPERT_BLOCK
    pend_b = jnp.cumsum(pblocks)
    start = jnp.cumsum(counts) - counts
    blk = jnp.arange(n_blocks, dtype=jnp.int32)
    block_e = jnp.minimum(jnp.sum((blk[:, None] >= pend_b[None, :]).astype(jnp.int32), axis=1), N_EXPERTS - 1)
    n_used = pend_b[N_EXPERTS - 1:].astype(jnp.int32)
    tok = jnp.arange(a, dtype=jnp.int32) // TOP_K
    _, sorted_tok = lax.sort_key_val(dest.reshape(a), tok)
    slots = n_blocks * EXPERT_BLOCK
    max_pad = N_EXPERTS * EXPERT_BLOCK
    padded = jnp.concatenate([jnp.zeros((max_pad,), jnp.int32), sorted_tok, jnp.zeros((slots - a,), jnp.int32)])
    shift = (pend_b - pblocks) * EXPERT_BLOCK - start
    onehot = block_e[:, None] == jnp.arange(N_EXPERTS)[None, :]
    first_blk = jnp.sum(jnp.where(onehot, (pend_b - pblocks)[None, :], 0), axis=1)
    n_valid = jnp.sum(jnp.where(onehot, counts[None, :], 0), axis=1) - (blk - first_blk) * EXPERT_BLOCK
    valid = (blk < n_used[0])[:, None] & (jnp.arange(EXPERT_BLOCK)[None, :] < n_valid[:, None])
    slot_tok = jnp.zeros((n_blocks, EXPERT_BLOCK), jnp.int32)
    for e in range(N_EXPERTS):
        moved = lax.dynamic_slice(padded, (max_pad - shift[e],), (slots,)).reshape(n_blocks, EXPERT_BLOCK)
        slot_tok = jnp.where(valid & (block_e == e)[:, None], moved, slot_tok)
    return block_e, n_used, slot_tok.reshape(-1)


EXPERT_PIECES = 4


def kernel(x, c, ctx, c_ctx, w_ada, b_ada, w_in, rnn_conv_w, rnn_conv_b, rnn_wa, rnn_ba, rnn_wx, rnn_bx, rnn_lam, att_lambda, att_subln, pool_w, pool_b, pool_scale, ssm_conv_w, ssm_conv_b, ssm_dt_bias, ssm_a_log, ssm_d, ssm_norm, w_branch, w_out, ln1_g, ln1_b, ln2_g, ln2_b, router_w, router_b, w_gate_up, b_gate_up, w_down, b_down):
    bsz, n_lat, d = x.shape
    n_ctx = ctx.shape[1]
    total = n_ctx + n_lat
    assert n_ctx == ROW_TILE and n_lat % ROW_TILE == 0 and n_lat % GRID_W == 0
    ctx_tiles = n_ctx // ROW_TILE
    cos_t, sin_t = _rope_tables(n_lat, n_ctx)
    pcols = _proj_columns()
    w_main = jnp.where(pcols[None, None, :] >= 0, jnp.take(w_in, jnp.maximum(pcols, 0), axis=2), 0.0).astype(BF16)
    w_gates = w_in[:, :, IN_OFFS[9]:].astype(BF16)
    w_branch_b = w_branch.astype(BF16)
    w_out_b = w_out.astype(BF16)
    w_router = _lane_pad(router_w).astype(BF16)
    b_router = _lane_pad(router_b)[:, None, :]
    n_cg = RNN_W // LANES
    bpg = LANES // RNN_BLOCK
    gate_blocks = jnp.stack([rnn_wa[:, 0], rnn_wx[:, 0], rnn_wa[:, 1], rnn_wx[:, 1]], axis=1)
    gate_blocks = gate_blocks.reshape(DEPTH, 4, n_cg, bpg, RNN_BLOCK, RNN_BLOCK)
    rnn_wg = jax.vmap(jax.vmap(jax.vmap(_block_diag)))(gate_blocks)
    rnn_wg = rnn_wg.transpose(0, 2, 3, 1, 4).reshape(DEPTH, n_cg, LANES, 4 * LANES).astype(BF16)
    gate_bias = jnp.stack([rnn_ba[:, 0], rnn_bx[:, 0], rnn_ba[:, 1], rnn_bx[:, 1]], axis=1)
    rnn_bg = gate_bias.reshape(DEPTH, 4, n_cg, LANES).transpose(0, 2, 1, 3).reshape(DEPTH, n_cg, 1, 4 * LANES)
    rnn_sp = jax.nn.softplus(-rnn_lam).reshape(DEPTH, 2, n_cg, LANES).transpose(0, 2, 1, 3).reshape(DEPTH, n_cg, 1, 2 * LANES)
    ccols = _ssd_conv_columns()
    ssd_cw = jnp.take(ssm_conv_w, ccols, axis=2).reshape(DEPTH, CONV_W, SSM_G, SSD_XBC).transpose(0, 2, 1, 3)
    ssd_cb = jnp.take(ssm_conv_b, ccols, axis=1).reshape(DEPTH, SSM_G, 1, SSD_XBC)
    per_group = lambda v: v.reshape(DEPTH, 2, SSM_G, SSM_K).transpose(0, 2, 1, 3).reshape(DEPTH, SSM_G, 1, 2 * SSM_K)
    ssd_dtb = _lane_pad(per_group(ssm_dt_bias))
    ssd_alog = _lane_pad(per_group(ssm_a_log))
    ssd_dsk = jnp.repeat((ssm_d[:, 0] + ssm_d[:, 1]).reshape(DEPTH, SSM_G, 1, SSM_K), SSM_P, axis=-1)
    ssd_nw = ssm_norm.reshape(DEPTH, SSM_G, 1, SSM_GW)
    pool_wb = pool_w.astype(BF16)

    xs = jnp.concatenate([ctx, x], axis=1)
    out = None
    for li in range(DEPTH):
        need_ctx = li < DEPTH - 1
        first = 0 if need_ctx else ctx_tiles
        mod_l = jax.nn.silu(c) @ w_ada[li] + b_ada[li]
        mod_c = jnp.broadcast_to(jax.nn.silu(c_ctx) @ w_ada[li] + b_ada[li], mod_l.shape)
        mods = jnp.stack([mod_c, mod_l], axis=1).reshape(bsz, 2, 6, d)

        proj = in_proj(xs, mods, w_main[li], n_ctx, 768 if total % 768 == 0 else ROW_TILE, BRANCH_W * 3 if PROJ_W % (BRANCH_W * 3) == 0 else BRANCH_W)
        y_rnn = rglru(proj, rnn_conv_w[li], rnn_conv_b[li][None], rnn_wg[li], rnn_bg[li], rnn_sp[li], n_ctx)
        lam_init = 0.8 - 0.6 * math.exp(-0.3 * li)
        lv = att_lambda[li]
        lam = jnp.exp(jnp.sum(lv[0] * lv[1])) - jnp.exp(jnp.sum(lv[2] * lv[3])) + lam_init
        y_att = diff_attention(proj, cos_t, sin_t, att_subln[li], lam, jnp.asarray(1.0 - lam_init, F32), n_ctx, need_ctx)
        y_pool = pool(proj, pool_wb[li], pool_b[li][None], pool_scale[li][None], n_ctx)
        y_ssd = ssd(proj, ssd_cw[li], ssd_cb[li], ssd_dtb[li], ssd_alog[li], ssd_dsk[li], ssd_nw[li], n_ctx)
        ln1 = jnp.stack([ln1_g[li], ln1_b[li]])
        x1, h2, logits = merge(xs, mods, (y_rnn, y_att, y_pool, y_ssd), w_gates[li], w_branch_b[li], w_out_b[li], ln1,
                               w_router[li], b_router[li], first)

        n_rout = total - first * ROW_TILE
        table, counts = route(logits[:, first * ROW_TILE:].reshape(bsz * n_rout, LANES))
        dest = table[:, ROUTE_DEST:ROUTE_DEST + TOP_K].astype(jnp.int32)
        block_e, n_used, slot_tok = _slot_tables(dest, counts[0, :N_EXPERTS].astype(jnp.int32))
        slot_row = slot_tok + (slot_tok // n_rout + 1) * (first * ROW_TILE)
        h2_rows = h2.reshape(bsz * total, d)
        n_blocks = block_e.shape[0]
        pieces = EXPERT_PIECES if n_blocks % EXPERT_PIECES == 0 else 1
        piece_rows = n_blocks // pieces * EXPERT_BLOCK
        yb = None
        for p in range(pieces):
            xb = h2_rows[slot_row[p * piece_rows:(p + 1) * piece_rows]]
            yb = expert_blocks(li, block_e, n_used, xb, yb, p * (n_blocks // pieces), n_blocks,
                               w_gate_up, b_gate_up, w_down, b_down)
        ys = [yb[dest[:, k]].reshape(bsz, n_rout, d) for k in range(TOP_K)]
        ln2 = jnp.stack([ln2_g[li], ln2_b[li]])
        out = combine_norm(x1, ys, table.reshape(bsz, n_rout, LANES), mods, ln2, first)
        xs = out
    return out
```

```python
import functools
import math

import jax
import jax.numpy as jnp
import numpy as np
from jax import lax
from jax.experimental import pallas as pl
from jax.experimental.pallas import tpu as pltpu

D_MODEL = 1024
DEPTH = 4
GRID_W = 64
N_BRANCH = 4
BRANCH_W = D_MODEL // 2
RNN_W = BRANCH_W
RNN_BLOCK = 64
RG_C = 8.0
CONV_W = 4
ATT_DIM = 64
ATT_HEADS = BRANCH_W // (2 * ATT_DIM)
ATT_V_W = ATT_HEADS * 2 * ATT_DIM
ROPE_F = ATT_DIM // 4
ROPE_BASE = 10000.0
POOL_W = BRANCH_W
POOL_WINDOWS = (2, 4, 8, 16)
POOL_GROUPS = len(POOL_WINDOWS)
POOL_GW = POOL_W // POOL_GROUPS
SSM_W = BRANCH_W
SSM_P = 64
SSM_HEADS = SSM_W // SSM_P
SSM_G = 2
SSM_K = SSM_HEADS // SSM_G
SSM_N = 64
SSM_GW = SSM_W // SSM_G
SSM_CHUNK = 128
N_EXPERTS = 32
TOP_K = 4
D_EXPERT = D_MODEL
SWIGLU_LIMIT = 7.0
SWIGLU_ALPHA = 1.702
EXPERT_BLOCK = 256
DN_ALPHA = (2 * DEPTH) ** 0.25
IN_WIDTHS = (RNN_W, RNN_W, BRANCH_W, BRANCH_W, ATT_V_W, POOL_W, SSM_W, SSM_W + 2 * SSM_G * SSM_N, 2 * SSM_HEADS, N_BRANCH * D_MODEL)
IN_OFFS = tuple(int(v) for v in np.cumsum((0,) + IN_WIDTHS))

LANES = 128
SUBLANES = 8
VMEM_LIMIT = 48 * 1024 * 1024
ROW_TILE = 256
EXPERT_VMEM_LIMIT = 54 * 1024 * 1024
MERGE_VMEM_LIMIT = 56 * 1024 * 1024

COL_RX, COL_RG, COL_Q, COL_K, COL_V, COL_PU, COL_Z, COL_SSD = 0, 1, 2, 3, 4, 5, 6, 7
PROJ_W = (COL_SSD + SSM_G) * BRANCH_W
SSD_XBC = SSM_GW + 2 * SSM_N

BF16 = jnp.bfloat16
F32 = jnp.float32


def _params(*sem):
    return pltpu.CompilerParams(dimension_semantics=sem, vmem_limit_bytes=VMEM_LIMIT)


def _normalize(x, eps):
    mu = jnp.mean(x, -1, keepdims=True)
    xc = x - mu
    var = jnp.mean(xc * xc, -1, keepdims=True)
    return xc * lax.rsqrt(var + eps)


def _segment_valid(t, off, n_ctx, total):
    lo = jnp.where(t < n_ctx, 0, n_ctx)
    hi = jnp.where(t < n_ctx, n_ctx, total)
    return (t + off >= lo) & (t + off < hi)


def _shift_rows(u, off, t, n_ctx):
    total = u.shape[0]
    rolled = pltpu.roll(u, (-off) % total, 0)
    return jnp.where(_segment_valid(t, off, n_ctx, total), rolled, 0.0)


def _centred_conv(u, pad_ref, w_ref, b_ref, n_ctx):
    total, c = u.shape
    gap = SUBLANES
    zeros = jnp.zeros((gap, c), F32)
    pad_ref[0:gap, :] = zeros
    pad_ref[gap:gap + n_ctx, :] = u[0:n_ctx]
    pad_ref[gap + n_ctx:2 * gap + n_ctx, :] = zeros
    pad_ref[2 * gap + n_ctx:2 * gap + total, :] = u[n_ctx:total]
    pad_ref[2 * gap + total:3 * gap + total, :] = zeros

    def tap(off):
        return jnp.concatenate([pad_ref[pl.ds(gap + off, n_ctx), :],
                                pad_ref[pl.ds(2 * gap + n_ctx + off, total - n_ctx), :]], axis=0)

    left = CONV_W // 2
    out = b_ref[...] + tap(-left) * w_ref[0:1, :]
    for k in range(1, CONV_W):
        out = out + (u if k == left else tap(k - left)) * w_ref[k:k + 1, :]
    return out


def _in_proj_kernel(x_ref, mod_ref, w_ref, o_ref, h_ref, *, n_ctx):
    @pl.when(pl.program_id(2) == 0)
    def _():
        tm = x_ref.shape[1]
        row = pl.program_id(1) * tm + lax.broadcasted_iota(jnp.int32, (tm, 1), 0)
        is_ctx = row < n_ctx
        shift = jnp.where(is_ctx, mod_ref[0, 0, 0:1, :], mod_ref[0, 1, 0:1, :])
        scale = jnp.where(is_ctx, mod_ref[0, 0, 1:2, :], mod_ref[0, 1, 1:2, :])
        h_ref[...] = (_normalize(x_ref[0], 1e-6) * (1.0 + scale) + shift).astype(BF16)

    o_ref[0] = jnp.dot(h_ref[...], w_ref[...], preferred_element_type=F32)


def in_proj(x, mods, w, n_ctx, tm, tn):
    b, t, d = x.shape
    n = w.shape[1]
    return pl.pallas_call(
        functools.partial(_in_proj_kernel, n_ctx=n_ctx),
        out_shape=jax.ShapeDtypeStruct((b, t, n), F32),
        grid=(b, t // tm, n // tn),
        in_specs=[
            pl.BlockSpec((1, tm, d), lambda bi, i, j: (bi, i, 0)),
            pl.BlockSpec((1, 2, 6, d), lambda bi, i, j: (bi, 0, 0, 0)),
            pl.BlockSpec((d, tn), lambda bi, i, j: (0, j)),
        ],
        out_specs=pl.BlockSpec((1, tm, tn), lambda bi, i, j: (bi, i, j)),
        scratch_shapes=[pltpu.VMEM((tm, d), BF16)],
        compiler_params=_params("parallel", "parallel", "arbitrary"),
        name="in_proj",
    )(x, mods, w)


SCAN_ROWS = 64


def _tile_scan(a, b, reverse):
    rows = a.shape[0]
    sub = lax.broadcasted_iota(jnp.int32, (rows, 1), 0) % SUBLANES
    s = 1
    while s < SUBLANES:
        shift = (rows - s) if reverse else s
        keep = (sub + s < SUBLANES) if reverse else (sub >= s)
        a_sh = pltpu.roll(a, shift, 0)
        b_sh = pltpu.roll(b, shift, 0)
        b = jnp.where(keep, a * b_sh + b, b)
        a = jnp.where(keep, a * a_sh, a)
        s *= 2
    return a, b


def _rglru_kernel(rx_ref, rg_ref, cw_ref, cb_ref, wg_ref, bg_ref, sp_ref, o_ref, a_ref, b_ref, y_ref, pad_ref, *, n_ctx):
    total = rx_ref.shape[1]
    cw = rx_ref.shape[2]
    u = _centred_conv(rx_ref[0], pad_ref, cw_ref, cb_ref, n_ctx)
    g = jnp.dot(u.astype(BF16), wg_ref[0], preferred_element_type=F32) + bg_ref[0]
    n_tiles = SCAN_ROWS // SUBLANES

    for d in range(2):
        r = jax.nn.sigmoid(g[:, (2 * d) * cw:(2 * d + 1) * cw])
        i = jax.nn.sigmoid(g[:, (2 * d + 1) * cw:(2 * d + 2) * cw])
        log_a = -RG_C * r * sp_ref[0, :, d * cw:(d + 1) * cw]
        a = jnp.exp(log_a)
        a_ref[d] = a
        b_ref[d] = jnp.sqrt((1.0 - a) * (1.0 + a)) * (i * u)

    def scan_rows(r0, h, d):
        reverse = d == 1
        a, b = _tile_scan(a_ref[d, pl.ds(r0, SCAN_ROWS), :], b_ref[d, pl.ds(r0, SCAN_ROWS), :], reverse)
        outs = [None] * n_tiles
        for j in (range(n_tiles - 1, -1, -1) if reverse else range(n_tiles)):
            sl = slice(j * SUBLANES, (j + 1) * SUBLANES)
            hj = b[sl] + a[sl] * h
            h = hj[0:1] if reverse else hj[SUBLANES - 1:SUBLANES]
            outs[j] = hj
        y_ref[d, pl.ds(r0, SCAN_ROWS), :] = jnp.concatenate(outs, axis=0)
        return h

    def step(it, hs, lo, n_steps):
        h_f = scan_rows(pl.multiple_of(lo + it * SCAN_ROWS, SCAN_ROWS), hs[0], 0)
        h_b = scan_rows(pl.multiple_of(lo + (n_steps - 1 - it) * SCAN_ROWS, SCAN_ROWS), hs[1], 1)
        return h_f, h_b

    h0 = jnp.zeros((1, cw), F32)
    hs = lax.fori_loop(0, n_ctx // SCAN_ROWS, functools.partial(step, lo=0, n_steps=n_ctx // SCAN_ROWS), (h0, h0))
    n_lat_steps = (total - n_ctx) // SCAN_ROWS
    lax.fori_loop(0, n_lat_steps, functools.partial(step, lo=n_ctx, n_steps=n_lat_steps), hs)

    o_ref[0] = (jax.nn.gelu(rg_ref[0]) * (y_ref[0] + y_ref[1])).astype(BF16)


def rglru(proj, conv_w, conv_b, wg, bg, sp, n_ctx):
    b, t, _ = proj.shape
    cw = LANES
    n_cg = RNN_W // cw
    per_block = BRANCH_W // cw
    return pl.pallas_call(
        functools.partial(_rglru_kernel, n_ctx=n_ctx),
        out_shape=jax.ShapeDtypeStruct((b, t, RNN_W), BF16),
        grid=(b, n_cg),
        in_specs=[
            pl.BlockSpec((1, t, cw), lambda bi, c: (bi, 0, COL_RX * per_block + c)),
            pl.BlockSpec((1, t, cw), lambda bi, c: (bi, 0, COL_RG * per_block + c)),
            pl.BlockSpec((CONV_W, cw), lambda bi, c: (0, c)),
            pl.BlockSpec((1, cw), lambda bi, c: (0, c)),
            pl.BlockSpec((1, cw, 4 * cw), lambda bi, c: (c, 0, 0)),
            pl.BlockSpec((1, 1, 4 * cw), lambda bi, c: (c, 0, 0)),
            pl.BlockSpec((1, 1, 2 * cw), lambda bi, c: (c, 0, 0)),
        ],
        out_specs=pl.BlockSpec((1, t, cw), lambda bi, c: (bi, 0, c)),
        scratch_shapes=[pltpu.VMEM((2, t, cw), F32), pltpu.VMEM((2, t, cw), F32), pltpu.VMEM((2, t, cw), F32),
                        pltpu.VMEM((t + 3 * SUBLANES, cw), F32)],
        compiler_params=_params("parallel", "parallel"),
        name="rglru",
    )(proj, proj, conv_w, conv_b, wg, bg, sp)


def _pool_kernel(u_ref, w_ref, b_ref, s_ref, o_ref, *, n_ctx):
    total = u_ref.shape[1]
    t = lax.broadcasted_iota(jnp.int32, (total, 1), 0)
    for gi, win in enumerate(POOL_WINDOWS):
        cols = slice(gi * POOL_GW, (gi + 1) * POOL_GW)
        u = u_ref[0, :, cols]
        acc = u
        cnt = jnp.ones((total, 1), F32)
        for off in range(-(win // 2), win - win // 2):
            if off == 0:
                continue
            acc = acc + _shift_rows(u, off, t, n_ctx)
            cnt = cnt + _segment_valid(t, off, n_ctx, total).astype(F32)
        mix = acc / cnt - u
        y = jnp.dot(mix.astype(BF16), w_ref[gi], preferred_element_type=F32) + b_ref[:, cols]
        o_ref[0, :, cols] = (y * s_ref[:, cols]).astype(BF16)


def pool(proj, w, bias, scale, n_ctx):
    b, t, _ = proj.shape
    return pl.pallas_call(
        functools.partial(_pool_kernel, n_ctx=n_ctx),
        out_shape=jax.ShapeDtypeStruct((b, t, POOL_W), BF16),
        grid=(b,),
        in_specs=[
            pl.BlockSpec((1, t, POOL_W), lambda bi: (bi, 0, COL_PU)),
            pl.BlockSpec((POOL_GROUPS, POOL_GW, POOL_GW), lambda bi: (0, 0, 0)),
            pl.BlockSpec((1, POOL_W), lambda bi: (0, 0)),
            pl.BlockSpec((1, POOL_W), lambda bi: (0, 0)),
        ],
        out_specs=pl.BlockSpec((1, t, POOL_W), lambda bi: (bi, 0, 0)),
        compiler_params=_params("parallel"),
        name="pool",
    )(proj, w, bias, scale)


def _bf16_pieces(v):
    hi = v.astype(BF16)
    r1 = v - hi.astype(F32)
    mid = r1.astype(BF16)
    lo = (r1 - mid.astype(F32)).astype(BF16)
    return hi, mid, lo


def _split_dot(tri, v):
    return sum(jnp.dot(tri, p, preferred_element_type=F32) for p in _bf16_pieces(v))


def _split_dot_lhs(v, tri):
    return sum(jnp.dot(p, tri, preferred_element_type=F32) for p in _bf16_pieces(v))


def _per_head(cols, width):
    head = lax.broadcasted_iota(jnp.int32, (1, width), 1) // SSM_P
    out = cols[SSM_K - 1]
    for k in range(SSM_K - 2, -1, -1):
        out = jnp.where(head == k, cols[k], out)
    return out


def _ssd_kernel(blk_ref, z_ref, cw_ref, cb_ref, dtb_ref, alog_ref, dsk_ref, nw_ref, o_ref,
                u_ref, dt_ref, adt_ref, tr_ref, y_ref, st_ref, pad_ref, csc_ref, csr_ref, *, n_ctx):
    total = blk_ref.shape[1]
    ck = SSM_CHUNK
    n_chunks = total // ck
    ctx_chunks = n_ctx // ck
    u_ref[...] = jax.nn.silu(_centred_conv(blk_ref[0, :, 0:SSD_XBC], pad_ref, cw_ref.at[0], cb_ref.at[0], n_ctx))
    dt = jax.nn.softplus(blk_ref[0, :, SSD_XBC:SSD_XBC + LANES] + dtb_ref[0])
    dt_ref[...] = dt
    adt_ref[...] = dt * (-jnp.exp(alog_ref[0]))
    for c in range(n_chunks):
        rows = slice(c * ck, (c + 1) * ck)
        tr_ref[c, 0:ck, :] = u_ref[rows, SSM_GW:SSM_GW + 2 * SSM_N].T
        tr_ref[c, ck:2 * ck, :] = adt_ref[rows, :].T

    ri = lax.broadcasted_iota(jnp.int32, (ck, ck), 0)
    ci = lax.broadcasted_iota(jnp.int32, (ck, ck), 1)
    lower = (ci <= ri)
    lower_b = lower.astype(BF16)
    upper_b = (ci >= ri).astype(BF16)
    lane = lax.broadcasted_iota(jnp.int32, (1, LANES), 1)
    head_w = lax.broadcasted_iota(jnp.int32, (1, SSM_GW), 1) // SSM_P
    srow = lax.broadcasted_iota(jnp.int32, (ck, 1), 0)

    a_cat = jnp.concatenate([adt_ref[c * ck:(c + 1) * ck, :] for c in range(n_chunks)], axis=1)
    a_rows = jnp.concatenate([tr_ref[c, ck:ck + SUBLANES, :] for c in range(n_chunks)], axis=0)
    for d, (tri_col, tri_row) in enumerate(((lower_b, upper_b), (upper_b, lower_b))):
        cols_all = _split_dot(tri_col, a_cat)
        for c in range(n_chunks):
            csc_ref[d, c] = cols_all[:, c * ck:(c + 1) * ck]
        csr_ref[d] = _split_dot_lhs(a_rows, tri_row)

    def chunk(c, d):
        reverse = d == 1
        causal = (ci >= ri) if reverse else lower
        r0 = pl.multiple_of(c * ck, ck)
        xs = u_ref[pl.ds(r0, ck), 0:SSM_GW]
        bc = u_ref[pl.ds(r0, ck), SSM_GW:SSM_GW + 2 * SSM_N]
        tr = tr_ref[c, 0:ck, :]
        c_lo = jnp.where(lane < SSM_N, pltpu.roll(bc, SSM_N, 1), 0.0).astype(BF16)
        cb = jnp.dot(c_lo, tr.astype(BF16), preferred_element_type=F32)
        cs_col = csc_ref[d, c]
        cs_row = csr_ref[d, pl.ds(pl.multiple_of(c * SUBLANES, SUBLANES), SUBLANES), :]
        edge = ck - 1 if not reverse else 0
        dtc = dt_ref[pl.ds(r0, ck), :]
        cols, tots, dts = [], [], []
        y = jnp.zeros((ck, SSM_GW), F32)
        for k in range(SSM_K):
            j = d * SSM_K + k
            cols.append(cs_col[:, j:j + 1])
            tots.append(cs_col[edge:edge + 1, j:j + 1])
            dts.append(dtc[:, j:j + 1])
        xdt = xs * _per_head(dts, SSM_GW)
        for k in range(SSM_K):
            j = d * SSM_K + k
            seg = cols[k] - cs_row[j:j + 1, :]
            m = (cb * jnp.where(causal, jnp.exp(seg), 0.0)).astype(BF16)
            y = y + jnp.dot(m, jnp.where(head_w == k, xdt, 0.0).astype(BF16), preferred_element_type=F32)
        col_w = _per_head(cols, SSM_GW)
        tot_w = _per_head(tots, SSM_GW)
        state = st_ref[d]
        y = y + jnp.dot(c_lo, state.astype(BF16), preferred_element_type=F32) * jnp.exp(col_w)
        xd = (xdt * jnp.exp(tot_w - col_w)).astype(BF16)
        new = jnp.exp(tot_w) * state + jnp.dot(tr.astype(BF16), xd, preferred_element_type=F32)
        st_ref[d] = jnp.where(srow < SSM_N, new, 0.0)
        y_ref[d, pl.ds(r0, ck), :] = y

    def both(it, carry, lo, n_steps):
        chunk(lo + it, 0)
        chunk(lo + n_steps - 1 - it, 1)
        return carry

    st_ref[...] = jnp.zeros_like(st_ref)
    lax.fori_loop(0, ctx_chunks, functools.partial(both, lo=0, n_steps=ctx_chunks), 0)
    lat_chunks = n_chunks - ctx_chunks
    lax.fori_loop(0, lat_chunks, functools.partial(both, lo=ctx_chunks, n_steps=lat_chunks), 0,
                  unroll=2 if lat_chunks % 2 == 0 else 1)

    y = y_ref[0] + y_ref[1] + dsk_ref[0] * u_ref[:, 0:SSM_GW]
    g = y * jax.nn.silu(z_ref[0])
    g = g * lax.rsqrt(jnp.mean(g * g, -1, keepdims=True) + 1e-5)
    o_ref[0] = (g * nw_ref[0]).astype(BF16)


def ssd(proj, conv_w, conv_b, dt_bias, a_log, d_skip, norm_w, n_ctx):
    b, t, _ = proj.shape
    n_chunks = t // SSM_CHUNK
    z_per = BRANCH_W // SSM_GW
    return pl.pallas_call(
        functools.partial(_ssd_kernel, n_ctx=n_ctx),
        out_shape=jax.ShapeDtypeStruct((b, t, SSM_W), BF16),
        grid=(b, SSM_G),
        in_specs=[
            pl.BlockSpec((1, t, BRANCH_W), lambda bi, g: (bi, 0, COL_SSD + g)),
            pl.BlockSpec((1, t, SSM_GW), lambda bi, g: (bi, 0, COL_Z * z_per + g)),
            pl.BlockSpec((1, CONV_W, SSD_XBC), lambda bi, g: (g, 0, 0)),
            pl.BlockSpec((1, 1, SSD_XBC), lambda bi, g: (g, 0, 0)),
            pl.BlockSpec((1, 1, LANES), lambda bi, g: (g, 0, 0)),
            pl.BlockSpec((1, 1, LANES), lambda bi, g: (g, 0, 0)),
            pl.BlockSpec((1, 1, SSM_GW), lambda bi, g: (g, 0, 0)),
            pl.BlockSpec((1, 1, SSM_GW), lambda bi, g: (g, 0, 0)),
        ],
        out_specs=pl.BlockSpec((1, t, SSM_GW), lambda bi, g: (bi, 0, g)),
        scratch_shapes=[
            pltpu.VMEM((t, SSD_XBC), F32),
            pltpu.VMEM((t, LANES), F32),
            pltpu.VMEM((t, LANES), F32),
            pltpu.VMEM((n_chunks, 2 * SSM_CHUNK, SSM_CHUNK), F32),
            pltpu.VMEM((2, t, SSM_GW), F32),
            pltpu.VMEM((2, SSM_CHUNK, SSM_GW), F32),
            pltpu.VMEM((t + 3 * SUBLANES, SSD_XBC), F32),
            pltpu.VMEM((2, n_chunks, SSM_CHUNK, LANES), F32),
            pltpu.VMEM((2, n_chunks * SUBLANES, SSM_CHUNK), F32),
        ],
        compiler_params=_params("parallel", "parallel"),
        name="ssd",
    )(proj, proj, conv_w, conv_b, dt_bias, a_log, d_skip, norm_w)


def _rope(x, cos, sin):
    lane = lax.broadcasted_iota(jnp.int32, (1, x.shape[1]), 1)
    partner = jnp.where(lane % (2 * ROPE_F) < ROPE_F, pltpu.roll(x, x.shape[1] - ROPE_F, 1), pltpu.roll(x, ROPE_F, 1))
    return x * cos + partner * sin


Q_TILES = (1024, 512, 256)


def _diff_attn_kernel(par_ref, q_ref, k_ref, v_ref, cos_ref, sin_ref, g_ref, o_ref, kb_ref, vb_ref,
                      *, n_ctx, need_ctx, q_tile):
    lam = par_ref[0]
    out_scale = par_ref[1]
    total = k_ref.shape[1]
    kb_ref[...] = _rope(k_ref[0], cos_ref[...], sin_ref[...]).astype(BF16)
    vb_ref[...] = v_ref[0].astype(BF16)
    dims = (((1,), (1,)), ((), ()))

    def attend(r0, rows, n_keys):
        q = _rope(q_ref[0, pl.ds(r0, rows), :], cos_ref[pl.ds(r0, rows), :], sin_ref[pl.ds(r0, rows), :])
        q = (q * (ATT_DIM ** -0.5 * math.log2(math.e))).astype(BF16)
        lane = lax.broadcasted_iota(jnp.int32, q.shape, 1)
        k = kb_ref[0:n_keys, :]
        v = vb_ref[0:n_keys, :]

        def softmax_v(qh):
            s = lax.dot_general(qh, k, dims, preferred_element_type=F32)
            e = jnp.exp2(s - jnp.max(s, -1, keepdims=True))
            return jnp.dot(e.astype(BF16), v, preferred_element_type=F32) / jnp.sum(e, -1, keepdims=True)

        o = (softmax_v(jnp.where(lane < ATT_DIM, q, jnp.zeros_like(q)))
             - lam * softmax_v(jnp.where(lane >= ATT_DIM, q, jnp.zeros_like(q))))
        o = o * lax.rsqrt(jnp.mean(o * o, -1, keepdims=True) + 1e-5) * g_ref[...] * out_scale
        o_ref[0, pl.ds(r0, rows), :] = o.astype(BF16)

    if need_ctx:
        attend(0, n_ctx, n_ctx)

    def latent_tile(i, carry):
        attend(pl.multiple_of(n_ctx + i * q_tile, ROW_TILE), q_tile, total)
        return carry

    lax.fori_loop(0, (total - n_ctx) // q_tile, latent_tile, 0)


def diff_attention(proj, cos_t, sin_t, subln, lam, out_scale, n_ctx, need_ctx):
    b, t, _ = proj.shape
    hw = 2 * ATT_DIM
    per_block = BRANCH_W // hw
    q_tile = next(q for q in Q_TILES if (t - n_ctx) % q == 0)
    par = jnp.stack([lam, out_scale]).astype(F32)
    col = lambda block: (lambda bi, h: (bi, 0, block * per_block + h))
    whole = lambda bi, h: (0, 0)
    return pl.pallas_call(
        functools.partial(_diff_attn_kernel, n_ctx=n_ctx, need_ctx=need_ctx, q_tile=q_tile),
        out_shape=jax.ShapeDtypeStruct((b, t, ATT_V_W), BF16),
        grid=(b, ATT_HEADS),
        in_specs=[
            pl.BlockSpec(memory_space=pltpu.SMEM),
            pl.BlockSpec((1, t, hw), col(COL_Q)),
            pl.BlockSpec((1, t, hw), col(COL_K)),
            pl.BlockSpec((1, t, hw), col(COL_V)),
            pl.BlockSpec((t, hw), whole),
            pl.BlockSpec((t, hw), whole),
            pl.BlockSpec((1, hw), whole),
        ],
        out_specs=pl.BlockSpec((1, t, hw), lambda bi, h: (bi, 0, h)),
        scratch_shapes=[pltpu.VMEM((t, hw), BF16), pltpu.VMEM((t, hw), BF16)],
        compiler_params=_params("parallel", "parallel"),
        name="diff_attention",
    )(par, proj, proj, proj, cos_t, sin_t, subln.reshape(1, hw).astype(F32))


def _merge_kernel(x_ref, mod_ref, y0_ref, y1_ref, y2_ref, y3_ref, wg_ref, wb_ref, wo_ref, ln_ref, wr_ref, br_ref,
                  x1_ref, h2_ref, lg_ref):
    x = x_ref[0]
    mod = mod_ref[0, 0]
    h = (_normalize(x, 1e-6) * (1.0 + mod[1:2]) + mod[0:1]).astype(BF16)
    m = None
    for i, y_ref in enumerate((y0_ref, y1_ref, y2_ref, y3_ref)):
        gate = jax.nn.sigmoid(jnp.dot(h, wg_ref[:, i * D_MODEL:(i + 1) * D_MODEL], preferred_element_type=F32))
        term = gate * jnp.dot(y_ref[0], wb_ref[i], preferred_element_type=F32)
        m = term if m is None else m + term
    y = jnp.dot(m.astype(BF16), wo_ref[...], preferred_element_type=F32)
    x1 = _normalize(DN_ALPHA * x + mod[2:3] * y, 1e-5) * ln_ref[0:1] + ln_ref[1:2]
    x1_ref[0] = x1
    h2 = (_normalize(x1, 1e-6) * (1.0 + mod[4:5]) + mod[3:4]).astype(BF16)
    h2_ref[0] = h2
    lg_ref[0] = jnp.dot(h2, wr_ref[...], preferred_element_type=F32) + br_ref[...]


def merge(x, mods, ys, wg, wb, wo, ln, wr, br, first_tile):
    b, t, d = x.shape
    tm = ROW_TILE
    row = lambda bi, i: (bi, i + first_tile, 0)
    const2 = lambda bi, i: (0, 0)
    y_spec = pl.BlockSpec((1, tm, BRANCH_W), row)
    return pl.pallas_call(
        _merge_kernel,
        out_shape=(jax.ShapeDtypeStruct((b, t, d), F32), jax.ShapeDtypeStruct((b, t, d), BF16),
                   jax.ShapeDtypeStruct((b, t, LANES), F32)),
        grid=(b, t // tm - first_tile),
        in_specs=[
            pl.BlockSpec((1, tm, d), row),
            pl.BlockSpec((1, 1, 6, d), lambda bi, i: (bi, jnp.minimum(i + first_tile, 1), 0, 0)),
            y_spec, y_spec, y_spec, y_spec,
            pl.BlockSpec((d, N_BRANCH * d), const2),
            pl.BlockSpec((N_BRANCH, BRANCH_W, d), lambda bi, i: (0, 0, 0)),
            pl.BlockSpec((d, d), const2),
            pl.BlockSpec((2, d), const2),
            pl.BlockSpec((d, LANES), const2),
            pl.BlockSpec((1, LANES), const2),
        ],
        out_specs=(pl.BlockSpec((1, tm, d), row), pl.BlockSpec((1, tm, d), row), pl.BlockSpec((1, tm, LANES), row)),
        compiler_params=pltpu.CompilerParams(dimension_semantics=("parallel", "parallel"),
                                             vmem_limit_bytes=MERGE_VMEM_LIMIT),
        name="merge",
    )(x, mods, *ys, wg, wb, wo, ln, wr, br)


def _expert_kernel(be_ref, nb_ref, x_ref, wgu_ref, bgu_ref, wd_ref, bd_ref, *rest, first_block):
    o_ref, wgu_b, wd_b = rest[-3:]
    i = pl.program_id(0)
    blk = first_block + i

    @pl.when(blk < nb_ref[0])
    def _():
        @pl.when((i == 0) | (be_ref[blk] != be_ref[jnp.maximum(blk - 1, 0)]))
        def _():
            wgu_b[...] = wgu_ref[0, 0].astype(BF16)
            wd_b[...] = wd_ref[0, 0].astype(BF16)

        gu = jnp.dot(x_ref[...], wgu_b[...], preferred_element_type=F32) + bgu_ref[0, 0]
        gate = jnp.minimum(gu[:, :D_EXPERT], SWIGLU_LIMIT)
        up = jnp.clip(gu[:, D_EXPERT:], -SWIGLU_LIMIT, SWIGLU_LIMIT)
        glu = gate * jax.nn.sigmoid(gate * SWIGLU_ALPHA)
        act = ((up + 1.0) * glu).astype(BF16)
        y = jnp.dot(act, wd_b[...], preferred_element_type=F32) + bd_ref[0, 0]
        o_ref[...] = y.astype(BF16)


def expert_blocks(layer, block_e, n_used, xb, yb_prev, first_block, n_blocks_total, wgu, bgu, wd, bd):
    rows, d = xb.shape
    n_piece = rows // EXPERT_BLOCK

    def local(i, nb):
        return jnp.maximum(jnp.minimum(i, nb[0] - 1 - first_block), 0)

    expert = lambda i, be, nb: (layer, be[first_block + local(i, nb)], 0, 0)
    in_specs = [
        pl.BlockSpec((EXPERT_BLOCK, d), lambda i, be, nb: (local(i, nb), 0)),
        pl.BlockSpec((1, 1, d, 2 * D_EXPERT), expert),
        pl.BlockSpec((1, 1, 1, 2 * D_EXPERT), expert),
        pl.BlockSpec((1, 1, D_EXPERT, d), expert),
        pl.BlockSpec((1, 1, 1, d), expert),
    ]
    args = [block_e, n_used, xb, wgu, bgu[:, :, None, :], wd, bd[:, :, None, :]]
    aliases = {}
    if yb_prev is not None:
        in_specs.append(pl.BlockSpec(memory_space=pl.ANY))
        aliases = {len(args): 0}
        args.append(yb_prev)
    grid_spec = pltpu.PrefetchScalarGridSpec(
        num_scalar_prefetch=2,
        grid=(n_piece,),
        in_specs=in_specs,
        out_specs=pl.BlockSpec((EXPERT_BLOCK, d), lambda i, be, nb: (first_block + local(i, nb), 0)),
        scratch_shapes=[pltpu.VMEM((d, 2 * D_EXPERT), BF16), pltpu.VMEM((D_EXPERT, d), BF16)],
    )
    return pl.pallas_call(
        functools.partial(_expert_kernel, first_block=first_block),
        out_shape=jax.ShapeDtypeStruct((n_blocks_total * EXPERT_BLOCK, d), BF16),
        grid_spec=grid_spec,
        input_output_aliases=aliases,
        compiler_params=pltpu.CompilerParams(dimension_semantics=("arbitrary",), vmem_limit_bytes=EXPERT_VMEM_LIMIT),
        name="expert_blocks",
    )(*args)


ROUTE_DEST, ROUTE_PROB = 0, TOP_K
ROUTE_TILES = (1024, 512, 256)


def _route_kernel(lg_ref, out_ref, cnt_ref, run_ref, base_ref):
    phase = pl.program_id(0)
    i = pl.program_id(1)
    tm = lg_ref.shape[0]
    lane = lax.broadcasted_iota(jnp.int32, (1, LANES), 1).astype(F32)
    cur = jnp.where(lane < N_EXPERTS, lg_ref[...], -jnp.inf)
    vals, picks = [], []
    member = jnp.zeros((tm, LANES), F32)
    for _ in range(TOP_K):
        v = jnp.max(cur, -1, keepdims=True)
        pick = lane == jnp.min(jnp.where(cur == v, lane, float(LANES)), -1, keepdims=True)
        member = member + pick.astype(F32)
        cur = jnp.where(pick, -jnp.inf, cur)
        vals.append(v)
        picks.append(pick)
    tile_counts = jnp.sum(member, 0, keepdims=True)

    @pl.when(phase == 0)
    def _():
        @pl.when(i == 0)
        def _():
            cnt_ref[...] = jnp.zeros_like(cnt_ref)
        cnt_ref[...] = cnt_ref[...] + tile_counts

    @pl.when(phase == 1)
    def _():
        @pl.when(i == 0)
        def _():
            blocks = jnp.floor((cnt_ref[...] + (EXPERT_BLOCK - 1)) * (1.0 / EXPERT_BLOCK))
            r = lax.broadcasted_iota(jnp.int32, (LANES, LANES), 0)
            c = lax.broadcasted_iota(jnp.int32, (LANES, LANES), 1)
            before = _split_dot_lhs(jnp.broadcast_to(blocks, (SUBLANES, LANES)), (r < c).astype(BF16))
            base_ref[...] = before[0:1] * float(EXPERT_BLOCK)
            run_ref[...] = jnp.zeros_like(run_ref)

        r = lax.broadcasted_iota(jnp.int32, (tm, tm), 0)
        c = lax.broadcasted_iota(jnp.int32, (tm, tm), 1)
        earlier = jnp.dot((c < r).astype(BF16), member.astype(BF16), preferred_element_type=F32)
        slot = base_ref[...] + run_ref[...] + earlier
        denom = 1.0
        exps = [1.0]
        for k in range(1, TOP_K):
            exps.append(jnp.exp(vals[k] - vals[0]))
            denom = denom + exps[k]
        out_lane = lax.broadcasted_iota(jnp.int32, (1, LANES), 1)
        row = jnp.zeros((tm, LANES), F32)
        for k in range(TOP_K):
            dest = jnp.sum(jnp.where(picks[k], slot, 0.0), -1, keepdims=True)
            row = jnp.where(out_lane == ROUTE_DEST + k, dest, row)
            row = jnp.where(out_lane == ROUTE_PROB + k, exps[k] / denom, row)
        out_ref[...] = row
        run_ref[...] = run_ref[...] + tile_counts


def route(logits):
    n = logits.shape[0]
    tm = next(t for t in ROUTE_TILES if n % t == 0)
    return pl.pallas_call(
        _route_kernel,
        out_shape=(jax.ShapeDtypeStruct((n, LANES), F32), jax.ShapeDtypeStruct((1, LANES), F32)),
        grid=(2, n // tm),
        in_specs=[pl.BlockSpec((tm, LANES), lambda ph, i: (i, 0))],
        out_specs=(pl.BlockSpec((tm, LANES), lambda ph, i: (i * ph, 0)), pl.BlockSpec((1, LANES), lambda ph, i: (0, 0))),
        scratch_shapes=[pltpu.VMEM((1, LANES), F32), pltpu.VMEM((1, LANES), F32)],
        compiler_params=_params("arbitrary", "arbitrary"),
        name="route",
    )(logits)


def _combine_norm_kernel(x_ref, y_ref, rt_ref, mod_ref, ln_ref, o_ref):
    d = x_ref.shape[2]
    f = None
    for k in range(TOP_K):
        term = rt_ref[0, :, ROUTE_PROB + k:ROUTE_PROB + k + 1] * y_ref[0, :, k * d:(k + 1) * d].astype(F32)
        f = term if f is None else f + term
    x2 = DN_ALPHA * x_ref[0] + mod_ref[0, 0, 5:6] * f
    o_ref[0] = _normalize(x2, 1e-5) * ln_ref[0:1] + ln_ref[1:2]


def combine_norm(x1, ys, table, mods, ln, first_tile):
    b, t, d = x1.shape
    tm = ROW_TILE
    n_tiles = t // tm - first_tile
    return pl.pallas_call(
        _combine_norm_kernel,
        out_shape=jax.ShapeDtypeStruct((b, n_tiles * tm, d), F32),
        grid=(b, n_tiles),
        in_specs=[
            pl.BlockSpec((1, tm, d), lambda bi, i: (bi, i + first_tile, 0)),
            pl.BlockSpec((1, tm, TOP_K * d), lambda bi, i: (bi, i, 0)),
            pl.BlockSpec((1, tm, LANES), lambda bi, i: (bi, i, 0)),
            pl.BlockSpec((1, 1, 6, d), lambda bi, i: (bi, jnp.minimum(i + first_tile, 1), 0, 0)),
            pl.BlockSpec((2, d), lambda bi, i: (0, 0)),
        ],
        out_specs=pl.BlockSpec((1, tm, d), lambda bi, i: (bi, i, 0)),
        compiler_params=_params("parallel", "parallel"),
        name="combine_norm",
    )(x1, ys, table, mods, ln)


def _proj_columns():
    o = IN_OFFS
    cols = list(range(o[0], o[7]))
    xbc, dt = o[7], o[8]
    for g in range(SSM_G):
        cols += [xbc + g * SSM_GW + i for i in range(SSM_GW)]
        cols += [xbc + SSM_W + g * SSM_N + i for i in range(SSM_N)]
        cols += [xbc + SSM_W + SSM_G * SSM_N + g * SSM_N + i for i in range(SSM_N)]
        cols += [dt + d * SSM_HEADS + g * SSM_K + k for d in range(2) for k in range(SSM_K)]
        cols += [-1] * (BRANCH_W - SSD_XBC - 2 * SSM_K)
    return np.asarray(cols, np.int32)


def _ssd_conv_columns():
    cols = []
    for g in range(SSM_G):
        cols += [g * SSM_GW + i for i in range(SSM_GW)]
        cols += [SSM_W + g * SSM_N + i for i in range(SSM_N)]
        cols += [SSM_W + SSM_G * SSM_N + g * SSM_N + i for i in range(SSM_N)]
    return np.asarray(cols, np.int32)


def _rope_tables(n_lat, n_ctx):
    rows = n_lat // GRID_W
    row = jnp.repeat(jnp.arange(rows, dtype=F32), GRID_W)
    col = jnp.tile(jnp.arange(GRID_W, dtype=F32), rows)
    inv = ROPE_BASE ** (-jnp.arange(ROPE_F, dtype=F32) / ROPE_F)
    ang = jnp.stack([row[:, None] * inv, col[:, None] * inv], axis=1)
    cos = jnp.broadcast_to(jnp.cos(ang)[:, None, :, None, :], (n_lat, 2, 2, 2, ROPE_F))
    sin = jnp.sin(ang)[:, None, :, None, :] * jnp.asarray([-1.0, 1.0], F32)[None, None, None, :, None]
    sin = jnp.broadcast_to(sin, (n_lat, 2, 2, 2, ROPE_F))
    hw = 2 * ATT_DIM
    cos = jnp.concatenate([jnp.ones((n_ctx, hw), F32), cos.reshape(n_lat, hw)], axis=0)
    sin = jnp.concatenate([jnp.zeros((n_ctx, hw), F32), sin.reshape(n_lat, hw)], axis=0)
    return cos, sin


def _block_diag(w):
    n, r, c = w.shape
    eye = jnp.eye(n, dtype=w.dtype)
    return (w[:, :, None, :] * eye[:, None, :, None]).reshape(n * r, n * c)


def _lane_pad(v, width=LANES):
    return jnp.pad(v, [(0, 0)] * (v.ndim - 1) + [(0, width - v.shape[-1])])


def _slot_tables(dest, counts):
    n = dest.shape[0]
    a = n * TOP_K
    n_blocks = -(-a // EXPERT_BLOCK) + N_EXPERTS
    pblocks = (counts + EXPERT_BLOCK - 1) // EXPERT_BLOCK
    pend_b = jnp.cumsum(pblocks)
    start = jnp.cumsum(counts) - counts
    blk = jnp.arange(n_blocks, dtype=jnp.int32)
    block_e = jnp.minimum(jnp.sum((blk[:, None] >= pend_b[None, :]).astype(jnp.int32), axis=1), N_EXPERTS - 1)
    n_used = pend_b[N_EXPERTS - 1:].astype(jnp.int32)
    tok = jnp.arange(a, dtype=jnp.int32) // TOP_K
    _, sorted_tok = lax.sort_key_val(dest.reshape(a), tok)
    slots = n_blocks * EXPERT_BLOCK
    max_pad = N_EXPERTS * EXPERT_BLOCK
    padded = jnp.concatenate([jnp.zeros((max_pad,), jnp.int32), sorted_tok, jnp.zeros((slots - a,), jnp.int32)])
    shift = (pend_b - pblocks) * EXPERT_BLOCK - start
    onehot = block_e[:, None] == jnp.arange(N_EXPERTS)[None, :]
    first_blk = jnp.sum(jnp.where(onehot, (pend_b - pblocks)[None, :], 0), axis=1)
    n_valid = jnp.sum(jnp.where(onehot, counts[None, :], 0), axis=1) - (blk - first_blk) * EXPERT_BLOCK
    valid = (blk < n_used[0])[:, None] & (jnp.arange(EXPERT_BLOCK)[None, :] < n_valid[:, None])
    slot_e = jnp.where(valid, block_e[:, None], N_EXPERTS).reshape(slots)
    slot_tok = jnp.zeros((slots,), jnp.int32)
    for e in range(N_EXPERTS):
        slot_tok = jnp.where(slot_e == e, lax.dynamic_slice(padded, (max_pad - shift[e],), (slots,)), slot_tok)
    return block_e, n_used, slot_tok


EXPERT_PIECES = 4


def kernel(x, c, ctx, c_ctx, w_ada, b_ada, w_in, rnn_conv_w, rnn_conv_b, rnn_wa, rnn_ba, rnn_wx, rnn_bx, rnn_lam, att_lambda, att_subln, pool_w, pool_b, pool_scale, ssm_conv_w, ssm_conv_b, ssm_dt_bias, ssm_a_log, ssm_d, ssm_norm, w_branch, w_out, ln1_g, ln1_b, ln2_g, ln2_b, router_w, router_b, w_gate_up, b_gate_up, w_down, b_down):
    bsz, n_lat, d = x.shape
    n_ctx = ctx.shape[1]
    total = n_ctx + n_lat
    assert n_ctx == ROW_TILE and n_lat % ROW_TILE == 0 and n_lat % GRID_W == 0
    ctx_tiles = n_ctx // ROW_TILE
    cos_t, sin_t = _rope_tables(n_lat, n_ctx)
    pcols = _proj_columns()
    w_main = jnp.where(pcols[None, None, :] >= 0, jnp.take(w_in, jnp.maximum(pcols, 0), axis=2), 0.0).astype(BF16)
    w_gates = w_in[:, :, IN_OFFS[9]:].astype(BF16)
    w_branch_b = w_branch.astype(BF16)
    w_out_b = w_out.astype(BF16)
    w_router = _lane_pad(router_w).astype(BF16)
    b_router = _lane_pad(router_b)[:, None, :]
    n_cg = RNN_W // LANES
    bpg = LANES // RNN_BLOCK
    gate_blocks = jnp.stack([rnn_wa[:, 0], rnn_wx[:, 0], rnn_wa[:, 1], rnn_wx[:, 1]], axis=1)
    gate_blocks = gate_blocks.reshape(DEPTH, 4, n_cg, bpg, RNN_BLOCK, RNN_BLOCK)
    rnn_wg = jax.vmap(jax.vmap(jax.vmap(_block_diag)))(gate_blocks)
    rnn_wg = rnn_wg.transpose(0, 2, 3, 1, 4).reshape(DEPTH, n_cg, LANES, 4 * LANES).astype(BF16)
    gate_bias = jnp.stack([rnn_ba[:, 0], rnn_bx[:, 0], rnn_ba[:, 1], rnn_bx[:, 1]], axis=1)
    rnn_bg = gate_bias.reshape(DEPTH, 4, n_cg, LANES).transpose(0, 2, 1, 3).reshape(DEPTH, n_cg, 1, 4 * LANES)
    rnn_sp = jax.nn.softplus(-rnn_lam).reshape(DEPTH, 2, n_cg, LANES).transpose(0, 2, 1, 3).reshape(DEPTH, n_cg, 1, 2 * LANES)
    ccols = _ssd_conv_columns()
    ssd_cw = jnp.take(ssm_conv_w, ccols, axis=2).reshape(DEPTH, CONV_W, SSM_G, SSD_XBC).transpose(0, 2, 1, 3)
    ssd_cb = jnp.take(ssm_conv_b, ccols, axis=1).reshape(DEPTH, SSM_G, 1, SSD_XBC)
    per_group = lambda v: v.reshape(DEPTH, 2, SSM_G, SSM_K).transpose(0, 2, 1, 3).reshape(DEPTH, SSM_G, 1, 2 * SSM_K)
    ssd_dtb = _lane_pad(per_group(ssm_dt_bias))
    ssd_alog = _lane_pad(per_group(ssm_a_log))
    ssd_dsk = jnp.repeat((ssm_d[:, 0] + ssm_d[:, 1]).reshape(DEPTH, SSM_G, 1, SSM_K), SSM_P, axis=-1)
    ssd_nw = ssm_norm.reshape(DEPTH, SSM_G, 1, SSM_GW)
    pool_wb = pool_w.astype(BF16)

    xs = jnp.concatenate([ctx, x], axis=1)
    out = None
    for li in range(DEPTH):
        need_ctx = li < DEPTH - 1
        first = 0 if need_ctx else ctx_tiles
        mod_l = jax.nn.silu(c) @ w_ada[li] + b_ada[li]
        mod_c = jnp.broadcast_to(jax.nn.silu(c_ctx) @ w_ada[li] + b_ada[li], mod_l.shape)
        mods = jnp.stack([mod_c, mod_l], axis=1).reshape(bsz, 2, 6, d)

        proj = in_proj(xs, mods, w_main[li], n_ctx, 768 if total % 768 == 0 else ROW_TILE, BRANCH_W * 3 if PROJ_W % (BRANCH_W * 3) == 0 else BRANCH_W)
        y_rnn = rglru(proj, rnn_conv_w[li], rnn_conv_b[li][None], rnn_wg[li], rnn_bg[li], rnn_sp[li], n_ctx)
        lam_init = 0.8 - 0.6 * math.exp(-0.3 * li)
        lv = att_lambda[li]
        lam = jnp.exp(jnp.sum(lv[0] * lv[1])) - jnp.exp(jnp.sum(lv[2] * lv[3])) + lam_init
        y_att = diff_attention(proj, cos_t, sin_t, att_subln[li], lam, jnp.asarray(1.0 - lam_init, F32), n_ctx, need_ctx)
        y_pool = pool(proj, pool_wb[li], pool_b[li][None], pool_scale[li][None], n_ctx)
        y_ssd = ssd(proj, ssd_cw[li], ssd_cb[li], ssd_dtb[li], ssd_alog[li], ssd_dsk[li], ssd_nw[li], n_ctx)
        ln1 = jnp.stack([ln1_g[li], ln1_b[li]])
        x1, h2, logits = merge(xs, mods, (y_rnn, y_att, y_pool, y_ssd), w_gates[li], w_branch_b[li], w_out_b[li], ln1,
                               w_router[li], b_router[li], first)

        n_rout = total - first * ROW_TILE
        table, counts = route(logits[:, first * ROW_TILE:].reshape(bsz * n_rout, LANES))
        dest = table[:, ROUTE_DEST:ROUTE_DEST + TOP_K].astype(jnp.int32)
        block_e, n_used, slot_tok = _slot_tables(dest, counts[0, :N_EXPERTS].astype(jnp.int32))
        slot_row = slot_tok + (slot_tok // n_rout + 1) * (first * ROW_TILE)
        h2_rows = h2.reshape(bsz * total, d)
        n_blocks = block_e.shape[0]
        pieces = EXPERT_PIECES if n_blocks % EXPERT_PIECES == 0 else 1
        piece_rows = n_blocks // pieces * EXPERT_BLOCK
        yb = None
        for p in range(pieces):
            xb = h2_rows[slot_row[p * piece_rows:(p + 1) * piece_rows]]
            yb = expert_blocks(li, block_e, n_used, xb, yb, p * (n_blocks // pieces), n_blocks,
                               w_gate_up, b_gate_up, w_down, b_down)
        ys = yb[dest.reshape(-1)].reshape(bsz, n_rout, TOP_K * d)
        ln2 = jnp.stack([ln2_g[li], ln2_b[li]])
        out = combine_norm(x1, ys, table.reshape(bsz, n_rout, LANES), mods, ln2, first)
        xs = out
    return out
```

```python
import functools
import math

import jax
import jax.numpy as jnp
import numpy as np
from jax import lax
from jax.experimental import pallas as pl
from jax.experimental.pallas import tpu as pltpu

D_MODEL = 1024
DEPTH = 4
GRID_W = 64
N_BRANCH = 4
BRANCH_W = D_MODEL // 2
RNN_W = BRANCH_W
RNN_BLOCK = 64
RG_C = 8.0
CONV_W = 4
ATT_DIM = 64
ATT_HEADS = BRANCH_W // (2 * ATT_DIM)
ATT_V_W = ATT_HEADS * 2 * ATT_DIM
ROPE_F = ATT_DIM // 4
ROPE_BASE = 10000.0
POOL_W = BRANCH_W
POOL_WINDOWS = (2, 4, 8, 16)
POOL_GROUPS = len(POOL_WINDOWS)
POOL_GW = POOL_W // POOL_GROUPS
SSM_W = BRANCH_W
SSM_P = 64
SSM_HEADS = SSM_W // SSM_P
SSM_G = 2
SSM_K = SSM_HEADS // SSM_G
SSM_N = 64
SSM_GW = SSM_W // SSM_G
SSM_CHUNK = 128
N_EXPERTS = 32
TOP_K = 4
D_EXPERT = D_MODEL
SWIGLU_LIMIT = 7.0
SWIGLU_ALPHA = 1.702
EXPERT_BLOCK = 256
DN_ALPHA = (2 * DEPTH) ** 0.25
IN_WIDTHS = (RNN_W, RNN_W, BRANCH_W, BRANCH_W, ATT_V_W, POOL_W, SSM_W, SSM_W + 2 * SSM_G * SSM_N, 2 * SSM_HEADS, N_BRANCH * D_MODEL)
IN_OFFS = tuple(int(v) for v in np.cumsum((0,) + IN_WIDTHS))

LANES = 128
SUBLANES = 8
VMEM_LIMIT = 48 * 1024 * 1024
ROW_TILE = 256
EXPERT_VMEM_LIMIT = 54 * 1024 * 1024
MERGE_VMEM_LIMIT = 56 * 1024 * 1024

COL_RX, COL_RG, COL_Q, COL_K, COL_V, COL_PU, COL_Z, COL_SSD = 0, 1, 2, 3, 4, 5, 6, 7
PROJ_W = (COL_SSD + SSM_G) * BRANCH_W
SSD_XBC = SSM_GW + 2 * SSM_N

BF16 = jnp.bfloat16
F32 = jnp.float32


def _params(*sem):
    return pltpu.CompilerParams(dimension_semantics=sem, vmem_limit_bytes=VMEM_LIMIT)


def _normalize(x, eps):
    mu = jnp.mean(x, -1, keepdims=True)
    xc = x - mu
    var = jnp.mean(xc * xc, -1, keepdims=True)
    return xc * lax.rsqrt(var + eps)


def _segment_valid(t, off, n_ctx, total):
    lo = jnp.where(t < n_ctx, 0, n_ctx)
    hi = jnp.where(t < n_ctx, n_ctx, total)
    return (t + off >= lo) & (t + off < hi)


def _shift_rows(u, off, t, n_ctx):
    total = u.shape[0]
    rolled = pltpu.roll(u, (-off) % total, 0)
    return jnp.where(_segment_valid(t, off, n_ctx, total), rolled, 0.0)


def _centred_conv(u, pad_ref, w_ref, b_ref, n_ctx):
    total, c = u.shape
    gap = SUBLANES
    zeros = jnp.zeros((gap, c), F32)
    pad_ref[0:gap, :] = zeros
    pad_ref[gap:gap + n_ctx, :] = u[0:n_ctx]
    pad_ref[gap + n_ctx:2 * gap + n_ctx, :] = zeros
    pad_ref[2 * gap + n_ctx:2 * gap + total, :] = u[n_ctx:total]
    pad_ref[2 * gap + total:3 * gap + total, :] = zeros

    def tap(off):
        return jnp.concatenate([pad_ref[pl.ds(gap + off, n_ctx), :],
                                pad_ref[pl.ds(2 * gap + n_ctx + off, total - n_ctx), :]], axis=0)

    left = CONV_W // 2
    out = b_ref[...] + tap(-left) * w_ref[0:1, :]
    for k in range(1, CONV_W):
        out = out + (u if k == left else tap(k - left)) * w_ref[k:k + 1, :]
    return out


def _in_proj_kernel(x_ref, mod_ref, w_ref, o_ref, h_ref, *, n_ctx):
    @pl.when(pl.program_id(2) == 0)
    def _():
        tm = x_ref.shape[1]
        row = pl.program_id(1) * tm + lax.broadcasted_iota(jnp.int32, (tm, 1), 0)
        is_ctx = row < n_ctx
        shift = jnp.where(is_ctx, mod_ref[0, 0, 0:1, :], mod_ref[0, 1, 0:1, :])
        scale = jnp.where(is_ctx, mod_ref[0, 0, 1:2, :], mod_ref[0, 1, 1:2, :])
        h_ref[...] = (_normalize(x_ref[0], 1e-6) * (1.0 + scale) + shift).astype(BF16)

    o_ref[0] = jnp.dot(h_ref[...], w_ref[...], preferred_element_type=F32)


def in_proj(x, mods, w, n_ctx, tm, tn):
    b, t, d = x.shape
    n = w.shape[1]
    return pl.pallas_call(
        functools.partial(_in_proj_kernel, n_ctx=n_ctx),
        out_shape=jax.ShapeDtypeStruct((b, t, n), F32),
        grid=(b, t // tm, n // tn),
        in_specs=[
            pl.BlockSpec((1, tm, d), lambda bi, i, j: (bi, i, 0)),
            pl.BlockSpec((1, 2, 6, d), lambda bi, i, j: (bi, 0, 0, 0)),
            pl.BlockSpec((d, tn), lambda bi, i, j: (0, j)),
        ],
        out_specs=pl.BlockSpec((1, tm, tn), lambda bi, i, j: (bi, i, j)),
        scratch_shapes=[pltpu.VMEM((tm, d), BF16)],
        compiler_params=_params("parallel", "parallel", "arbitrary"),
        name="in_proj",
    )(x, mods, w)


SCAN_ROWS = 64


def _tile_scan(a, b, reverse):
    rows = a.shape[0]
    sub = lax.broadcasted_iota(jnp.int32, (rows, 1), 0) % SUBLANES
    s = 1
    while s < SUBLANES:
        shift = (rows - s) if reverse else s
        keep = (sub + s < SUBLANES) if reverse else (sub >= s)
        a_sh = pltpu.roll(a, shift, 0)
        b_sh = pltpu.roll(b, shift, 0)
        b = jnp.where(keep, a * b_sh + b, b)
        a = jnp.where(keep, a * a_sh, a)
        s *= 2
    return a, b


def _rglru_kernel(rx_ref, rg_ref, cw_ref, cb_ref, wg_ref, bg_ref, sp_ref, o_ref, a_ref, b_ref, y_ref, pad_ref, *, n_ctx):
    total = rx_ref.shape[1]
    cw = rx_ref.shape[2]
    u = _centred_conv(rx_ref[0], pad_ref, cw_ref, cb_ref, n_ctx)
    g = jnp.dot(u.astype(BF16), wg_ref[0], preferred_element_type=F32) + bg_ref[0]
    n_tiles = SCAN_ROWS // SUBLANES

    for d in range(2):
        r = jax.nn.sigmoid(g[:, (2 * d) * cw:(2 * d + 1) * cw])
        i = jax.nn.sigmoid(g[:, (2 * d + 1) * cw:(2 * d + 2) * cw])
        log_a = -RG_C * r * sp_ref[0, :, d * cw:(d + 1) * cw]
        a = jnp.exp(log_a)
        a_ref[d] = a
        b_ref[d] = jnp.sqrt((1.0 - a) * (1.0 + a)) * (i * u)

    def scan_rows(r0, h, d):
        reverse = d == 1
        a, b = _tile_scan(a_ref[d, pl.ds(r0, SCAN_ROWS), :], b_ref[d, pl.ds(r0, SCAN_ROWS), :], reverse)
        outs = [None] * n_tiles
        for j in (range(n_tiles - 1, -1, -1) if reverse else range(n_tiles)):
            sl = slice(j * SUBLANES, (j + 1) * SUBLANES)
            hj = b[sl] + a[sl] * h
            h = hj[0:1] if reverse else hj[SUBLANES - 1:SUBLANES]
            outs[j] = hj
        y_ref[d, pl.ds(r0, SCAN_ROWS), :] = jnp.concatenate(outs, axis=0)
        return h

    def step(it, hs, lo, n_steps):
        h_f = scan_rows(pl.multiple_of(lo + it * SCAN_ROWS, SCAN_ROWS), hs[0], 0)
        h_b = scan_rows(pl.multiple_of(lo + (n_steps - 1 - it) * SCAN_ROWS, SCAN_ROWS), hs[1], 1)
        return h_f, h_b

    h0 = jnp.zeros((1, cw), F32)
    hs = lax.fori_loop(0, n_ctx // SCAN_ROWS, functools.partial(step, lo=0, n_steps=n_ctx // SCAN_ROWS), (h0, h0))
    n_lat_steps = (total - n_ctx) // SCAN_ROWS
    lax.fori_loop(0, n_lat_steps, functools.partial(step, lo=n_ctx, n_steps=n_lat_steps), hs)

    o_ref[0] = (jax.nn.gelu(rg_ref[0]) * (y_ref[0] + y_ref[1])).astype(BF16)


def rglru(proj, conv_w, conv_b, wg, bg, sp, n_ctx):
    b, t, _ = proj.shape
    cw = LANES
    n_cg = RNN_W // cw
    per_block = BRANCH_W // cw
    return pl.pallas_call(
        functools.partial(_rglru_kernel, n_ctx=n_ctx),
        out_shape=jax.ShapeDtypeStruct((b, t, RNN_W), BF16),
        grid=(b, n_cg),
        in_specs=[
            pl.BlockSpec((1, t, cw), lambda bi, c: (bi, 0, COL_RX * per_block + c)),
            pl.BlockSpec((1, t, cw), lambda bi, c: (bi, 0, COL_RG * per_block + c)),
            pl.BlockSpec((CONV_W, cw), lambda bi, c: (0, c)),
            pl.BlockSpec((1, cw), lambda bi, c: (0, c)),
            pl.BlockSpec((1, cw, 4 * cw), lambda bi, c: (c, 0, 0)),
            pl.BlockSpec((1, 1, 4 * cw), lambda bi, c: (c, 0, 0)),
            pl.BlockSpec((1, 1, 2 * cw), lambda bi, c: (c, 0, 0)),
        ],
        out_specs=pl.BlockSpec((1, t, cw), lambda bi, c: (bi, 0, c)),
        scratch_shapes=[pltpu.VMEM((2, t, cw), F32), pltpu.VMEM((2, t, cw), F32), pltpu.VMEM((2, t, cw), F32),
                        pltpu.VMEM((t + 3 * SUBLANES, cw), F32)],
        compiler_params=_params("parallel", "parallel"),
        name="rglru",
    )(proj, proj, conv_w, conv_b, wg, bg, sp)


def _pool_kernel(u_ref, w_ref, b_ref, s_ref, o_ref, *, n_ctx):
    total = u_ref.shape[1]
    t = lax.broadcasted_iota(jnp.int32, (total, 1), 0)
    for gi, win in enumerate(POOL_WINDOWS):
        cols = slice(gi * POOL_GW, (gi + 1) * POOL_GW)
        u = u_ref[0, :, cols]
        acc = u
        cnt = jnp.ones((total, 1), F32)
        for off in range(-(win // 2), win - win // 2):
            if off == 0:
                continue
            acc = acc + _shift_rows(u, off, t, n_ctx)
            cnt = cnt + _segment_valid(t, off, n_ctx, total).astype(F32)
        mix = acc / cnt - u
        y = jnp.dot(mix.astype(BF16), w_ref[gi], preferred_element_type=F32) + b_ref[:, cols]
        o_ref[0, :, cols] = (y * s_ref[:, cols]).astype(BF16)


def pool(proj, w, bias, scale, n_ctx):
    b, t, _ = proj.shape
    return pl.pallas_call(
        functools.partial(_pool_kernel, n_ctx=n_ctx),
        out_shape=jax.ShapeDtypeStruct((b, t, POOL_W), BF16),
        grid=(b,),
        in_specs=[
            pl.BlockSpec((1, t, POOL_W), lambda bi: (bi, 0, COL_PU)),
            pl.BlockSpec((POOL_GROUPS, POOL_GW, POOL_GW), lambda bi: (0, 0, 0)),
            pl.BlockSpec((1, POOL_W), lambda bi: (0, 0)),
            pl.BlockSpec((1, POOL_W), lambda bi: (0, 0)),
        ],
        out_specs=pl.BlockSpec((1, t, POOL_W), lambda bi: (bi, 0, 0)),
        compiler_params=_params("parallel"),
        name="pool",
    )(proj, w, bias, scale)


def _bf16_pieces(v):
    hi = v.astype(BF16)
    r1 = v - hi.astype(F32)
    mid = r1.astype(BF16)
    lo = (r1 - mid.astype(F32)).astype(BF16)
    return hi, mid, lo


def _split_dot(tri, v):
    return sum(jnp.dot(tri, p, preferred_element_type=F32) for p in _bf16_pieces(v))


def _split_dot_lhs(v, tri):
    return sum(jnp.dot(p, tri, preferred_element_type=F32) for p in _bf16_pieces(v))


def _per_head(cols, width):
    head = lax.broadcasted_iota(jnp.int32, (1, width), 1) // SSM_P
    out = cols[SSM_K - 1]
    for k in range(SSM_K - 2, -1, -1):
        out = jnp.where(head == k, cols[k], out)
    return out


def _ssd_kernel(blk_ref, z_ref, cw_ref, cb_ref, dtb_ref, alog_ref, dsk_ref, nw_ref, o_ref,
                u_ref, dt_ref, adt_ref, tr_ref, y_ref, st_ref, pad_ref, csc_ref, csr_ref, *, n_ctx):
    total = blk_ref.shape[1]
    ck = SSM_CHUNK
    n_chunks = total // ck
    ctx_chunks = n_ctx // ck
    u_ref[...] = jax.nn.silu(_centred_conv(blk_ref[0, :, 0:SSD_XBC], pad_ref, cw_ref.at[0], cb_ref.at[0], n_ctx))
    dt = jax.nn.softplus(blk_ref[0, :, SSD_XBC:SSD_XBC + LANES] + dtb_ref[0])
    dt_ref[...] = dt
    adt_ref[...] = dt * (-jnp.exp(alog_ref[0]))
    for c in range(n_chunks):
        rows = slice(c * ck, (c + 1) * ck)
        tr_ref[c, 0:ck, :] = u_ref[rows, SSM_GW:SSM_GW + 2 * SSM_N].T
        tr_ref[c, ck:2 * ck, :] = adt_ref[rows, :].T

    ri = lax.broadcasted_iota(jnp.int32, (ck, ck), 0)
    ci = lax.broadcasted_iota(jnp.int32, (ck, ck), 1)
    lower = (ci <= ri)
    lower_b = lower.astype(BF16)
    upper_b = (ci >= ri).astype(BF16)
    lane = lax.broadcasted_iota(jnp.int32, (1, LANES), 1)
    head_w = lax.broadcasted_iota(jnp.int32, (1, SSM_GW), 1) // SSM_P
    srow = lax.broadcasted_iota(jnp.int32, (ck, 1), 0)

    a_cat = jnp.concatenate([adt_ref[c * ck:(c + 1) * ck, :] for c in range(n_chunks)], axis=1)
    a_rows = jnp.concatenate([tr_ref[c, ck:ck + SUBLANES, :] for c in range(n_chunks)], axis=0)
    for d, (tri_col, tri_row) in enumerate(((lower_b, upper_b), (upper_b, lower_b))):
        cols_all = _split_dot(tri_col, a_cat)
        for c in range(n_chunks):
            csc_ref[d, c] = cols_all[:, c * ck:(c + 1) * ck]
        csr_ref[d] = _split_dot_lhs(a_rows, tri_row)

    def chunk(c, d):
        reverse = d == 1
        causal = (ci >= ri) if reverse else lower
        r0 = pl.multiple_of(c * ck, ck)
        xs = u_ref[pl.ds(r0, ck), 0:SSM_GW]
        bc = u_ref[pl.ds(r0, ck), SSM_GW:SSM_GW + 2 * SSM_N]
        tr = tr_ref[c, 0:ck, :]
        c_lo = jnp.where(lane < SSM_N, pltpu.roll(bc, SSM_N, 1), 0.0).astype(BF16)
        cb = jnp.dot(c_lo, tr.astype(BF16), preferred_element_type=F32)
        cs_col = csc_ref[d, c]
        cs_row = csr_ref[d, pl.ds(pl.multiple_of(c * SUBLANES, SUBLANES), SUBLANES), :]
        edge = ck - 1 if not reverse else 0
        dtc = dt_ref[pl.ds(r0, ck), :]
        cols, tots, dts = [], [], []
        y = jnp.zeros((ck, SSM_GW), F32)
        for k in range(SSM_K):
            j = d * SSM_K + k
            cols.append(cs_col[:, j:j + 1])
            tots.append(cs_col[edge:edge + 1, j:j + 1])
            dts.append(dtc[:, j:j + 1])
        xdt = xs * _per_head(dts, SSM_GW)
        for k in range(SSM_K):
            j = d * SSM_K + k
            seg = cols[k] - cs_row[j:j + 1, :]
            m = (cb * jnp.where(causal, jnp.exp(seg), 0.0)).astype(BF16)
            y = y + jnp.dot(m, jnp.where(head_w == k, xdt, 0.0).astype(BF16), preferred_element_type=F32)
        col_w = _per_head(cols, SSM_GW)
        tot_w = _per_head(tots, SSM_GW)
        state = st_ref[d]
        y = y + jnp.dot(c_lo, state.astype(BF16), preferred_element_type=F32) * jnp.exp(col_w)
        xd = (xdt * jnp.exp(tot_w - col_w)).astype(BF16)
        new = jnp.exp(tot_w) * state + jnp.dot(tr.astype(BF16), xd, preferred_element_type=F32)
        st_ref[d] = jnp.where(srow < SSM_N, new, 0.0)
        y_ref[d, pl.ds(r0, ck), :] = y

    def both(it, carry, lo, n_steps):
        chunk(lo + it, 0)
        chunk(lo + n_steps - 1 - it, 1)
        return carry

    st_ref[...] = jnp.zeros_like(st_ref)
    lax.fori_loop(0, ctx_chunks, functools.partial(both, lo=0, n_steps=ctx_chunks), 0)
    lat_chunks = n_chunks - ctx_chunks
    lax.fori_loop(0, lat_chunks, functools.partial(both, lo=ctx_chunks, n_steps=lat_chunks), 0,
                  unroll=2 if lat_chunks % 2 == 0 else 1)

    y = y_ref[0] + y_ref[1] + dsk_ref[0] * u_ref[:, 0:SSM_GW]
    g = y * jax.nn.silu(z_ref[0])
    g = g * lax.rsqrt(jnp.mean(g * g, -1, keepdims=True) + 1e-5)
    o_ref[0] = (g * nw_ref[0]).astype(BF16)


def ssd(proj, conv_w, conv_b, dt_bias, a_log, d_skip, norm_w, n_ctx):
    b, t, _ = proj.shape
    n_chunks = t // SSM_CHUNK
    z_per = BRANCH_W // SSM_GW
    return pl.pallas_call(
        functools.partial(_ssd_kernel, n_ctx=n_ctx),
        out_shape=jax.ShapeDtypeStruct((b, t, SSM_W), BF16),
        grid=(b, SSM_G),
        in_specs=[
            pl.BlockSpec((1, t, BRANCH_W), lambda bi, g: (bi, 0, COL_SSD + g)),
            pl.BlockSpec((1, t, SSM_GW), lambda bi, g: (bi, 0, COL_Z * z_per + g)),
            pl.BlockSpec((1, CONV_W, SSD_XBC), lambda bi, g: (g, 0, 0)),
            pl.BlockSpec((1, 1, SSD_XBC), lambda bi, g: (g, 0, 0)),
            pl.BlockSpec((1, 1, LANES), lambda bi, g: (g, 0, 0)),
            pl.BlockSpec((1, 1, LANES), lambda bi, g: (g, 0, 0)),
            pl.BlockSpec((1, 1, SSM_GW), lambda bi, g: (g, 0, 0)),
            pl.BlockSpec((1, 1, SSM_GW), lambda bi, g: (g, 0, 0)),
        ],
        out_specs=pl.BlockSpec((1, t, SSM_GW), lambda bi, g: (bi, 0, g)),
        scratch_shapes=[
            pltpu.VMEM((t, SSD_XBC), F32),
            pltpu.VMEM((t, LANES), F32),
            pltpu.VMEM((t, LANES), F32),
            pltpu.VMEM((n_chunks, 2 * SSM_CHUNK, SSM_CHUNK), F32),
            pltpu.VMEM((2, t, SSM_GW), F32),
            pltpu.VMEM((2, SSM_CHUNK, SSM_GW), F32),
            pltpu.VMEM((t + 3 * SUBLANES, SSD_XBC), F32),
            pltpu.VMEM((2, n_chunks, SSM_CHUNK, LANES), F32),
            pltpu.VMEM((2, n_chunks * SUBLANES, SSM_CHUNK), F32),
        ],
        compiler_params=_params("parallel", "parallel"),
        name="ssd",
    )(proj, proj, conv_w, conv_b, dt_bias, a_log, d_skip, norm_w)


def _rope(x, cos, sin):
    lane = lax.broadcasted_iota(jnp.int32, (1, x.shape[1]), 1)
    partner = jnp.where(lane % (2 * ROPE_F) < ROPE_F, pltpu.roll(x, x.shape[1] - ROPE_F, 1), pltpu.roll(x, ROPE_F, 1))
    return x * cos + partner * sin


Q_TILES = (1024, 512, 256)


def _diff_attn_kernel(par_ref, q_ref, k_ref, v_ref, cos_ref, sin_ref, g_ref, o_ref, kb_ref, vb_ref,
                      *, n_ctx, need_ctx, q_tile):
    lam = par_ref[0]
    out_scale = par_ref[1]
    total = k_ref.shape[1]
    kb_ref[...] = _rope(k_ref[0], cos_ref[...], sin_ref[...]).astype(BF16)
    vb_ref[...] = v_ref[0].astype(BF16)
    dims = (((1,), (1,)), ((), ()))

    def attend(r0, rows, n_keys):
        q = _rope(q_ref[0, pl.ds(r0, rows), :], cos_ref[pl.ds(r0, rows), :], sin_ref[pl.ds(r0, rows), :])
        q = (q * (ATT_DIM ** -0.5 * math.log2(math.e))).astype(BF16)
        lane = lax.broadcasted_iota(jnp.int32, q.shape, 1)
        k = kb_ref[0:n_keys, :]
        v = vb_ref[0:n_keys, :]

        def softmax_v(qh):
            s = lax.dot_general(qh, k, dims, preferred_element_type=F32)
            e = jnp.exp2(s - jnp.max(s, -1, keepdims=True))
            return jnp.dot(e.astype(BF16), v, preferred_element_type=F32) / jnp.sum(e, -1, keepdims=True)

        o = (softmax_v(jnp.where(lane < ATT_DIM, q, jnp.zeros_like(q)))
             - lam * softmax_v(jnp.where(lane >= ATT_DIM, q, jnp.zeros_like(q))))
        o = o * lax.rsqrt(jnp.mean(o * o, -1, keepdims=True) + 1e-5) * g_ref[...] * out_scale
        o_ref[0, pl.ds(r0, rows), :] = o.astype(BF16)

    if need_ctx:
        attend(0, n_ctx, n_ctx)

    def latent_tile(i, carry):
        attend(pl.multiple_of(n_ctx + i * q_tile, ROW_TILE), q_tile, total)
        return carry

    lax.fori_loop(0, (total - n_ctx) // q_tile, latent_tile, 0)


def diff_attention(proj, cos_t, sin_t, subln, lam, out_scale, n_ctx, need_ctx):
    b, t, _ = proj.shape
    hw = 2 * ATT_DIM
    per_block = BRANCH_W // hw
    q_tile = next(q for q in Q_TILES if (t - n_ctx) % q == 0)
    par = jnp.stack([lam, out_scale]).astype(F32)
    col = lambda block: (lambda bi, h: (bi, 0, block * per_block + h))
    whole = lambda bi, h: (0, 0)
    return pl.pallas_call(
        functools.partial(_diff_attn_kernel, n_ctx=n_ctx, need_ctx=need_ctx, q_tile=q_tile),
        out_shape=jax.ShapeDtypeStruct((b, t, ATT_V_W), BF16),
        grid=(b, ATT_HEADS),
        in_specs=[
            pl.BlockSpec(memory_space=pltpu.SMEM),
            pl.BlockSpec((1, t, hw), col(COL_Q)),
            pl.BlockSpec((1, t, hw), col(COL_K)),
            pl.BlockSpec((1, t, hw), col(COL_V)),
            pl.BlockSpec((t, hw), whole),
            pl.BlockSpec((t, hw), whole),
            pl.BlockSpec((1, hw), whole),
        ],
        out_specs=pl.BlockSpec((1, t, hw), lambda bi, h: (bi, 0, h)),
        scratch_shapes=[pltpu.VMEM((t, hw), BF16), pltpu.VMEM((t, hw), BF16)],
        compiler_params=_params("parallel", "parallel"),
        name="diff_attention",
    )(par, proj, proj, proj, cos_t, sin_t, subln.reshape(1, hw).astype(F32))


def _merge_kernel(x_ref, mod_ref, y0_ref, y1_ref, y2_ref, y3_ref, wg_ref, wb_ref, wo_ref, ln_ref, wr_ref, br_ref,
                  x1_ref, h2_ref, lg_ref):
    x = x_ref[0]
    mod = mod_ref[0, 0]
    h = (_normalize(x, 1e-6) * (1.0 + mod[1:2]) + mod[0:1]).astype(BF16)
    m = None
    for i, y_ref in enumerate((y0_ref, y1_ref, y2_ref, y3_ref)):
        gate = jax.nn.sigmoid(jnp.dot(h, wg_ref[:, i * D_MODEL:(i + 1) * D_MODEL], preferred_element_type=F32))
        term = gate * jnp.dot(y_ref[0], wb_ref[i], preferred_element_type=F32)
        m = term if m is None else m + term
    y = jnp.dot(m.astype(BF16), wo_ref[...], preferred_element_type=F32)
    x1 = _normalize(DN_ALPHA * x + mod[2:3] * y, 1e-5) * ln_ref[0:1] + ln_ref[1:2]
    x1_ref[0] = x1
    h2 = (_normalize(x1, 1e-6) * (1.0 + mod[4:5]) + mod[3:4]).astype(BF16)
    h2_ref[0] = h2
    lg_ref[0] = jnp.dot(h2, wr_ref[...], preferred_element_type=F32) + br_ref[...]


def merge(x, mods, ys, wg, wb, wo, ln, wr, br, first_tile):
    b, t, d = x.shape
    tm = ROW_TILE
    row = lambda bi, i: (bi, i + first_tile, 0)
    const2 = lambda bi, i: (0, 0)
    y_spec = pl.BlockSpec((1, tm, BRANCH_W), row)
    return pl.pallas_call(
        _merge_kernel,
        out_shape=(jax.ShapeDtypeStruct((b, t, d), F32), jax.ShapeDtypeStruct((b, t, d), BF16),
                   jax.ShapeDtypeStruct((b, t, LANES), F32)),
        grid=(b, t // tm - first_tile),
        in_specs=[
            pl.BlockSpec((1, tm, d), row),
            pl.BlockSpec((1, 1, 6, d), lambda bi, i: (bi, jnp.minimum(i + first_tile, 1), 0, 0)),
            y_spec, y_spec, y_spec, y_spec,
            pl.BlockSpec((d, N_BRANCH * d), const2),
            pl.BlockSpec((N_BRANCH, BRANCH_W, d), lambda bi, i: (0, 0, 0)),
            pl.BlockSpec((d, d), const2),
            pl.BlockSpec((2, d), const2),
            pl.BlockSpec((d, LANES), const2),
            pl.BlockSpec((1, LANES), const2),
        ],
        out_specs=(pl.BlockSpec((1, tm, d), row), pl.BlockSpec((1, tm, d), row), pl.BlockSpec((1, tm, LANES), row)),
        compiler_params=pltpu.CompilerParams(dimension_semantics=("parallel", "parallel"),
                                             vmem_limit_bytes=MERGE_VMEM_LIMIT),
        name="merge",
    )(x, mods, *ys, wg, wb, wo, ln, wr, br)


def _expert_kernel(be_ref, nb_ref, x_ref, wgu_ref, bgu_ref, wd_ref, bd_ref, *rest, first_block):
    o_ref, wgu_b, wd_b = rest[-3:]
    i = pl.program_id(0)
    blk = first_block + i

    @pl.when(blk < nb_ref[0])
    def _():
        @pl.when((i == 0) | (be_ref[blk] != be_ref[jnp.maximum(blk - 1, 0)]))
        def _():
            wgu_b[...] = wgu_ref[0, 0].astype(BF16)
            wd_b[...] = wd_ref[0, 0].astype(BF16)

        gu = jnp.dot(x_ref[...], wgu_b[...], preferred_element_type=F32) + bgu_ref[0, 0]
        gate = jnp.minimum(gu[:, :D_EXPERT], SWIGLU_LIMIT)
        up = jnp.clip(gu[:, D_EXPERT:], -SWIGLU_LIMIT, SWIGLU_LIMIT)
        glu = gate * jax.nn.sigmoid(gate * SWIGLU_ALPHA)
        act = ((up + 1.0) * glu).astype(BF16)
        y = jnp.dot(act, wd_b[...], preferred_element_type=F32) + bd_ref[0, 0]
        o_ref[...] = y.astype(BF16)


def expert_blocks(layer, block_e, n_used, xb, yb_prev, first_block, n_blocks_total, wgu, bgu, wd, bd):
    rows, d = xb.shape
    n_piece = rows // EXPERT_BLOCK

    def local(i, nb):
        return jnp.maximum(jnp.minimum(i, nb[0] - 1 - first_block), 0)

    expert = lambda i, be, nb: (layer, be[first_block + local(i, nb)], 0, 0)
    in_specs = [
        pl.BlockSpec((EXPERT_BLOCK, d), lambda i, be, nb: (local(i, nb), 0)),
        pl.BlockSpec((1, 1, d, 2 * D_EXPERT), expert),
        pl.BlockSpec((1, 1, 1, 2 * D_EXPERT), expert),
        pl.BlockSpec((1, 1, D_EXPERT, d), expert),
        pl.BlockSpec((1, 1, 1, d), expert),
    ]
    args = [block_e, n_used, xb, wgu, bgu[:, :, None, :], wd, bd[:, :, None, :]]
    aliases = {}
    if yb_prev is not None:
        in_specs.append(pl.BlockSpec(memory_space=pl.ANY))
        aliases = {len(args): 0}
        args.append(yb_prev)
    grid_spec = pltpu.PrefetchScalarGridSpec(
        num_scalar_prefetch=2,
        grid=(n_piece,),
        in_specs=in_specs,
        out_specs=pl.BlockSpec((EXPERT_BLOCK, d), lambda i, be, nb: (first_block + local(i, nb), 0)),
        scratch_shapes=[pltpu.VMEM((d, 2 * D_EXPERT), BF16), pltpu.VMEM((D_EXPERT, d), BF16)],
    )
    return pl.pallas_call(
        functools.partial(_expert_kernel, first_block=first_block),
        out_shape=jax.ShapeDtypeStruct((n_blocks_total * EXPERT_BLOCK, d), BF16),
        grid_spec=grid_spec,
        input_output_aliases=aliases,
        compiler_params=pltpu.CompilerParams(dimension_semantics=("arbitrary",), vmem_limit_bytes=EXPERT_VMEM_LIMIT),
        name="expert_blocks",
    )(*args)


ROUTE_DEST, ROUTE_PROB = 0, TOP_K
ROUTE_TILES = (1024, 512, 256)


def _route_kernel(lg_ref, out_ref, cnt_ref, run_ref, base_ref):
    phase = pl.program_id(0)
    i = pl.program_id(1)
    tm = lg_ref.shape[0]
    lane = lax.broadcasted_iota(jnp.int32, (1, LANES), 1).astype(F32)
    cur = jnp.where(lane < N_EXPERTS, lg_ref[...], -jnp.inf)
    vals, picks = [], []
    member = jnp.zeros((tm, LANES), F32)
    for _ in range(TOP_K):
        v = jnp.max(cur, -1, keepdims=True)
        pick = lane == jnp.min(jnp.where(cur == v, lane, float(LANES)), -1, keepdims=True)
        member = member + pick.astype(F32)
        cur = jnp.where(pick, -jnp.inf, cur)
        vals.append(v)
        picks.append(pick)
    tile_counts = jnp.sum(member, 0, keepdims=True)

    @pl.when(phase == 0)
    def _():
        @pl.when(i == 0)
        def _():
            cnt_ref[...] = jnp.zeros_like(cnt_ref)
        cnt_ref[...] = cnt_ref[...] + tile_counts

    @pl.when(phase == 1)
    def _():
        @pl.when(i == 0)
        def _():
            blocks = jnp.floor((cnt_ref[...] + (EXPERT_BLOCK - 1)) * (1.0 / EXPERT_BLOCK))
            r = lax.broadcasted_iota(jnp.int32, (LANES, LANES), 0)
            c = lax.broadcasted_iota(jnp.int32, (LANES, LANES), 1)
            before = _split_dot_lhs(jnp.broadcast_to(blocks, (SUBLANES, LANES)), (r < c).astype(BF16))
            base_ref[...] = before[0:1] * float(EXPERT_BLOCK)
            run_ref[...] = jnp.zeros_like(run_ref)

        r = lax.broadcasted_iota(jnp.int32, (tm, tm), 0)
        c = lax.broadcasted_iota(jnp.int32, (tm, tm), 1)
        earlier = jnp.dot((c < r).astype(BF16), member.astype(BF16), preferred_element_type=F32)
        slot = base_ref[...] + run_ref[...] + earlier
        denom = 1.0
        exps = [1.0]
        for k in range(1, TOP_K):
            exps.append(jnp.exp(vals[k] - vals[0]))
            denom = denom + exps[k]
        out_lane = lax.broadcasted_iota(jnp.int32, (1, LANES), 1)
        row = jnp.zeros((tm, LANES), F32)
        for k in range(TOP_K):
            dest = jnp.sum(jnp.where(picks[k], slot, 0.0), -1, keepdims=True)
            row = jnp.where(out_lane == ROUTE_DEST + k, dest, row)
            row = jnp.where(out_lane == ROUTE_PROB + k, exps[k] / denom, row)
        out_ref[...] = row
        run_ref[...] = run_ref[...] + tile_counts


def route(logits):
    n = logits.shape[0]
    tm = next(t for t in ROUTE_TILES if n % t == 0)
    return pl.pallas_call(
        _route_kernel,
        out_shape=(jax.ShapeDtypeStruct((n, LANES), F32), jax.ShapeDtypeStruct((1, LANES), F32)),
        grid=(2, n // tm),
        in_specs=[pl.BlockSpec((tm, LANES), lambda ph, i: (i, 0))],
        out_specs=(pl.BlockSpec((tm, LANES), lambda ph, i: (i * ph, 0)), pl.BlockSpec((1, LANES), lambda ph, i: (0, 0))),
        scratch_shapes=[pltpu.VMEM((1, LANES), F32), pltpu.VMEM((1, LANES), F32)],
        compiler_params=_params("arbitrary", "arbitrary"),
        name="route",
    )(logits)


def _combine_norm_kernel(x_ref, y_ref, rt_ref, mod_ref, ln_ref, o_ref):
    f = None
    for k in range(TOP_K):
        term = rt_ref[0, :, ROUTE_PROB + k:ROUTE_PROB + k + 1] * y_ref[k, 0].astype(F32)
        f = term if f is None else f + term
    x2 = DN_ALPHA * x_ref[0] + mod_ref[0, 0, 5:6] * f
    o_ref[0] = _normalize(x2, 1e-5) * ln_ref[0:1] + ln_ref[1:2]


def combine_norm(x1, ys, table, mods, ln, first_tile):
    b, t, d = x1.shape
    tm = ROW_TILE
    n_tiles = t // tm - first_tile
    return pl.pallas_call(
        _combine_norm_kernel,
        out_shape=jax.ShapeDtypeStruct((b, n_tiles * tm, d), F32),
        grid=(b, n_tiles),
        in_specs=[
            pl.BlockSpec((1, tm, d), lambda bi, i: (bi, i + first_tile, 0)),
            pl.BlockSpec((TOP_K, 1, tm, d), lambda bi, i: (0, bi, i, 0)),
            pl.BlockSpec((1, tm, LANES), lambda bi, i: (bi, i, 0)),
            pl.BlockSpec((1, 1, 6, d), lambda bi, i: (bi, jnp.minimum(i + first_tile, 1), 0, 0)),
            pl.BlockSpec((2, d), lambda bi, i: (0, 0)),
        ],
        out_specs=pl.BlockSpec((1, tm, d), lambda bi, i: (bi, i, 0)),
        compiler_params=_params("parallel", "parallel"),
        name="combine_norm",
    )(x1, ys, table, mods, ln)


def _proj_columns():
    o = IN_OFFS
    cols = list(range(o[0], o[7]))
    xbc, dt = o[7], o[8]
    for g in range(SSM_G):
        cols += [xbc + g * SSM_GW + i for i in range(SSM_GW)]
        cols += [xbc + SSM_W + g * SSM_N + i for i in range(SSM_N)]
        cols += [xbc + SSM_W + SSM_G * SSM_N + g * SSM_N + i for i in range(SSM_N)]
        cols += [dt + d * SSM_HEADS + g * SSM_K + k for d in range(2) for k in range(SSM_K)]
        cols += [-1] * (BRANCH_W - SSD_XBC - 2 * SSM_K)
    return np.asarray(cols, np.int32)


def _ssd_conv_columns():
    cols = []
    for g in range(SSM_G):
        cols += [g * SSM_GW + i for i in range(SSM_GW)]
        cols += [SSM_W + g * SSM_N + i for i in range(SSM_N)]
        cols += [SSM_W + SSM_G * SSM_N + g * SSM_N + i for i in range(SSM_N)]
    return np.asarray(cols, np.int32)


def _rope_tables(n_lat, n_ctx):
    rows = n_lat // GRID_W
    row = jnp.repeat(jnp.arange(rows, dtype=F32), GRID_W)
    col = jnp.tile(jnp.arange(GRID_W, dtype=F32), rows)
    inv = ROPE_BASE ** (-jnp.arange(ROPE_F, dtype=F32) / ROPE_F)
    ang = jnp.stack([row[:, None] * inv, col[:, None] * inv], axis=1)
    cos = jnp.broadcast_to(jnp.cos(ang)[:, None, :, None, :], (n_lat, 2, 2, 2, ROPE_F))
    sin = jnp.sin(ang)[:, None, :, None, :] * jnp.asarray([-1.0, 1.0], F32)[None, None, None, :, None]
    sin = jnp.broadcast_to(sin, (n_lat, 2, 2, 2, ROPE_F))
    hw = 2 * ATT_DIM
    cos = jnp.concatenate([jnp.ones((n_ctx, hw), F32), cos.reshape(n_lat, hw)], axis=0)
    sin = jnp.concatenate([jnp.zeros((n_ctx, hw), F32), sin.reshape(n_lat, hw)], axis=0)
    return cos, sin


def _block_diag(w):
    n, r, c = w.shape
    eye = jnp.eye(n, dtype=w.dtype)
    return (w[:, :, None, :] * eye[:, None, :, None]).reshape(n * r, n * c)


def _lane_pad(v, width=LANES):
    return jnp.pad(v, [(0, 0)] * (v.ndim - 1) + [(0, width - v.shape[-1])])


def _slot_tables(dest, counts):
    n = dest.shape[0]
    a = n * TOP_K
    n_blocks = -(-a // EXPERT_BLOCK) + N_EXPERTS
    pblocks = (counts + EXPERT_BLOCK - 1) // EXPERT_BLOCK
    pend_b = jnp.cumsum(pblocks)
    start = jnp.cumsum(counts) - counts
    blk = jnp.arange(n_blocks, dtype=jnp.int32)
    block_e = jnp.minimum(jnp.sum((blk[:, None] >= pend_b[None, :]).astype(jnp.int32), axis=1), N_EXPERTS - 1)
    n_used = pend_b[N_EXPERTS - 1:].astype(jnp.int32)
    tok = jnp.arange(a, dtype=jnp.int32) // TOP_K
    _, sorted_tok = lax.sort_key_val(dest.reshape(a), tok)
    slots = n_blocks * EXPERT_BLOCK
    max_pad = N_EXPERTS * EXPERT_BLOCK
    padded = jnp.concatenate([jnp.zeros((max_pad,), jnp.int32), sorted_tok, jnp.zeros((slots - a,), jnp.int32)])
    shift = (pend_b - pblocks) * EXPERT_BLOCK - start
    onehot = block_e[:, None] == jnp.arange(N_EXPERTS)[None, :]
    first_blk = jnp.sum(jnp.where(onehot, (pend_b - pblocks)[None, :], 0), axis=1)
    n_valid = jnp.sum(jnp.where(onehot, counts[None, :], 0), axis=1) - (blk - first_blk) * EXPERT_BLOCK
    valid = (blk < n_used[0])[:, None] & (jnp.arange(EXPERT_BLOCK)[None, :] < n_valid[:, None])
    slot_e = jnp.where(valid, block_e[:, None], N_EXPERTS).reshape(slots)
    slot_tok = jnp.zeros((slots,), jnp.int32)
    for e in range(N_EXPERTS):
        slot_tok = jnp.where(slot_e == e, lax.dynamic_slice(padded, (max_pad - shift[e],), (slots,)), slot_tok)
    return block_e, n_used, slot_tok


EXPERT_PIECES = 4


def kernel(x, c, ctx, c_ctx, w_ada, b_ada, w_in, rnn_conv_w, rnn_conv_b, rnn_wa, rnn_ba, rnn_wx, rnn_bx, rnn_lam, att_lambda, att_subln, pool_w, pool_b, pool_scale, ssm_conv_w, ssm_conv_b, ssm_dt_bias, ssm_a_log, ssm_d, ssm_norm, w_branch, w_out, ln1_g, ln1_b, ln2_g, ln2_b, router_w, router_b, w_gate_up, b_gate_up, w_down, b_down):
    bsz, n_lat, d = x.shape
    n_ctx = ctx.shape[1]
    total = n_ctx + n_lat
    assert n_ctx == ROW_TILE and n_lat % ROW_TILE == 0 and n_lat % GRID_W == 0
    ctx_tiles = n_ctx // ROW_TILE
    cos_t, sin_t = _rope_tables(n_lat, n_ctx)
    pcols = _proj_columns()
    w_main = jnp.where(pcols[None, None, :] >= 0, jnp.take(w_in, jnp.maximum(pcols, 0), axis=2), 0.0).astype(BF16)
    w_gates = w_in[:, :, IN_OFFS[9]:].astype(BF16)
    w_branch_b = w_branch.astype(BF16)
    w_out_b = w_out.astype(BF16)
    w_router = _lane_pad(router_w).astype(BF16)
    b_router = _lane_pad(router_b)[:, None, :]
    n_cg = RNN_W // LANES
    bpg = LANES // RNN_BLOCK
    gate_blocks = jnp.stack([rnn_wa[:, 0], rnn_wx[:, 0], rnn_wa[:, 1], rnn_wx[:, 1]], axis=1)
    gate_blocks = gate_blocks.reshape(DEPTH, 4, n_cg, bpg, RNN_BLOCK, RNN_BLOCK)
    rnn_wg = jax.vmap(jax.vmap(jax.vmap(_block_diag)))(gate_blocks)
    rnn_wg = rnn_wg.transpose(0, 2, 3, 1, 4).reshape(DEPTH, n_cg, LANES, 4 * LANES).astype(BF16)
    gate_bias = jnp.stack([rnn_ba[:, 0], rnn_bx[:, 0], rnn_ba[:, 1], rnn_bx[:, 1]], axis=1)
    rnn_bg = gate_bias.reshape(DEPTH, 4, n_cg, LANES).transpose(0, 2, 1, 3).reshape(DEPTH, n_cg, 1, 4 * LANES)
    rnn_sp = jax.nn.softplus(-rnn_lam).reshape(DEPTH, 2, n_cg, LANES).transpose(0, 2, 1, 3).reshape(DEPTH, n_cg, 1, 2 * LANES)
    ccols = _ssd_conv_columns()
    ssd_cw = jnp.take(ssm_conv_w, ccols, axis=2).reshape(DEPTH, CONV_W, SSM_G, SSD_XBC).transpose(0, 2, 1, 3)
    ssd_cb = jnp.take(ssm_conv_b, ccols, axis=1).reshape(DEPTH, SSM_G, 1, SSD_XBC)
    per_group = lambda v: v.reshape(DEPTH, 2, SSM_G, SSM_K).transpose(0, 2, 1, 3).reshape(DEPTH, SSM_G, 1, 2 * SSM_K)
    ssd_dtb = _lane_pad(per_group(ssm_dt_bias))
    ssd_alog = _lane_pad(per_group(ssm_a_log))
    ssd_dsk = jnp.repeat((ssm_d[:, 0] + ssm_d[:, 1]).reshape(DEPTH, SSM_G, 1, SSM_K), SSM_P, axis=-1)
    ssd_nw = ssm_norm.reshape(DEPTH, SSM_G, 1, SSM_GW)
    pool_wb = pool_w.astype(BF16)

    xs = jnp.concatenate([ctx, x], axis=1)
    out = None
    for li in range(DEPTH):
        need_ctx = li < DEPTH - 1
        first = 0 if need_ctx else ctx_tiles
        mod_l = jax.nn.silu(c) @ w_ada[li] + b_ada[li]
        mod_c = jnp.broadcast_to(jax.nn.silu(c_ctx) @ w_ada[li] + b_ada[li], mod_l.shape)
        mods = jnp.stack([mod_c, mod_l], axis=1).reshape(bsz, 2, 6, d)

        proj = in_proj(xs, mods, w_main[li], n_ctx, 768 if total % 768 == 0 else ROW_TILE, BRANCH_W * 3 if PROJ_W % (BRANCH_W * 3) == 0 else BRANCH_W)
        y_rnn = rglru(proj, rnn_conv_w[li], rnn_conv_b[li][None], rnn_wg[li], rnn_bg[li], rnn_sp[li], n_ctx)
        lam_init = 0.8 - 0.6 * math.exp(-0.3 * li)
        lv = att_lambda[li]
        lam = jnp.exp(jnp.sum(lv[0] * lv[1])) - jnp.exp(jnp.sum(lv[2] * lv[3])) + lam_init
        y_att = diff_attention(proj, cos_t, sin_t, att_subln[li], lam, jnp.asarray(1.0 - lam_init, F32), n_ctx, need_ctx)
        y_pool = pool(proj, pool_wb[li], pool_b[li][None], pool_scale[li][None], n_ctx)
        y_ssd = ssd(proj, ssd_cw[li], ssd_cb[li], ssd_dtb[li], ssd_alog[li], ssd_dsk[li], ssd_nw[li], n_ctx)
        ln1 = jnp.stack([ln1_g[li], ln1_b[li]])
        x1, h2, logits = merge(xs, mods, (y_rnn, y_att, y_pool, y_ssd), w_gates[li], w_branch_b[li], w_out_b[li], ln1,
                               w_router[li], b_router[li], first)

        n_rout = total - first * ROW_TILE
        table, counts = route(logits[:, first * ROW_TILE:].reshape(bsz * n_rout, LANES))
        dest = table[:, ROUTE_DEST:ROUTE_DEST + TOP_K].astype(jnp.int32)
        block_e, n_used, slot_tok = _slot_tables(dest, counts[0, :N_EXPERTS].astype(jnp.int32))
        slot_row = slot_tok + (slot_tok // n_rout + 1) * (first * ROW_TILE)
        h2_rows = h2.reshape(bsz * total, d)
        n_blocks = block_e.shape[0]
        pieces = EXPERT_PIECES if n_blocks % EXPERT_PIECES == 0 else 1
        piece_rows = n_blocks // pieces * EXPERT_BLOCK
        yb = None
        for p in range(pieces):
            xb = h2_rows[slot_row[p * piece_rows:(p + 1) * piece_rows]]
            yb = expert_blocks(li, block_e, n_used, xb, yb, p * (n_blocks // pieces), n_blocks,
                               w_gate_up, b_gate_up, w_down, b_down)
        ys = yb[dest.T.reshape(-1)].reshape(TOP_K, bsz, n_rout, d)
        ln2 = jnp.stack([ln2_g[li], ln2_b[li]])
        out = combine_norm(x1, ys, table.reshape(bsz, n_rout, LANES), mods, ln2, first)
        xs = out
    return out
```

```python
import functools
import math

import jax
import jax.numpy as jnp
import numpy as np
from jax import lax
from jax.experimental import pallas as pl
from jax.experimental.pallas import tpu as pltpu

D_MODEL = 1024
DEPTH = 4
GRID_W = 64
N_BRANCH = 4
BRANCH_W = D_MODEL // 2
RNN_W = BRANCH_W
RNN_BLOCK = 64
RG_C = 8.0
CONV_W = 4
ATT_DIM = 64
ATT_HEADS = BRANCH_W // (2 * ATT_DIM)
ATT_V_W = ATT_HEADS * 2 * ATT_DIM
ROPE_F = ATT_DIM // 4
ROPE_BASE = 10000.0
POOL_W = BRANCH_W
POOL_WINDOWS = (2, 4, 8, 16)
POOL_GROUPS = len(POOL_WINDOWS)
POOL_GW = POOL_W // POOL_GROUPS
SSM_W = BRANCH_W
SSM_P = 64
SSM_HEADS = SSM_W // SSM_P
SSM_G = 2
SSM_K = SSM_HEADS // SSM_G
SSM_N = 64
SSM_GW = SSM_W // SSM_G
SSM_CHUNK = 128
N_EXPERTS = 32
TOP_K = 4
D_EXPERT = D_MODEL
SWIGLU_LIMIT = 7.0
SWIGLU_ALPHA = 1.702
EXPERT_BLOCK = 256
DN_ALPHA = (2 * DEPTH) ** 0.25
IN_WIDTHS = (RNN_W, RNN_W, BRANCH_W, BRANCH_W, ATT_V_W, POOL_W, SSM_W, SSM_W + 2 * SSM_G * SSM_N, 2 * SSM_HEADS, N_BRANCH * D_MODEL)
IN_OFFS = tuple(int(v) for v in np.cumsum((0,) + IN_WIDTHS))

LANES = 128
SUBLANES = 8
VMEM_LIMIT = 48 * 1024 * 1024
ROW_TILE = 256
EXPERT_VMEM_LIMIT = 54 * 1024 * 1024
MERGE_VMEM_LIMIT = 56 * 1024 * 1024

COL_RX, COL_RG, COL_Q, COL_K, COL_V, COL_PU, COL_Z, COL_SSD = 0, 1, 2, 3, 4, 5, 6, 7
PROJ_W = (COL_SSD + SSM_G) * BRANCH_W
SSD_XBC = SSM_GW + 2 * SSM_N

BF16 = jnp.bfloat16
F32 = jnp.float32


def _params(*sem):
    return pltpu.CompilerParams(dimension_semantics=sem, vmem_limit_bytes=VMEM_LIMIT)


def _normalize(x, eps):
    mu = jnp.mean(x, -1, keepdims=True)
    xc = x - mu
    var = jnp.mean(xc * xc, -1, keepdims=True)
    return xc * lax.rsqrt(var + eps)


def _segment_valid(t, off, n_ctx, total):
    lo = jnp.where(t < n_ctx, 0, n_ctx)
    hi = jnp.where(t < n_ctx, n_ctx, total)
    return (t + off >= lo) & (t + off < hi)


def _shift_rows(u, off, t, n_ctx):
    total = u.shape[0]
    rolled = pltpu.roll(u, (-off) % total, 0)
    return jnp.where(_segment_valid(t, off, n_ctx, total), rolled, 0.0)


def _centred_conv(u, pad_ref, w_ref, b_ref, n_ctx):
    total, c = u.shape
    gap = SUBLANES
    zeros = jnp.zeros((gap, c), F32)
    pad_ref[0:gap, :] = zeros
    pad_ref[gap:gap + n_ctx, :] = u[0:n_ctx]
    pad_ref[gap + n_ctx:2 * gap + n_ctx, :] = zeros
    pad_ref[2 * gap + n_ctx:2 * gap + total, :] = u[n_ctx:total]
    pad_ref[2 * gap + total:3 * gap + total, :] = zeros

    def tap(off):
        return jnp.concatenate([pad_ref[pl.ds(gap + off, n_ctx), :],
                                pad_ref[pl.ds(2 * gap + n_ctx + off, total - n_ctx), :]], axis=0)

    left = CONV_W // 2
    out = b_ref[...] + tap(-left) * w_ref[0:1, :]
    for k in range(1, CONV_W):
        out = out + (u if k == left else tap(k - left)) * w_ref[k:k + 1, :]
    return out


IN_PROJ_ROWS = (384, 256)


def _in_proj_kernel(x_ref, mod_ref, w_ref, o_ref, *, n_ctx):
    tm = x_ref.shape[1]
    row = pl.program_id(1) * tm + lax.broadcasted_iota(jnp.int32, (tm, 1), 0)
    is_ctx = row < n_ctx
    shift = jnp.where(is_ctx, mod_ref[0, 0, 0:1, :], mod_ref[0, 1, 0:1, :])
    scale = jnp.where(is_ctx, mod_ref[0, 0, 1:2, :], mod_ref[0, 1, 1:2, :])
    h = (_normalize(x_ref[0], 1e-6) * (1.0 + scale) + shift).astype(BF16)
    o_ref[0] = jnp.dot(h, w_ref[...], preferred_element_type=F32)


def in_proj(x, mods, w, n_ctx):
    b, t, d = x.shape
    n = w.shape[1]
    tm = next(r for r in IN_PROJ_ROWS if t % r == 0)
    return pl.pallas_call(
        functools.partial(_in_proj_kernel, n_ctx=n_ctx),
        out_shape=jax.ShapeDtypeStruct((b, t, n), F32),
        grid=(b, t // tm),
        in_specs=[
            pl.BlockSpec((1, tm, d), lambda bi, i: (bi, i, 0)),
            pl.BlockSpec((1, 2, 6, d), lambda bi, i: (bi, 0, 0, 0)),
            pl.BlockSpec((d, n), lambda bi, i: (0, 0)),
        ],
        out_specs=pl.BlockSpec((1, tm, n), lambda bi, i: (bi, i, 0)),
        compiler_params=_params("parallel", "parallel"),
        name="in_proj",
    )(x, mods, w)


SCAN_ROWS = 64


def _tile_scan(a, b, reverse):
    rows = a.shape[0]
    sub = lax.broadcasted_iota(jnp.int32, (rows, 1), 0) % SUBLANES
    s = 1
    while s < SUBLANES:
        shift = (rows - s) if reverse else s
        keep = (sub + s < SUBLANES) if reverse else (sub >= s)
        a_sh = pltpu.roll(a, shift, 0)
        b_sh = pltpu.roll(b, shift, 0)
        b = jnp.where(keep, a * b_sh + b, b)
        a = jnp.where(keep, a * a_sh, a)
        s *= 2
    return a, b


def _rglru_kernel(rx_ref, rg_ref, cw_ref, cb_ref, wg_ref, bg_ref, sp_ref, o_ref, a_ref, b_ref, y_ref, pad_ref, *, n_ctx):
    total = rx_ref.shape[1]
    cw = rx_ref.shape[2]
    u = _centred_conv(rx_ref[0], pad_ref, cw_ref, cb_ref, n_ctx)
    g = jnp.dot(u.astype(BF16), wg_ref[0], preferred_element_type=F32) + bg_ref[0]
    n_tiles = SCAN_ROWS // SUBLANES

    for d in range(2):
        r = jax.nn.sigmoid(g[:, (2 * d) * cw:(2 * d + 1) * cw])
        i = jax.nn.sigmoid(g[:, (2 * d + 1) * cw:(2 * d + 2) * cw])
        log_a = -RG_C * r * sp_ref[0, :, d * cw:(d + 1) * cw]
        a = jnp.exp(log_a)
        a_ref[d] = a
        b_ref[d] = jnp.sqrt((1.0 - a) * (1.0 + a)) * (i * u)

    def scan_rows(r0, h, d):
        reverse = d == 1
        a, b = _tile_scan(a_ref[d, pl.ds(r0, SCAN_ROWS), :], b_ref[d, pl.ds(r0, SCAN_ROWS), :], reverse)
        outs = [None] * n_tiles
        for j in (range(n_tiles - 1, -1, -1) if reverse else range(n_tiles)):
            sl = slice(j * SUBLANES, (j + 1) * SUBLANES)
            hj = b[sl] + a[sl] * h
            h = hj[0:1] if reverse else hj[SUBLANES - 1:SUBLANES]
            outs[j] = hj
        y_ref[d, pl.ds(r0, SCAN_ROWS), :] = jnp.concatenate(outs, axis=0)
        return h

    def step(it, hs, lo, n_steps):
        h_f = scan_rows(pl.multiple_of(lo + it * SCAN_ROWS, SCAN_ROWS), hs[0], 0)
        h_b = scan_rows(pl.multiple_of(lo + (n_steps - 1 - it) * SCAN_ROWS, SCAN_ROWS), hs[1], 1)
        return h_f, h_b

    h0 = jnp.zeros((1, cw), F32)
    hs = lax.fori_loop(0, n_ctx // SCAN_ROWS, functools.partial(step, lo=0, n_steps=n_ctx // SCAN_ROWS), (h0, h0))
    n_lat_steps = (total - n_ctx) // SCAN_ROWS
    lax.fori_loop(0, n_lat_steps, functools.partial(step, lo=n_ctx, n_steps=n_lat_steps), hs)

    o_ref[0] = (jax.nn.gelu(rg_ref[0]) * (y_ref[0] + y_ref[1])).astype(BF16)


def rglru(proj, conv_w, conv_b, wg, bg, sp, n_ctx):
    b, t, _ = proj.shape
    cw = LANES
    n_cg = RNN_W // cw
    per_block = BRANCH_W // cw
    return pl.pallas_call(
        functools.partial(_rglru_kernel, n_ctx=n_ctx),
        out_shape=jax.ShapeDtypeStruct((b, t, RNN_W), BF16),
        grid=(b, n_cg),
        in_specs=[
            pl.BlockSpec((1, t, cw), lambda bi, c: (bi, 0, COL_RX * per_block + c)),
            pl.BlockSpec((1, t, cw), lambda bi, c: (bi, 0, COL_RG * per_block + c)),
            pl.BlockSpec((CONV_W, cw), lambda bi, c: (0, c)),
            pl.BlockSpec((1, cw), lambda bi, c: (0, c)),
            pl.BlockSpec((1, cw, 4 * cw), lambda bi, c: (c, 0, 0)),
            pl.BlockSpec((1, 1, 4 * cw), lambda bi, c: (c, 0, 0)),
            pl.BlockSpec((1, 1, 2 * cw), lambda bi, c: (c, 0, 0)),
        ],
        out_specs=pl.BlockSpec((1, t, cw), lambda bi, c: (bi, 0, c)),
        scratch_shapes=[pltpu.VMEM((2, t, cw), F32), pltpu.VMEM((2, t, cw), F32), pltpu.VMEM((2, t, cw), F32),
                        pltpu.VMEM((t + 3 * SUBLANES, cw), F32)],
        compiler_params=_params("parallel", "parallel"),
        name="rglru",
    )(proj, proj, conv_w, conv_b, wg, bg, sp)


def _pool_kernel(u_ref, w_ref, b_ref, s_ref, o_ref, *, n_ctx):
    total = u_ref.shape[1]
    t = lax.broadcasted_iota(jnp.int32, (total, 1), 0)
    for gi, win in enumerate(POOL_WINDOWS):
        cols = slice(gi * POOL_GW, (gi + 1) * POOL_GW)
        u = u_ref[0, :, cols]
        acc = u
        cnt = jnp.ones((total, 1), F32)
        for off in range(-(win // 2), win - win // 2):
            if off == 0:
                continue
            acc = acc + _shift_rows(u, off, t, n_ctx)
            cnt = cnt + _segment_valid(t, off, n_ctx, total).astype(F32)
        mix = acc / cnt - u
        y = jnp.dot(mix.astype(BF16), w_ref[gi], preferred_element_type=F32) + b_ref[:, cols]
        o_ref[0, :, cols] = (y * s_ref[:, cols]).astype(BF16)


def pool(proj, w, bias, scale, n_ctx):
    b, t, _ = proj.shape
    return pl.pallas_call(
        functools.partial(_pool_kernel, n_ctx=n_ctx),
        out_shape=jax.ShapeDtypeStruct((b, t, POOL_W), BF16),
        grid=(b,),
        in_specs=[
            pl.BlockSpec((1, t, POOL_W), lambda bi: (bi, 0, COL_PU)),
            pl.BlockSpec((POOL_GROUPS, POOL_GW, POOL_GW), lambda bi: (0, 0, 0)),
            pl.BlockSpec((1, POOL_W), lambda bi: (0, 0)),
            pl.BlockSpec((1, POOL_W), lambda bi: (0, 0)),
        ],
        out_specs=pl.BlockSpec((1, t, POOL_W), lambda bi: (bi, 0, 0)),
        compiler_params=_params("parallel"),
        name="pool",
    )(proj, w, bias, scale)


def _bf16_pieces(v):
    hi = v.astype(BF16)
    r1 = v - hi.astype(F32)
    mid = r1.astype(BF16)
    lo = (r1 - mid.astype(F32)).astype(BF16)
    return hi, mid, lo


def _split_dot(tri, v):
    return sum(jnp.dot(tri, p, preferred_element_type=F32) for p in _bf16_pieces(v))


def _split_dot_lhs(v, tri):
    return sum(jnp.dot(p, tri, preferred_element_type=F32) for p in _bf16_pieces(v))


def _per_head(cols, width):
    head = lax.broadcasted_iota(jnp.int32, (1, width), 1) // SSM_P
    out = cols[SSM_K - 1]
    for k in range(SSM_K - 2, -1, -1):
        out = jnp.where(head == k, cols[k], out)
    return out


def _ssd_kernel(blk_ref, z_ref, cw_ref, cb_ref, dtb_ref, alog_ref, dsk_ref, nw_ref, o_ref,
                u_ref, dt_ref, adt_ref, tr_ref, y_ref, st_ref, pad_ref, csc_ref, csr_ref, *, n_ctx):
    total = blk_ref.shape[1]
    ck = SSM_CHUNK
    n_chunks = total // ck
    ctx_chunks = n_ctx // ck
    u_ref[...] = jax.nn.silu(_centred_conv(blk_ref[0, :, 0:SSD_XBC], pad_ref, cw_ref.at[0], cb_ref.at[0], n_ctx))
    dt = jax.nn.softplus(blk_ref[0, :, SSD_XBC:SSD_XBC + LANES] + dtb_ref[0])
    dt_ref[...] = dt
    adt_ref[...] = dt * (-jnp.exp(alog_ref[0]))
    for c in range(n_chunks):
        rows = slice(c * ck, (c + 1) * ck)
        tr_ref[c, 0:ck, :] = u_ref[rows, SSM_GW:SSM_GW + 2 * SSM_N].T
        tr_ref[c, ck:2 * ck, :] = adt_ref[rows, :].T

    ri = lax.broadcasted_iota(jnp.int32, (ck, ck), 0)
    ci = lax.broadcasted_iota(jnp.int32, (ck, ck), 1)
    lower = (ci <= ri)
    lower_b = lower.astype(BF16)
    upper_b = (ci >= ri).astype(BF16)
    lane = lax.broadcasted_iota(jnp.int32, (1, LANES), 1)
    head_w = lax.broadcasted_iota(jnp.int32, (1, SSM_GW), 1) // SSM_P
    srow = lax.broadcasted_iota(jnp.int32, (ck, 1), 0)

    a_cat = jnp.concatenate([adt_ref[c * ck:(c + 1) * ck, :] for c in range(n_chunks)], axis=1)
    a_rows = jnp.concatenate([tr_ref[c, ck:ck + SUBLANES, :] for c in range(n_chunks)], axis=0)
    for d, (tri_col, tri_row) in enumerate(((lower_b, upper_b), (upper_b, lower_b))):
        cols_all = _split_dot(tri_col, a_cat)
        for c in range(n_chunks):
            csc_ref[d, c] = cols_all[:, c * ck:(c + 1) * ck]
        csr_ref[d] = _split_dot_lhs(a_rows, tri_row)

    def chunk(c, d):
        reverse = d == 1
        causal = (ci >= ri) if reverse else lower
        r0 = pl.multiple_of(c * ck, ck)
        xs = u_ref[pl.ds(r0, ck), 0:SSM_GW]
        bc = u_ref[pl.ds(r0, ck), SSM_GW:SSM_GW + 2 * SSM_N]
        tr = tr_ref[c, 0:ck, :]
        c_lo = jnp.where(lane < SSM_N, pltpu.roll(bc, SSM_N, 1), 0.0).astype(BF16)
        cb = jnp.dot(c_lo, tr.astype(BF16), preferred_element_type=F32)
        cs_col = csc_ref[d, c]
        cs_row = csr_ref[d, pl.ds(pl.multiple_of(c * SUBLANES, SUBLANES), SUBLANES), :]
        edge = ck - 1 if not reverse else 0
        dtc = dt_ref[pl.ds(r0, ck), :]
        cols, tots, dts = [], [], []
        y = jnp.zeros((ck, SSM_GW), F32)
        for k in range(SSM_K):
            j = d * SSM_K + k
            cols.append(cs_col[:, j:j + 1])
            tots.append(cs_col[edge:edge + 1, j:j + 1])
            dts.append(dtc[:, j:j + 1])
        xdt = xs * _per_head(dts, SSM_GW)
        for k in range(SSM_K):
            j = d * SSM_K + k
            seg = cols[k] - cs_row[j:j + 1, :]
            m = (cb * jnp.where(causal, jnp.exp(seg), 0.0)).astype(BF16)
            y = y + jnp.dot(m, jnp.where(head_w == k, xdt, 0.0).astype(BF16), preferred_element_type=F32)
        col_w = _per_head(cols, SSM_GW)
        tot_w = _per_head(tots, SSM_GW)
        state = st_ref[d]
        y = y + jnp.dot(c_lo, state.astype(BF16), preferred_element_type=F32) * jnp.exp(col_w)
        xd = (xdt * jnp.exp(tot_w - col_w)).astype(BF16)
        new = jnp.exp(tot_w) * state + jnp.dot(tr.astype(BF16), xd, preferred_element_type=F32)
        st_ref[d] = jnp.where(srow < SSM_N, new, 0.0)
        y_ref[d, pl.ds(r0, ck), :] = y

    def both(it, carry, lo, n_steps):
        chunk(lo + it, 0)
        chunk(lo + n_steps - 1 - it, 1)
        return carry

    st_ref[...] = jnp.zeros_like(st_ref)
    lax.fori_loop(0, ctx_chunks, functools.partial(both, lo=0, n_steps=ctx_chunks), 0)
    lat_chunks = n_chunks - ctx_chunks
    lax.fori_loop(0, lat_chunks, functools.partial(both, lo=ctx_chunks, n_steps=lat_chunks), 0,
                  unroll=2 if lat_chunks % 2 == 0 else 1)

    y = y_ref[0] + y_ref[1] + dsk_ref[0] * u_ref[:, 0:SSM_GW]
    g = y * jax.nn.silu(z_ref[0])
    g = g * lax.rsqrt(jnp.mean(g * g, -1, keepdims=True) + 1e-5)
    o_ref[0] = (g * nw_ref[0]).astype(BF16)


def ssd(proj, conv_w, conv_b, dt_bias, a_log, d_skip, norm_w, n_ctx):
    b, t, _ = proj.shape
    n_chunks = t // SSM_CHUNK
    z_per = BRANCH_W // SSM_GW
    return pl.pallas_call(
        functools.partial(_ssd_kernel, n_ctx=n_ctx),
        out_shape=jax.ShapeDtypeStruct((b, t, SSM_W), BF16),
        grid=(b, SSM_G),
        in_specs=[
            pl.BlockSpec((1, t, BRANCH_W), lambda bi, g: (bi, 0, COL_SSD + g)),
            pl.BlockSpec((1, t, SSM_GW), lambda bi, g: (bi, 0, COL_Z * z_per + g)),
            pl.BlockSpec((1, CONV_W, SSD_XBC), lambda bi, g: (g, 0, 0)),
            pl.BlockSpec((1, 1, SSD_XBC), lambda bi, g: (g, 0, 0)),
            pl.BlockSpec((1, 1, LANES), lambda bi, g: (g, 0, 0)),
            pl.BlockSpec((1, 1, LANES), lambda bi, g: (g, 0, 0)),
            pl.BlockSpec((1, 1, SSM_GW), lambda bi, g: (g, 0, 0)),
            pl.BlockSpec((1, 1, SSM_GW), lambda bi, g: (g, 0, 0)),
        ],
        out_specs=pl.BlockSpec((1, t, SSM_GW), lambda bi, g: (bi, 0, g)),
        scratch_shapes=[
            pltpu.VMEM((t, SSD_XBC), F32),
            pltpu.VMEM((t, LANES), F32),
            pltpu.VMEM((t, LANES), F32),
            pltpu.VMEM((n_chunks, 2 * SSM_CHUNK, SSM_CHUNK), F32),
            pltpu.VMEM((2, t, SSM_GW), F32),
            pltpu.VMEM((2, SSM_CHUNK, SSM_GW), F32),
            pltpu.VMEM((t + 3 * SUBLANES, SSD_XBC), F32),
            pltpu.VMEM((2, n_chunks, SSM_CHUNK, LANES), F32),
            pltpu.VMEM((2, n_chunks * SUBLANES, SSM_CHUNK), F32),
        ],
        compiler_params=_params("parallel", "parallel"),
        name="ssd",
    )(proj, proj, conv_w, conv_b, dt_bias, a_log, d_skip, norm_w)


def _rope(x, cos, sin):
    lane = lax.broadcasted_iota(jnp.int32, (1, x.shape[1]), 1)
    partner = jnp.where(lane % (2 * ROPE_F) < ROPE_F, pltpu.roll(x, x.shape[1] - ROPE_F, 1), pltpu.roll(x, ROPE_F, 1))
    return x * cos + partner * sin


Q_TILES = (1024, 512, 256)


def _diff_attn_kernel(par_ref, q_ref, k_ref, v_ref, cos_ref, sin_ref, g_ref, o_ref, kb_ref, vb_ref,
                      *, n_ctx, need_ctx, q_tile):
    lam = par_ref[0]
    out_scale = par_ref[1]
    total = k_ref.shape[1]
    kb_ref[...] = _rope(k_ref[0], cos_ref[...], sin_ref[...]).astype(BF16)
    vb_ref[...] = v_ref[0].astype(BF16)
    dims = (((1,), (1,)), ((), ()))

    def attend(r0, rows, n_keys):
        q = _rope(q_ref[0, pl.ds(r0, rows), :], cos_ref[pl.ds(r0, rows), :], sin_ref[pl.ds(r0, rows), :])
        q = (q * (ATT_DIM ** -0.5 * math.log2(math.e))).astype(BF16)
        lane = lax.broadcasted_iota(jnp.int32, q.shape, 1)
        k = kb_ref[0:n_keys, :]
        v = vb_ref[0:n_keys, :]

        def softmax_v(qh):
            s = lax.dot_general(qh, k, dims, preferred_element_type=F32)
            e = jnp.exp2(s - jnp.max(s, -1, keepdims=True))
            return jnp.dot(e.astype(BF16), v, preferred_element_type=F32) / jnp.sum(e, -1, keepdims=True)

        o = (softmax_v(jnp.where(lane < ATT_DIM, q, jnp.zeros_like(q)))
             - lam * softmax_v(jnp.where(lane >= ATT_DIM, q, jnp.zeros_like(q))))
        o = o * lax.rsqrt(jnp.mean(o * o, -1, keepdims=True) + 1e-5) * g_ref[...] * out_scale
        o_ref[0, pl.ds(r0, rows), :] = o.astype(BF16)

    if need_ctx:
        attend(0, n_ctx, n_ctx)

    def latent_tile(i, carry):
        attend(pl.multiple_of(n_ctx + i * q_tile, ROW_TILE), q_tile, total)
        return carry

    lax.fori_loop(0, (total - n_ctx) // q_tile, latent_tile, 0)


def diff_attention(proj, cos_t, sin_t, subln, lam, out_scale, n_ctx, need_ctx):
    b, t, _ = proj.shape
    hw = 2 * ATT_DIM
    per_block = BRANCH_W // hw
    q_tile = next(q for q in Q_TILES if (t - n_ctx) % q == 0)
    par = jnp.stack([lam, out_scale]).astype(F32)
    col = lambda block: (lambda bi, h: (bi, 0, block * per_block + h))
    whole = lambda bi, h: (0, 0)
    return pl.pallas_call(
        functools.partial(_diff_attn_kernel, n_ctx=n_ctx, need_ctx=need_ctx, q_tile=q_tile),
        out_shape=jax.ShapeDtypeStruct((b, t, ATT_V_W), BF16),
        grid=(b, ATT_HEADS),
        in_specs=[
            pl.BlockSpec(memory_space=pltpu.SMEM),
            pl.BlockSpec((1, t, hw), col(COL_Q)),
            pl.BlockSpec((1, t, hw), col(COL_K)),
            pl.BlockSpec((1, t, hw), col(COL_V)),
            pl.BlockSpec((t, hw), whole),
            pl.BlockSpec((t, hw), whole),
            pl.BlockSpec((1, hw), whole),
        ],
        out_specs=pl.BlockSpec((1, t, hw), lambda bi, h: (bi, 0, h)),
        scratch_shapes=[pltpu.VMEM((t, hw), BF16), pltpu.VMEM((t, hw), BF16)],
        compiler_params=_params("parallel", "parallel"),
        name="diff_attention",
    )(par, proj, proj, proj, cos_t, sin_t, subln.reshape(1, hw).astype(F32))


def _merge_kernel(x_ref, mod_ref, y0_ref, y1_ref, y2_ref, y3_ref, wg_ref, wb_ref, wo_ref, ln_ref, wr_ref, br_ref,
                  x1_ref, h2_ref, lg_ref):
    x = x_ref[0]
    mod = mod_ref[0, 0]
    h = (_normalize(x, 1e-6) * (1.0 + mod[1:2]) + mod[0:1]).astype(BF16)
    m = None
    for i, y_ref in enumerate((y0_ref, y1_ref, y2_ref, y3_ref)):
        gate = jax.nn.sigmoid(jnp.dot(h, wg_ref[:, i * D_MODEL:(i + 1) * D_MODEL], preferred_element_type=F32))
        term = gate * jnp.dot(y_ref[0], wb_ref[i], preferred_element_type=F32)
        m = term if m is None else m + term
    y = jnp.dot(m.astype(BF16), wo_ref[...], preferred_element_type=F32)
    x1 = _normalize(DN_ALPHA * x + mod[2:3] * y, 1e-5) * ln_ref[0:1] + ln_ref[1:2]
    x1_ref[0] = x1
    h2 = (_normalize(x1, 1e-6) * (1.0 + mod[4:5]) + mod[3:4]).astype(BF16)
    h2_ref[0] = h2
    lg_ref[0] = jnp.dot(h2, wr_ref[...], preferred_element_type=F32) + br_ref[...]


def merge(x, mods, ys, wg, wb, wo, ln, wr, br, first_tile):
    b, t, d = x.shape
    tm = ROW_TILE
    row = lambda bi, i: (bi, i + first_tile, 0)
    const2 = lambda bi, i: (0, 0)
    y_spec = pl.BlockSpec((1, tm, BRANCH_W), row)
    return pl.pallas_call(
        _merge_kernel,
        out_shape=(jax.ShapeDtypeStruct((b, t, d), F32), jax.ShapeDtypeStruct((b, t, d), BF16),
                   jax.ShapeDtypeStruct((b, t, LANES), F32)),
        grid=(b, t // tm - first_tile),
        in_specs=[
            pl.BlockSpec((1, tm, d), row),
            pl.BlockSpec((1, 1, 6, d), lambda bi, i: (bi, jnp.minimum(i + first_tile, 1), 0, 0)),
            y_spec, y_spec, y_spec, y_spec,
            pl.BlockSpec((d, N_BRANCH * d), const2),
            pl.BlockSpec((N_BRANCH, BRANCH_W, d), lambda bi, i: (0, 0, 0)),
            pl.BlockSpec((d, d), const2),
            pl.BlockSpec((2, d), const2),
            pl.BlockSpec((d, LANES), const2),
            pl.BlockSpec((1, LANES), const2),
        ],
        out_specs=(pl.BlockSpec((1, tm, d), row), pl.BlockSpec((1, tm, d), row), pl.BlockSpec((1, tm, LANES), row)),
        compiler_params=pltpu.CompilerParams(dimension_semantics=("parallel", "parallel"),
                                             vmem_limit_bytes=MERGE_VMEM_LIMIT),
        name="merge",
    )(x, mods, *ys, wg, wb, wo, ln, wr, br)


def _expert_kernel(be_ref, nb_ref, x_ref, wgu_ref, bgu_ref, wd_ref, bd_ref, *rest, first_block):
    o_ref, wgu_b, wd_b = rest[-3:]
    i = pl.program_id(0)
    blk = first_block + i

    @pl.when(blk < nb_ref[0])
    def _():
        @pl.when((i == 0) | (be_ref[blk] != be_ref[jnp.maximum(blk - 1, 0)]))
        def _():
            wgu_b[...] = wgu_ref[0, 0].astype(BF16)
            wd_b[...] = wd_ref[0, 0].astype(BF16)

        gu = jnp.dot(x_ref[...], wgu_b[...], preferred_element_type=F32) + bgu_ref[0, 0]
        gate = jnp.minimum(gu[:, :D_EXPERT], SWIGLU_LIMIT)
        up = jnp.clip(gu[:, D_EXPERT:], -SWIGLU_LIMIT, SWIGLU_LIMIT)
        glu = gate * jax.nn.sigmoid(gate * SWIGLU_ALPHA)
        act = ((up + 1.0) * glu).astype(BF16)
        y = jnp.dot(act, wd_b[...], preferred_element_type=F32) + bd_ref[0, 0]
        o_ref[...] = y.astype(BF16)


def expert_blocks(layer, block_e, n_used, xb, yb_prev, first_block, n_blocks_total, wgu, bgu, wd, bd):
    rows, d = xb.shape
    n_piece = rows // EXPERT_BLOCK

    def local(i, nb):
        return jnp.maximum(jnp.minimum(i, nb[0] - 1 - first_block), 0)

    expert = lambda i, be, nb: (layer, be[first_block + local(i, nb)], 0, 0)
    in_specs = [
        pl.BlockSpec((EXPERT_BLOCK, d), lambda i, be, nb: (local(i, nb), 0)),
        pl.BlockSpec((1, 1, d, 2 * D_EXPERT), expert),
        pl.BlockSpec((1, 1, 1, 2 * D_EXPERT), expert),
        pl.BlockSpec((1, 1, D_EXPERT, d), expert),
        pl.BlockSpec((1, 1, 1, d), expert),
    ]
    args = [block_e, n_used, xb, wgu, bgu[:, :, None, :], wd, bd[:, :, None, :]]
    aliases = {}
    if yb_prev is not None:
        in_specs.append(pl.BlockSpec(memory_space=pl.ANY))
        aliases = {len(args): 0}
        args.append(yb_prev)
    grid_spec = pltpu.PrefetchScalarGridSpec(
        num_scalar_prefetch=2,
        grid=(n_piece,),
        in_specs=in_specs,
        out_specs=pl.BlockSpec((EXPERT_BLOCK, d), lambda i, be, nb: (first_block + local(i, nb), 0)),
        scratch_shapes=[pltpu.VMEM((d, 2 * D_EXPERT), BF16), pltpu.VMEM((D_EXPERT, d), BF16)],
    )
    return pl.pallas_call(
        functools.partial(_expert_kernel, first_block=first_block),
        out_shape=jax.ShapeDtypeStruct((n_blocks_total * EXPERT_BLOCK, d), BF16),
        grid_spec=grid_spec,
        input_output_aliases=aliases,
        compiler_params=pltpu.CompilerParams(dimension_semantics=("arbitrary",), vmem_limit_bytes=EXPERT_VMEM_LIMIT),
        name="expert_blocks",
    )(*args)


ROUTE_DEST, ROUTE_PROB = 0, TOP_K
ROUTE_TILES = (1024, 512, 256)


def _route_kernel(lg_ref, out_ref, cnt_ref, run_ref, base_ref):
    phase = pl.program_id(0)
    i = pl.program_id(1)
    tm = lg_ref.shape[0]
    lane = lax.broadcasted_iota(jnp.int32, (1, LANES), 1).astype(F32)
    cur = jnp.where(lane < N_EXPERTS, lg_ref[...], -jnp.inf)
    vals, picks = [], []
    member = jnp.zeros((tm, LANES), F32)
    for _ in range(TOP_K):
        v = jnp.max(cur, -1, keepdims=True)
        pick = lane == jnp.min(jnp.where(cur == v, lane, float(LANES)), -1, keepdims=True)
        member = member + pick.astype(F32)
        cur = jnp.where(pick, -jnp.inf, cur)
        vals.append(v)
        picks.append(pick)
    tile_counts = jnp.sum(member, 0, keepdims=True)

    @pl.when(phase == 0)
    def _():
        @pl.when(i == 0)
        def _():
            cnt_ref[...] = jnp.zeros_like(cnt_ref)
        cnt_ref[...] = cnt_ref[...] + tile_counts

    @pl.when(phase == 1)
    def _():
        @pl.when(i == 0)
        def _():
            blocks = jnp.floor((cnt_ref[...] + (EXPERT_BLOCK - 1)) * (1.0 / EXPERT_BLOCK))
            r = lax.broadcasted_iota(jnp.int32, (LANES, LANES), 0)
            c = lax.broadcasted_iota(jnp.int32, (LANES, LANES), 1)
            before = _split_dot_lhs(jnp.broadcast_to(blocks, (SUBLANES, LANES)), (r < c).astype(BF16))
            base_ref[...] = before[0:1] * float(EXPERT_BLOCK)
            run_ref[...] = jnp.zeros_like(run_ref)

        r = lax.broadcasted_iota(jnp.int32, (tm, tm), 0)
        c = lax.broadcasted_iota(jnp.int32, (tm, tm), 1)
        earlier = jnp.dot((c < r).astype(BF16), member.astype(BF16), preferred_element_type=F32)
        slot = base_ref[...] + run_ref[...] + earlier
        denom = 1.0
        exps = [1.0]
        for k in range(1, TOP_K):
            exps.append(jnp.exp(vals[k] - vals[0]))
            denom = denom + exps[k]
        out_lane = lax.broadcasted_iota(jnp.int32, (1, LANES), 1)
        row = jnp.zeros((tm, LANES), F32)
        for k in range(TOP_K):
            dest = jnp.sum(jnp.where(picks[k], slot, 0.0), -1, keepdims=True)
            row = jnp.where(out_lane == ROUTE_DEST + k, dest, row)
            row = jnp.where(out_lane == ROUTE_PROB + k, exps[k] / denom, row)
        out_ref[...] = row
        run_ref[...] = run_ref[...] + tile_counts


def route(logits):
    n = logits.shape[0]
    tm = next(t for t in ROUTE_TILES if n % t == 0)
    return pl.pallas_call(
        _route_kernel,
        out_shape=(jax.ShapeDtypeStruct((n, LANES), F32), jax.ShapeDtypeStruct((1, LANES), F32)),
        grid=(2, n // tm),
        in_specs=[pl.BlockSpec((tm, LANES), lambda ph, i: (i, 0))],
        out_specs=(pl.BlockSpec((tm, LANES), lambda ph, i: (i * ph, 0)), pl.BlockSpec((1, LANES), lambda ph, i: (0, 0))),
        scratch_shapes=[pltpu.VMEM((1, LANES), F32), pltpu.VMEM((1, LANES), F32)],
        compiler_params=_params("arbitrary", "arbitrary"),
        name="route",
    )(logits)


def _combine_norm_kernel(x_ref, y_ref, rt_ref, mod_ref, ln_ref, o_ref):
    f = None
    for k in range(TOP_K):
        term = rt_ref[0, :, ROUTE_PROB + k:ROUTE_PROB + k + 1] * y_ref[k, 0].astype(F32)
        f = term if f is None else f + term
    x2 = DN_ALPHA * x_ref[0] + mod_ref[0, 0, 5:6] * f
    o_ref[0] = _normalize(x2, 1e-5) * ln_ref[0:1] + ln_ref[1:2]


def combine_norm(x1, ys, table, mods, ln, first_tile):
    b, t, d = x1.shape
    tm = ROW_TILE
    n_tiles = t // tm - first_tile
    return pl.pallas_call(
        _combine_norm_kernel,
        out_shape=jax.ShapeDtypeStruct((b, n_tiles * tm, d), F32),
        grid=(b, n_tiles),
        in_specs=[
            pl.BlockSpec((1, tm, d), lambda bi, i: (bi, i + first_tile, 0)),
            pl.BlockSpec((TOP_K, 1, tm, d), lambda bi, i: (0, bi, i, 0)),
            pl.BlockSpec((1, tm, LANES), lambda bi, i: (bi, i, 0)),
            pl.BlockSpec((1, 1, 6, d), lambda bi, i: (bi, jnp.minimum(i + first_tile, 1), 0, 0)),
            pl.BlockSpec((2, d), lambda bi, i: (0, 0)),
        ],
        out_specs=pl.BlockSpec((1, tm, d), lambda bi, i: (bi, i, 0)),
        compiler_params=_params("parallel", "parallel"),
        name="combine_norm",
    )(x1, ys, table, mods, ln)


def _proj_columns():
    o = IN_OFFS
    cols = list(range(o[0], o[7]))
    xbc, dt = o[7], o[8]
    for g in range(SSM_G):
        cols += [xbc + g * SSM_GW + i for i in range(SSM_GW)]
        cols += [xbc + SSM_W + g * SSM_N + i for i in range(SSM_N)]
        cols += [xbc + SSM_W + SSM_G * SSM_N + g * SSM_N + i for i in range(SSM_N)]
        cols += [dt + d * SSM_HEADS + g * SSM_K + k for d in range(2) for k in range(SSM_K)]
        cols += [-1] * (BRANCH_W - SSD_XBC - 2 * SSM_K)
    return np.asarray(cols, np.int32)


def _ssd_conv_columns():
    cols = []
    for g in range(SSM_G):
        cols += [g * SSM_GW + i for i in range(SSM_GW)]
        cols += [SSM_W + g * SSM_N + i for i in range(SSM_N)]
        cols += [SSM_W + SSM_G * SSM_N + g * SSM_N + i for i in range(SSM_N)]
    return np.asarray(cols, np.int32)


def _rope_tables(n_lat, n_ctx):
    rows = n_lat // GRID_W
    row = jnp.repeat(jnp.arange(rows, dtype=F32), GRID_W)
    col = jnp.tile(jnp.arange(GRID_W, dtype=F32), rows)
    inv = ROPE_BASE ** (-jnp.arange(ROPE_F, dtype=F32) / ROPE_F)
    ang = jnp.stack([row[:, None] * inv, col[:, None] * inv], axis=1)
    cos = jnp.broadcast_to(jnp.cos(ang)[:, None, :, None, :], (n_lat, 2, 2, 2, ROPE_F))
    sin = jnp.sin(ang)[:, None, :, None, :] * jnp.asarray([-1.0, 1.0], F32)[None, None, None, :, None]
    sin = jnp.broadcast_to(sin, (n_lat, 2, 2, 2, ROPE_F))
    hw = 2 * ATT_DIM
    cos = jnp.concatenate([jnp.ones((n_ctx, hw), F32), cos.reshape(n_lat, hw)], axis=0)
    sin = jnp.concatenate([jnp.zeros((n_ctx, hw), F32), sin.reshape(n_lat, hw)], axis=0)
    return cos, sin


def _block_diag(w):
    n, r, c = w.shape
    eye = jnp.eye(n, dtype=w.dtype)
    return (w[:, :, None, :] * eye[:, None, :, None]).reshape(n * r, n * c)


def _lane_pad(v, width=LANES):
    return jnp.pad(v, [(0, 0)] * (v.ndim - 1) + [(0, width - v.shape[-1])])


def _slot_tables(dest, counts):
    n = dest.shape[0]
    a = n * TOP_K
    n_blocks = -(-a // EXPERT_BLOCK) + N_EXPERTS
    pblocks = (counts + EXPERT_BLOCK - 1) // EXPERT_BLOCK
    pend_b = jnp.cumsum(pblocks)
    start = jnp.cumsum(counts) - counts
    blk = jnp.arange(n_blocks, dtype=jnp.int32)
    block_e = jnp.minimum(jnp.sum((blk[:, None] >= pend_b[None, :]).astype(jnp.int32), axis=1), N_EXPERTS - 1)
    n_used = pend_b[N_EXPERTS - 1:].astype(jnp.int32)
    tok = jnp.arange(a, dtype=jnp.int32) // TOP_K
    _, sorted_tok = lax.sort_key_val(dest.reshape(a), tok)
    slots = n_blocks * EXPERT_BLOCK
    max_pad = N_EXPERTS * EXPERT_BLOCK
    padded = jnp.concatenate([jnp.zeros((max_pad,), jnp.int32), sorted_tok, jnp.zeros((slots - a,), jnp.int32)])
    shift = (pend_b - pblocks) * EXPERT_BLOCK - start
    onehot = block_e[:, None] == jnp.arange(N_EXPERTS)[None, :]
    first_blk = jnp.sum(jnp.where(onehot, (pend_b - pblocks)[None, :], 0), axis=1)
    n_valid = jnp.sum(jnp.where(onehot, counts[None, :], 0), axis=1) - (blk - first_blk) * EXPERT_BLOCK
    valid = (blk < n_used[0])[:, None] & (jnp.arange(EXPERT_BLOCK)[None, :] < n_valid[:, None])
    slot_e = jnp.where(valid, block_e[:, None], N_EXPERTS).reshape(slots)
    slot_tok = jnp.zeros((slots,), jnp.int32)
    for e in range(N_EXPERTS):
        slot_tok = jnp.where(slot_e == e, lax.dynamic_slice(padded, (max_pad - shift[e],), (slots,)), slot_tok)
    return block_e, n_used, slot_tok


EXPERT_PIECES = 4


def kernel(x, c, ctx, c_ctx, w_ada, b_ada, w_in, rnn_conv_w, rnn_conv_b, rnn_wa, rnn_ba, rnn_wx, rnn_bx, rnn_lam, att_lambda, att_subln, pool_w, pool_b, pool_scale, ssm_conv_w, ssm_conv_b, ssm_dt_bias, ssm_a_log, ssm_d, ssm_norm, w_branch, w_out, ln1_g, ln1_b, ln2_g, ln2_b, router_w, router_b, w_gate_up, b_gate_up, w_down, b_down):
    bsz, n_lat, d = x.shape
    n_ctx = ctx.shape[1]
    total = n_ctx + n_lat
    assert n_ctx == ROW_TILE and n_lat % ROW_TILE == 0 and n_lat % GRID_W == 0
    ctx_tiles = n_ctx // ROW_TILE
    cos_t, sin_t = _rope_tables(n_lat, n_ctx)
    pcols = _proj_columns()
    w_main = jnp.where(pcols[None, None, :] >= 0, jnp.take(w_in, jnp.maximum(pcols, 0), axis=2), 0.0).astype(BF16)
    w_gates = w_in[:, :, IN_OFFS[9]:].astype(BF16)
    w_branch_b = w_branch.astype(BF16)
    w_out_b = w_out.astype(BF16)
    w_router = _lane_pad(router_w).astype(BF16)
    b_router = _lane_pad(router_b)[:, None, :]
    n_cg = RNN_W // LANES
    bpg = LANES // RNN_BLOCK
    gate_blocks = jnp.stack([rnn_wa[:, 0], rnn_wx[:, 0], rnn_wa[:, 1], rnn_wx[:, 1]], axis=1)
    gate_blocks = gate_blocks.reshape(DEPTH, 4, n_cg, bpg, RNN_BLOCK, RNN_BLOCK)
    rnn_wg = jax.vmap(jax.vmap(jax.vmap(_block_diag)))(gate_blocks)
    rnn_wg = rnn_wg.transpose(0, 2, 3, 1, 4).reshape(DEPTH, n_cg, LANES, 4 * LANES).astype(BF16)
    gate_bias = jnp.stack([rnn_ba[:, 0], rnn_bx[:, 0], rnn_ba[:, 1], rnn_bx[:, 1]], axis=1)
    rnn_bg = gate_bias.reshape(DEPTH, 4, n_cg, LANES).transpose(0, 2, 1, 3).reshape(DEPTH, n_cg, 1, 4 * LANES)
    rnn_sp = jax.nn.softplus(-rnn_lam).reshape(DEPTH, 2, n_cg, LANES).transpose(0, 2, 1, 3).reshape(DEPTH, n_cg, 1, 2 * LANES)
    ccols = _ssd_conv_columns()
    ssd_cw = jnp.take(ssm_conv_w, ccols, axis=2).reshape(DEPTH, CONV_W, SSM_G, SSD_XBC).transpose(0, 2, 1, 3)
    ssd_cb = jnp.take(ssm_conv_b, ccols, axis=1).reshape(DEPTH, SSM_G, 1, SSD_XBC)
    per_group = lambda v: v.reshape(DEPTH, 2, SSM_G, SSM_K).transpose(0, 2, 1, 3).reshape(DEPTH, SSM_G, 1, 2 * SSM_K)
    ssd_dtb = _lane_pad(per_group(ssm_dt_bias))
    ssd_alog = _lane_pad(per_group(ssm_a_log))
    ssd_dsk = jnp.repeat((ssm_d[:, 0] + ssm_d[:, 1]).reshape(DEPTH, SSM_G, 1, SSM_K), SSM_P, axis=-1)
    ssd_nw = ssm_norm.reshape(DEPTH, SSM_G, 1, SSM_GW)
    pool_wb = pool_w.astype(BF16)

    xs = jnp.concatenate([ctx, x], axis=1)
    out = None
    for li in range(DEPTH):
        need_ctx = li < DEPTH - 1
        first = 0 if need_ctx else ctx_tiles
        mod_l = jax.nn.silu(c) @ w_ada[li] + b_ada[li]
        mod_c = jnp.broadcast_to(jax.nn.silu(c_ctx) @ w_ada[li] + b_ada[li], mod_l.shape)
        mods = jnp.stack([mod_c, mod_l], axis=1).reshape(bsz, 2, 6, d)

        proj = in_proj(xs, mods, w_main[li], n_ctx)
        y_rnn = rglru(proj, rnn_conv_w[li], rnn_conv_b[li][None], rnn_wg[li], rnn_bg[li], rnn_sp[li], n_ctx)
        lam_init = 0.8 - 0.6 * math.exp(-0.3 * li)
        lv = att_lambda[li]
        lam = jnp.exp(jnp.sum(lv[0] * lv[1])) - jnp.exp(jnp.sum(lv[2] * lv[3])) + lam_init
        y_att = diff_attention(proj, cos_t, sin_t, att_subln[li], lam, jnp.asarray(1.0 - lam_init, F32), n_ctx, need_ctx)
        y_pool = pool(proj, pool_wb[li], pool_b[li][None], pool_scale[li][None], n_ctx)
        y_ssd = ssd(proj, ssd_cw[li], ssd_cb[li], ssd_dtb[li], ssd_alog[li], ssd_dsk[li], ssd_nw[li], n_ctx)
        ln1 = jnp.stack([ln1_g[li], ln1_b[li]])
        x1, h2, logits = merge(xs, mods, (y_rnn, y_att, y_pool, y_ssd), w_gates[li], w_branch_b[li], w_out_b[li], ln1,
                               w_router[li], b_router[li], first)

        n_rout = total - first * ROW_TILE
        table, counts = route(logits[:, first * ROW_TILE:].reshape(bsz * n_rout, LANES))
        dest = table[:, ROUTE_DEST:ROUTE_DEST + TOP_K].astype(jnp.int32)
        block_e, n_used, slot_tok = _slot_tables(dest, counts[0, :N_EXPERTS].astype(jnp.int32))
        slot_row = slot_tok + (slot_tok // n_rout + 1) * (first * ROW_TILE)
        h2_rows = h2.reshape(bsz * total, d)
        n_blocks = block_e.shape[0]
        pieces = EXPERT_PIECES if n_blocks % EXPERT_PIECES == 0 else 1
        piece_rows = n_blocks // pieces * EXPERT_BLOCK
        yb = None
        for p in range(pieces):
            xb = h2_rows[slot_row[p * piece_rows:(p + 1) * piece_rows]]
            yb = expert_blocks(li, block_e, n_used, xb, yb, p * (n_blocks // pieces), n_blocks,
                               w_gate_up, b_gate_up, w_down, b_down)
        ys = yb[dest.T.reshape(-1)].reshape(TOP_K, bsz, n_rout, d)
        ln2 = jnp.stack([ln2_g[li], ln2_b[li]])
        out = combine_norm(x1, ys, table.reshape(bsz, n_rout, LANES), mods, ln2, first)
        xs = out
    return out
```

```python
import functools
import math

import jax
import jax.numpy as jnp
import numpy as np
from jax import lax
from jax.experimental import pallas as pl
from jax.experimental.pallas import tpu as pltpu

D_MODEL = 1024
DEPTH = 4
GRID_W = 64
N_BRANCH = 4
BRANCH_W = D_MODEL // 2
RNN_W = BRANCH_W
RNN_BLOCK = 64
RG_C = 8.0
CONV_W = 4
ATT_DIM = 64
ATT_HEADS = BRANCH_W // (2 * ATT_DIM)
ATT_V_W = ATT_HEADS * 2 * ATT_DIM
ROPE_F = ATT_DIM // 4
ROPE_BASE = 10000.0
POOL_W = BRANCH_W
POOL_WINDOWS = (2, 4, 8, 16)
POOL_GROUPS = len(POOL_WINDOWS)
POOL_GW = POOL_W // POOL_GROUPS
SSM_W = BRANCH_W
SSM_P = 64
SSM_HEADS = SSM_W // SSM_P
SSM_G = 2
SSM_K = SSM_HEADS // SSM_G
SSM_N = 64
SSM_GW = SSM_W // SSM_G
SSM_CHUNK = 128
N_EXPERTS = 32
TOP_K = 4
D_EXPERT = D_MODEL
SWIGLU_LIMIT = 7.0
SWIGLU_ALPHA = 1.702
EXPERT_BLOCK = 256
DN_ALPHA = (2 * DEPTH) ** 0.25
IN_WIDTHS = (RNN_W, RNN_W, BRANCH_W, BRANCH_W, ATT_V_W, POOL_W, SSM_W, SSM_W + 2 * SSM_G * SSM_N, 2 * SSM_HEADS, N_BRANCH * D_MODEL)
IN_OFFS = tuple(int(v) for v in np.cumsum((0,) + IN_WIDTHS))

LANES = 128
SUBLANES = 8
VMEM_LIMIT = 48 * 1024 * 1024
ROW_TILE = 256
EXPERT_VMEM_LIMIT = 54 * 1024 * 1024
MERGE_VMEM_LIMIT = 56 * 1024 * 1024

COL_RX, COL_RG, COL_Q, COL_K, COL_V, COL_PU, COL_Z, COL_SSD = 0, 1, 2, 3, 4, 5, 6, 7
PROJ_W = (COL_SSD + SSM_G) * BRANCH_W
SSD_XBC = SSM_GW + 2 * SSM_N

BF16 = jnp.bfloat16
F32 = jnp.float32


def _params(*sem):
    return pltpu.CompilerParams(dimension_semantics=sem, vmem_limit_bytes=VMEM_LIMIT)


def _normalize(x, eps):
    mu = jnp.mean(x, -1, keepdims=True)
    xc = x - mu
    var = jnp.mean(xc * xc, -1, keepdims=True)
    return xc * lax.rsqrt(var + eps)


def _segment_valid(t, off, n_ctx, total):
    lo = jnp.where(t < n_ctx, 0, n_ctx)
    hi = jnp.where(t < n_ctx, n_ctx, total)
    return (t + off >= lo) & (t + off < hi)


def _shift_rows(u, off, t, n_ctx):
    total = u.shape[0]
    rolled = pltpu.roll(u, (-off) % total, 0)
    return jnp.where(_segment_valid(t, off, n_ctx, total), rolled, 0.0)


def _centred_conv(u, pad_ref, w_ref, b_ref, n_ctx):
    total, c = u.shape
    gap = SUBLANES
    zeros = jnp.zeros((gap, c), F32)
    pad_ref[0:gap, :] = zeros
    pad_ref[gap:gap + n_ctx, :] = u[0:n_ctx]
    pad_ref[gap + n_ctx:2 * gap + n_ctx, :] = zeros
    pad_ref[2 * gap + n_ctx:2 * gap + total, :] = u[n_ctx:total]
    pad_ref[2 * gap + total:3 * gap + total, :] = zeros

    def tap(off):
        return jnp.concatenate([pad_ref[pl.ds(gap + off, n_ctx), :],
                                pad_ref[pl.ds(2 * gap + n_ctx + off, total - n_ctx), :]], axis=0)

    left = CONV_W // 2
    out = b_ref[...] + tap(-left) * w_ref[0:1, :]
    for k in range(1, CONV_W):
        out = out + (u if k == left else tap(k - left)) * w_ref[k:k + 1, :]
    return out


IN_PROJ_ROWS = (384, 256)


def _in_proj_kernel(x_ref, mod_ref, w_ref, o_ref, *, n_ctx):
    tm = x_ref.shape[1]
    row = pl.program_id(1) * tm + lax.broadcasted_iota(jnp.int32, (tm, 1), 0)
    is_ctx = row < n_ctx
    shift = jnp.where(is_ctx, mod_ref[0, 0, 0:1, :], mod_ref[0, 1, 0:1, :])
    scale = jnp.where(is_ctx, mod_ref[0, 0, 1:2, :], mod_ref[0, 1, 1:2, :])
    h = (_normalize(x_ref[0], 1e-6) * (1.0 + scale) + shift).astype(BF16)
    o_ref[0] = jnp.dot(h, w_ref[...], preferred_element_type=F32)


def in_proj(x, mods, w, n_ctx):
    b, t, d = x.shape
    n = w.shape[1]
    tm = next(r for r in IN_PROJ_ROWS if t % r == 0)
    return pl.pallas_call(
        functools.partial(_in_proj_kernel, n_ctx=n_ctx),
        out_shape=jax.ShapeDtypeStruct((b, t, n), F32),
        grid=(b, t // tm),
        in_specs=[
            pl.BlockSpec((1, tm, d), lambda bi, i: (bi, i, 0)),
            pl.BlockSpec((1, 2, 6, d), lambda bi, i: (bi, 0, 0, 0)),
            pl.BlockSpec((d, n), lambda bi, i: (0, 0)),
        ],
        out_specs=pl.BlockSpec((1, tm, n), lambda bi, i: (bi, i, 0)),
        compiler_params=_params("parallel", "parallel"),
        name="in_proj",
    )(x, mods, w)


SCAN_ROWS = 64


def _tile_scan(a, b, reverse):
    rows = a.shape[0]
    sub = lax.broadcasted_iota(jnp.int32, (rows, 1), 0) % SUBLANES
    s = 1
    while s < SUBLANES:
        shift = (rows - s) if reverse else s
        keep = (sub + s < SUBLANES) if reverse else (sub >= s)
        a_sh = pltpu.roll(a, shift, 0)
        b_sh = pltpu.roll(b, shift, 0)
        b = jnp.where(keep, a * b_sh + b, b)
        a = jnp.where(keep, a * a_sh, a)
        s *= 2
    return a, b


def _rglru_kernel(rx_ref, rg_ref, cw_ref, cb_ref, wg_ref, bg_ref, sp_ref, o_ref, a_ref, b_ref, y_ref, pad_ref, *, n_ctx):
    total = rx_ref.shape[1]
    cw = rx_ref.shape[2]
    u = _centred_conv(rx_ref[0], pad_ref, cw_ref, cb_ref, n_ctx)
    g = jnp.dot(u.astype(BF16), wg_ref[0], preferred_element_type=F32) + bg_ref[0]
    n_tiles = SCAN_ROWS // SUBLANES

    for d in range(2):
        r = jax.nn.sigmoid(g[:, (2 * d) * cw:(2 * d + 1) * cw])
        i = jax.nn.sigmoid(g[:, (2 * d + 1) * cw:(2 * d + 2) * cw])
        log_a = -RG_C * r * sp_ref[0, :, d * cw:(d + 1) * cw]
        a = jnp.exp(log_a)
        a_ref[d] = a
        b_ref[d] = jnp.sqrt((1.0 - a) * (1.0 + a)) * (i * u)

    def scan_rows(r0, h, d):
        reverse = d == 1
        a, b = _tile_scan(a_ref[d, pl.ds(r0, SCAN_ROWS), :], b_ref[d, pl.ds(r0, SCAN_ROWS), :], reverse)
        outs = [None] * n_tiles
        for j in (range(n_tiles - 1, -1, -1) if reverse else range(n_tiles)):
            sl = slice(j * SUBLANES, (j + 1) * SUBLANES)
            hj = b[sl] + a[sl] * h
            h = hj[0:1] if reverse else hj[SUBLANES - 1:SUBLANES]
            outs[j] = hj
        y_ref[d, pl.ds(r0, SCAN_ROWS), :] = jnp.concatenate(outs, axis=0)
        return h

    def step(it, hs, lo, n_steps):
        h_f = scan_rows(pl.multiple_of(lo + it * SCAN_ROWS, SCAN_ROWS), hs[0], 0)
        h_b = scan_rows(pl.multiple_of(lo + (n_steps - 1 - it) * SCAN_ROWS, SCAN_ROWS), hs[1], 1)
        return h_f, h_b

    h0 = jnp.zeros((1, cw), F32)
    hs = lax.fori_loop(0, n_ctx // SCAN_ROWS, functools.partial(step, lo=0, n_steps=n_ctx // SCAN_ROWS), (h0, h0))
    n_lat_steps = (total - n_ctx) // SCAN_ROWS
    lax.fori_loop(0, n_lat_steps, functools.partial(step, lo=n_ctx, n_steps=n_lat_steps), hs)

    o_ref[0] = (jax.nn.gelu(rg_ref[0]) * (y_ref[0] + y_ref[1])).astype(BF16)


def rglru(proj, conv_w, conv_b, wg, bg, sp, n_ctx):
    b, t, _ = proj.shape
    cw = LANES
    n_cg = RNN_W // cw
    per_block = BRANCH_W // cw
    return pl.pallas_call(
        functools.partial(_rglru_kernel, n_ctx=n_ctx),
        out_shape=jax.ShapeDtypeStruct((b, t, RNN_W), BF16),
        grid=(b, n_cg),
        in_specs=[
            pl.BlockSpec((1, t, cw), lambda bi, c: (bi, 0, COL_RX * per_block + c)),
            pl.BlockSpec((1, t, cw), lambda bi, c: (bi, 0, COL_RG * per_block + c)),
            pl.BlockSpec((CONV_W, cw), lambda bi, c: (0, c)),
            pl.BlockSpec((1, cw), lambda bi, c: (0, c)),
            pl.BlockSpec((1, cw, 4 * cw), lambda bi, c: (c, 0, 0)),
            pl.BlockSpec((1, 1, 4 * cw), lambda bi, c: (c, 0, 0)),
            pl.BlockSpec((1, 1, 2 * cw), lambda bi, c: (c, 0, 0)),
        ],
        out_specs=pl.BlockSpec((1, t, cw), lambda bi, c: (bi, 0, c)),
        scratch_shapes=[pltpu.VMEM((2, t, cw), F32), pltpu.VMEM((2, t, cw), F32), pltpu.VMEM((2, t, cw), F32),
                        pltpu.VMEM((t + 3 * SUBLANES, cw), F32)],
        compiler_params=_params("parallel", "parallel"),
        name="rglru",
    )(proj, proj, conv_w, conv_b, wg, bg, sp)


def _pool_kernel(u_ref, w_ref, b_ref, s_ref, o_ref, *, n_ctx):
    total = u_ref.shape[1]
    t = lax.broadcasted_iota(jnp.int32, (total, 1), 0)
    for gi, win in enumerate(POOL_WINDOWS):
        cols = slice(gi * POOL_GW, (gi + 1) * POOL_GW)
        u = u_ref[0, :, cols]
        acc = u
        cnt = jnp.ones((total, 1), F32)
        for off in range(-(win // 2), win - win // 2):
            if off == 0:
                continue
            acc = acc + _shift_rows(u, off, t, n_ctx)
            cnt = cnt + _segment_valid(t, off, n_ctx, total).astype(F32)
        mix = acc / cnt - u
        y = jnp.dot(mix.astype(BF16), w_ref[gi], preferred_element_type=F32) + b_ref[:, cols]
        o_ref[0, :, cols] = (y * s_ref[:, cols]).astype(BF16)


def pool(proj, w, bias, scale, n_ctx):
    b, t, _ = proj.shape
    return pl.pallas_call(
        functools.partial(_pool_kernel, n_ctx=n_ctx),
        out_shape=jax.ShapeDtypeStruct((b, t, POOL_W), BF16),
        grid=(b,),
        in_specs=[
            pl.BlockSpec((1, t, POOL_W), lambda bi: (bi, 0, COL_PU)),
            pl.BlockSpec((POOL_GROUPS, POOL_GW, POOL_GW), lambda bi: (0, 0, 0)),
            pl.BlockSpec((1, POOL_W), lambda bi: (0, 0)),
            pl.BlockSpec((1, POOL_W), lambda bi: (0, 0)),
        ],
        out_specs=pl.BlockSpec((1, t, POOL_W), lambda bi: (bi, 0, 0)),
        compiler_params=_params("parallel"),
        name="pool",
    )(proj, w, bias, scale)


def _bf16_pieces(v):
    hi = v.astype(BF16)
    r1 = v - hi.astype(F32)
    mid = r1.astype(BF16)
    lo = (r1 - mid.astype(F32)).astype(BF16)
    return hi, mid, lo


def _split_dot(tri, v):
    return sum(jnp.dot(tri, p, preferred_element_type=F32) for p in _bf16_pieces(v))


def _split_dot_lhs(v, tri):
    return sum(jnp.dot(p, tri, preferred_element_type=F32) for p in _bf16_pieces(v))


def _per_head(cols, width):
    head = lax.broadcasted_iota(jnp.int32, (1, width), 1) // SSM_P
    out = cols[SSM_K - 1]
    for k in range(SSM_K - 2, -1, -1):
        out = jnp.where(head == k, cols[k], out)
    return out


def _ssd_kernel(blk_ref, z_ref, cw_ref, cb_ref, dtb_ref, alog_ref, dsk_ref, nw_ref, o_ref,
                u_ref, dt_ref, adt_ref, tr_ref, y_ref, st_ref, pad_ref, csc_ref, csr_ref, *, n_ctx):
    total = blk_ref.shape[1]
    ck = SSM_CHUNK
    n_chunks = total // ck
    ctx_chunks = n_ctx // ck
    u_ref[...] = jax.nn.silu(_centred_conv(blk_ref[0, :, 0:SSD_XBC], pad_ref, cw_ref.at[0], cb_ref.at[0], n_ctx))
    dt = jax.nn.softplus(blk_ref[0, :, SSD_XBC:SSD_XBC + LANES] + dtb_ref[0])
    dt_ref[...] = dt
    adt_ref[...] = dt * (-jnp.exp(alog_ref[0]))
    for c in range(n_chunks):
        rows = slice(c * ck, (c + 1) * ck)
        tr_ref[c, 0:ck, :] = u_ref[rows, SSM_GW:SSM_GW + 2 * SSM_N].T
        tr_ref[c, ck:2 * ck, :] = adt_ref[rows, :].T

    ri = lax.broadcasted_iota(jnp.int32, (ck, ck), 0)
    ci = lax.broadcasted_iota(jnp.int32, (ck, ck), 1)
    lower = (ci <= ri)
    lower_b = lower.astype(BF16)
    upper_b = (ci >= ri).astype(BF16)
    lane = lax.broadcasted_iota(jnp.int32, (1, LANES), 1)
    head_w = lax.broadcasted_iota(jnp.int32, (1, SSM_GW), 1) // SSM_P
    srow = lax.broadcasted_iota(jnp.int32, (ck, 1), 0)

    a_cat = jnp.concatenate([adt_ref[c * ck:(c + 1) * ck, :] for c in range(n_chunks)], axis=1)
    a_rows = jnp.concatenate([tr_ref[c, ck:ck + SUBLANES, :] for c in range(n_chunks)], axis=0)
    for d, (tri_col, tri_row) in enumerate(((lower_b, upper_b), (upper_b, lower_b))):
        cols_all = _split_dot(tri_col, a_cat)
        for c in range(n_chunks):
            csc_ref[d, c] = cols_all[:, c * ck:(c + 1) * ck]
        csr_ref[d] = _split_dot_lhs(a_rows, tri_row)

    def chunk(c, d):
        reverse = d == 1
        causal = (ci >= ri) if reverse else lower
        r0 = pl.multiple_of(c * ck, ck)
        xs = u_ref[pl.ds(r0, ck), 0:SSM_GW]
        bc = u_ref[pl.ds(r0, ck), SSM_GW:SSM_GW + 2 * SSM_N]
        tr = tr_ref[c, 0:ck, :]
        c_lo = jnp.where(lane < SSM_N, pltpu.roll(bc, SSM_N, 1), 0.0).astype(BF16)
        cb = jnp.dot(c_lo, tr.astype(BF16), preferred_element_type=F32)
        cs_col = csc_ref[d, c]
        cs_row = csr_ref[d, pl.ds(pl.multiple_of(c * SUBLANES, SUBLANES), SUBLANES), :]
        edge = ck - 1 if not reverse else 0
        dtc = dt_ref[pl.ds(r0, ck), :]
        cols, tots, dts = [], [], []
        y = jnp.zeros((ck, SSM_GW), F32)
        for k in range(SSM_K):
            j = d * SSM_K + k
            cols.append(cs_col[:, j:j + 1])
            tots.append(cs_col[edge:edge + 1, j:j + 1])
            dts.append(dtc[:, j:j + 1])
        xdt = xs * _per_head(dts, SSM_GW)
        for k in range(SSM_K):
            j = d * SSM_K + k
            seg = cols[k] - cs_row[j:j + 1, :]
            m = (cb * jnp.where(causal, jnp.exp(seg), 0.0)).astype(BF16)
            y = y + jnp.dot(m, jnp.where(head_w == k, xdt, 0.0).astype(BF16), preferred_element_type=F32)
        col_w = _per_head(cols, SSM_GW)
        tot_w = _per_head(tots, SSM_GW)
        state = st_ref[d]
        y = y + jnp.dot(c_lo, state.astype(BF16), preferred_element_type=F32) * jnp.exp(col_w)
        xd = (xdt * jnp.exp(tot_w - col_w)).astype(BF16)
        new = jnp.exp(tot_w) * state + jnp.dot(tr.astype(BF16), xd, preferred_element_type=F32)
        st_ref[d] = jnp.where(srow < SSM_N, new, 0.0)
        y_ref[d, pl.ds(r0, ck), :] = y

    def both(it, carry, lo, n_steps):
        chunk(lo + it, 0)
        chunk(lo + n_steps - 1 - it, 1)
        return carry

    st_ref[...] = jnp.zeros_like(st_ref)
    lax.fori_loop(0, ctx_chunks, functools.partial(both, lo=0, n_steps=ctx_chunks), 0)
    lat_chunks = n_chunks - ctx_chunks
    lax.fori_loop(0, lat_chunks, functools.partial(both, lo=ctx_chunks, n_steps=lat_chunks), 0,
                  unroll=2 if lat_chunks % 2 == 0 else 1)

    y = y_ref[0] + y_ref[1] + dsk_ref[0] * u_ref[:, 0:SSM_GW]
    g = y * jax.nn.silu(z_ref[0])
    g = g * lax.rsqrt(jnp.mean(g * g, -1, keepdims=True) + 1e-5)
    o_ref[0] = (g * nw_ref[0]).astype(BF16)


def ssd(proj, conv_w, conv_b, dt_bias, a_log, d_skip, norm_w, n_ctx):
    b, t, _ = proj.shape
    n_chunks = t // SSM_CHUNK
    z_per = BRANCH_W // SSM_GW
    return pl.pallas_call(
        functools.partial(_ssd_kernel, n_ctx=n_ctx),
        out_shape=jax.ShapeDtypeStruct((b, t, SSM_W), BF16),
        grid=(b, SSM_G),
        in_specs=[
            pl.BlockSpec((1, t, BRANCH_W), lambda bi, g: (bi, 0, COL_SSD + g)),
            pl.BlockSpec((1, t, SSM_GW), lambda bi, g: (bi, 0, COL_Z * z_per + g)),
            pl.BlockSpec((1, CONV_W, SSD_XBC), lambda bi, g: (g, 0, 0)),
            pl.BlockSpec((1, 1, SSD_XBC), lambda bi, g: (g, 0, 0)),
            pl.BlockSpec((1, 1, LANES), lambda bi, g: (g, 0, 0)),
            pl.BlockSpec((1, 1, LANES), lambda bi, g: (g, 0, 0)),
            pl.BlockSpec((1, 1, SSM_GW), lambda bi, g: (g, 0, 0)),
            pl.BlockSpec((1, 1, SSM_GW), lambda bi, g: (g, 0, 0)),
        ],
        out_specs=pl.BlockSpec((1, t, SSM_GW), lambda bi, g: (bi, 0, g)),
        scratch_shapes=[
            pltpu.VMEM((t, SSD_XBC), F32),
            pltpu.VMEM((t, LANES), F32),
            pltpu.VMEM((t, LANES), F32),
            pltpu.VMEM((n_chunks, 2 * SSM_CHUNK, SSM_CHUNK), F32),
            pltpu.VMEM((2, t, SSM_GW), F32),
            pltpu.VMEM((2, SSM_CHUNK, SSM_GW), F32),
            pltpu.VMEM((t + 3 * SUBLANES, SSD_XBC), F32),
            pltpu.VMEM((2, n_chunks, SSM_CHUNK, LANES), F32),
            pltpu.VMEM((2, n_chunks * SUBLANES, SSM_CHUNK), F32),
        ],
        compiler_params=_params("parallel", "parallel"),
        name="ssd",
    )(proj, proj, conv_w, conv_b, dt_bias, a_log, d_skip, norm_w)


def _rope(x, cos, sin):
    lane = lax.broadcasted_iota(jnp.int32, (1, x.shape[1]), 1)
    partner = jnp.where(lane % (2 * ROPE_F) < ROPE_F, pltpu.roll(x, x.shape[1] - ROPE_F, 1), pltpu.roll(x, ROPE_F, 1))
    return x * cos + partner * sin


Q_TILES = (1024, 512, 256)


def _diff_attn_kernel(par_ref, q_ref, k_ref, v_ref, cos_ref, sin_ref, g_ref, o_ref, kb_ref, vb_ref,
                      *, n_ctx, need_ctx, q_tile):
    lam = par_ref[0]
    out_scale = par_ref[1]
    total = k_ref.shape[1]
    kb_ref[...] = _rope(k_ref[0], cos_ref[...], sin_ref[...]).astype(BF16)
    vb_ref[...] = v_ref[0].astype(BF16)
    dims = (((1,), (1,)), ((), ()))

    def attend(r0, rows, n_keys):
        q = _rope(q_ref[0, pl.ds(r0, rows), :], cos_ref[pl.ds(r0, rows), :], sin_ref[pl.ds(r0, rows), :])
        q = (q * (ATT_DIM ** -0.5 * math.log2(math.e))).astype(BF16)
        lane = lax.broadcasted_iota(jnp.int32, q.shape, 1)
        k = kb_ref[0:n_keys, :]
        v = vb_ref[0:n_keys, :]

        def softmax_v(qh):
            s = lax.dot_general(qh, k, dims, preferred_element_type=F32)
            e = jnp.exp2(s - jnp.max(s, -1, keepdims=True))
            return jnp.dot(e.astype(BF16), v, preferred_element_type=F32) / jnp.sum(e, -1, keepdims=True)

        o = (softmax_v(jnp.where(lane < ATT_DIM, q, jnp.zeros_like(q)))
             - lam * softmax_v(jnp.where(lane >= ATT_DIM, q, jnp.zeros_like(q))))
        o = o * lax.rsqrt(jnp.mean(o * o, -1, keepdims=True) + 1e-5) * g_ref[...] * out_scale
        o_ref[0, pl.ds(r0, rows), :] = o.astype(BF16)

    if need_ctx:
        attend(0, n_ctx, n_ctx)

    def latent_tile(i, carry):
        attend(pl.multiple_of(n_ctx + i * q_tile, ROW_TILE), q_tile, total)
        return carry

    lax.fori_loop(0, (total - n_ctx) // q_tile, latent_tile, 0)


def diff_attention(proj, cos_t, sin_t, subln, lam, out_scale, n_ctx, need_ctx):
    b, t, _ = proj.shape
    hw = 2 * ATT_DIM
    per_block = BRANCH_W // hw
    q_tile = next(q for q in Q_TILES if (t - n_ctx) % q == 0)
    par = jnp.stack([lam, out_scale]).astype(F32)
    col = lambda block: (lambda bi, h: (bi, 0, block * per_block + h))
    whole = lambda bi, h: (0, 0)
    return pl.pallas_call(
        functools.partial(_diff_attn_kernel, n_ctx=n_ctx, need_ctx=need_ctx, q_tile=q_tile),
        out_shape=jax.ShapeDtypeStruct((b, t, ATT_V_W), BF16),
        grid=(b, ATT_HEADS),
        in_specs=[
            pl.BlockSpec(memory_space=pltpu.SMEM),
            pl.BlockSpec((1, t, hw), col(COL_Q)),
            pl.BlockSpec((1, t, hw), col(COL_K)),
            pl.BlockSpec((1, t, hw), col(COL_V)),
            pl.BlockSpec((t, hw), whole),
            pl.BlockSpec((t, hw), whole),
            pl.BlockSpec((1, hw), whole),
        ],
        out_specs=pl.BlockSpec((1, t, hw), lambda bi, h: (bi, 0, h)),
        scratch_shapes=[pltpu.VMEM((t, hw), BF16), pltpu.VMEM((t, hw), BF16)],
        compiler_params=_params("parallel", "parallel"),
        name="diff_attention",
    )(par, proj, proj, proj, cos_t, sin_t, subln.reshape(1, hw).astype(F32))


def _merge_kernel(x_ref, mod_ref, y0_ref, y1_ref, y2_ref, y3_ref, wg_ref, wb_ref, wo_ref, ln_ref, wr_ref, br_ref,
                  x1_ref, h2_ref, lg_ref):
    x = x_ref[0]
    mod = mod_ref[0, 0]
    h = (_normalize(x, 1e-6) * (1.0 + mod[1:2]) + mod[0:1]).astype(BF16)
    m = None
    for i, y_ref in enumerate((y0_ref, y1_ref, y2_ref, y3_ref)):
        gate = jax.nn.sigmoid(jnp.dot(h, wg_ref[:, i * D_MODEL:(i + 1) * D_MODEL], preferred_element_type=F32))
        term = gate * jnp.dot(y_ref[0], wb_ref[i], preferred_element_type=F32)
        m = term if m is None else m + term
    y = jnp.dot(m.astype(BF16), wo_ref[...], preferred_element_type=F32)
    x1 = _normalize(DN_ALPHA * x + mod[2:3] * y, 1e-5) * ln_ref[0:1] + ln_ref[1:2]
    x1_ref[0] = x1
    h2 = (_normalize(x1, 1e-6) * (1.0 + mod[4:5]) + mod[3:4]).astype(BF16)
    h2_ref[0] = h2
    lg_ref[0] = jnp.dot(h2, wr_ref[...], preferred_element_type=F32) + br_ref[...]


def merge(x, mods, ys, wg, wb, wo, ln, wr, br, first_tile):
    b, t, d = x.shape
    tm = ROW_TILE
    row = lambda bi, i: (bi, i + first_tile, 0)
    const2 = lambda bi, i: (0, 0)
    y_spec = pl.BlockSpec((1, tm, BRANCH_W), row)
    return pl.pallas_call(
        _merge_kernel,
        out_shape=(jax.ShapeDtypeStruct((b, t, d), F32), jax.ShapeDtypeStruct((b, t, d), BF16),
                   jax.ShapeDtypeStruct((b, t, LANES), F32)),
        grid=(b, t // tm - first_tile),
        in_specs=[
            pl.BlockSpec((1, tm, d), row),
            pl.BlockSpec((1, 1, 6, d), lambda bi, i: (bi, jnp.minimum(i + first_tile, 1), 0, 0)),
            y_spec, y_spec, y_spec, y_spec,
            pl.BlockSpec((d, N_BRANCH * d), const2),
            pl.BlockSpec((N_BRANCH, BRANCH_W, d), lambda bi, i: (0, 0, 0)),
            pl.BlockSpec((d, d), const2),
            pl.BlockSpec((2, d), const2),
            pl.BlockSpec((d, LANES), const2),
            pl.BlockSpec((1, LANES), const2),
        ],
        out_specs=(pl.BlockSpec((1, tm, d), row), pl.BlockSpec((1, tm, d), row), pl.BlockSpec((1, tm, LANES), row)),
        compiler_params=pltpu.CompilerParams(dimension_semantics=("parallel", "parallel"),
                                             vmem_limit_bytes=MERGE_VMEM_LIMIT),
        name="merge",
    )(x, mods, *ys, wg, wb, wo, ln, wr, br)


def _expert_kernel(be_ref, nb_ref, x_ref, wgu_ref, bgu_ref, wd_ref, bd_ref, *rest, first_block):
    o_ref, wgu_b, wd_b = rest[-3:]
    i = pl.program_id(0)
    blk = first_block + i

    @pl.when(blk < nb_ref[0])
    def _():
        @pl.when((i == 0) | (be_ref[blk] != be_ref[jnp.maximum(blk - 1, 0)]))
        def _():
            wgu_b[...] = wgu_ref[0, 0].astype(BF16)
            wd_b[...] = wd_ref[0, 0].astype(BF16)

        gu = jnp.dot(x_ref[...], wgu_b[...], preferred_element_type=F32) + bgu_ref[0, 0]
        gate = jnp.minimum(gu[:, :D_EXPERT], SWIGLU_LIMIT)
        up = jnp.clip(gu[:, D_EXPERT:], -SWIGLU_LIMIT, SWIGLU_LIMIT)
        glu = gate * jax.nn.sigmoid(gate * SWIGLU_ALPHA)
        act = ((up + 1.0) * glu).astype(BF16)
        y = jnp.dot(act, wd_b[...], preferred_element_type=F32) + bd_ref[0, 0]
        o_ref[...] = y.astype(BF16)


def expert_blocks(layer, block_e, n_used, xb, yb_prev, first_block, n_blocks_total, wgu, bgu, wd, bd):
    rows, d = xb.shape
    n_piece = rows // EXPERT_BLOCK

    def local(i, nb):
        return jnp.maximum(jnp.minimum(i, nb[0] - 1 - first_block), 0)

    expert = lambda i, be, nb: (layer, be[first_block + local(i, nb)], 0, 0)
    in_specs = [
        pl.BlockSpec((EXPERT_BLOCK, d), lambda i, be, nb: (local(i, nb), 0)),
        pl.BlockSpec((1, 1, d, 2 * D_EXPERT), expert),
        pl.BlockSpec((1, 1, 1, 2 * D_EXPERT), expert),
        pl.BlockSpec((1, 1, D_EXPERT, d), expert),
        pl.BlockSpec((1, 1, 1, d), expert),
    ]
    args = [block_e, n_used, xb, wgu, bgu[:, :, None, :], wd, bd[:, :, None, :]]
    aliases = {}
    if yb_prev is not None:
        in_specs.append(pl.BlockSpec(memory_space=pl.ANY))
        aliases = {len(args): 0}
        args.append(yb_prev)
    grid_spec = pltpu.PrefetchScalarGridSpec(
        num_scalar_prefetch=2,
        grid=(n_piece,),
        in_specs=in_specs,
        out_specs=pl.BlockSpec((EXPERT_BLOCK, d), lambda i, be, nb: (first_block + local(i, nb), 0)),
        scratch_shapes=[pltpu.VMEM((d, 2 * D_EXPERT), BF16), pltpu.VMEM((D_EXPERT, d), BF16)],
    )
    return pl.pallas_call(
        functools.partial(_expert_kernel, first_block=first_block),
        out_shape=jax.ShapeDtypeStruct((n_blocks_total * EXPERT_BLOCK, d), BF16),
        grid_spec=grid_spec,
        input_output_aliases=aliases,
        compiler_params=pltpu.CompilerParams(dimension_semantics=("arbitrary",), vmem_limit_bytes=EXPERT_VMEM_LIMIT),
        name="expert_blocks",
    )(*args)


ROUTE_DEST, ROUTE_PROB = 0, TOP_K
ROUTE_TILES = (1024, 512, 256)


def _route_kernel(lg_ref, out_ref, cnt_ref, run_ref, base_ref):
    phase = pl.program_id(0)
    i = pl.program_id(1)
    tm = lg_ref.shape[0]
    lane = lax.broadcasted_iota(jnp.int32, (1, LANES), 1).astype(F32)
    cur = jnp.where(lane < N_EXPERTS, lg_ref[...], -jnp.inf)
    vals, picks = [], []
    member = jnp.zeros((tm, LANES), F32)
    for _ in range(TOP_K):
        v = jnp.max(cur, -1, keepdims=True)
        pick = lane == jnp.min(jnp.where(cur == v, lane, float(LANES)), -1, keepdims=True)
        member = member + pick.astype(F32)
        cur = jnp.where(pick, -jnp.inf, cur)
        vals.append(v)
        picks.append(pick)
    tile_counts = jnp.sum(member, 0, keepdims=True)

    @pl.when(phase == 0)
    def _():
        @pl.when(i == 0)
        def _():
            cnt_ref[...] = jnp.zeros_like(cnt_ref)
        cnt_ref[...] = cnt_ref[...] + tile_counts

    @pl.when(phase == 1)
    def _():
        @pl.when(i == 0)
        def _():
            blocks = jnp.floor((cnt_ref[...] + (EXPERT_BLOCK - 1)) * (1.0 / EXPERT_BLOCK))
            r = lax.broadcasted_iota(jnp.int32, (LANES, LANES), 0)
            c = lax.broadcasted_iota(jnp.int32, (LANES, LANES), 1)
            before = _split_dot_lhs(jnp.broadcast_to(blocks, (SUBLANES, LANES)), (r < c).astype(BF16))
            base_ref[...] = before[0:1] * float(EXPERT_BLOCK)
            run_ref[...] = jnp.zeros_like(run_ref)

        r = lax.broadcasted_iota(jnp.int32, (tm, tm), 0)
        c = lax.broadcasted_iota(jnp.int32, (tm, tm), 1)
        earlier = jnp.dot((c < r).astype(BF16), member.astype(BF16), preferred_element_type=F32)
        slot = base_ref[...] + run_ref[...] + earlier
        denom = 1.0
        exps = [1.0]
        for k in range(1, TOP_K):
            exps.append(jnp.exp(vals[k] - vals[0]))
            denom = denom + exps[k]
        out_lane = lax.broadcasted_iota(jnp.int32, (1, LANES), 1)
        row = jnp.zeros((tm, LANES), F32)
        for k in range(TOP_K):
            dest = jnp.sum(jnp.where(picks[k], slot, 0.0), -1, keepdims=True)
            row = jnp.where(out_lane == ROUTE_DEST + k, dest, row)
            row = jnp.where(out_lane == ROUTE_PROB + k, exps[k] / denom, row)
        out_ref[...] = row
        run_ref[...] = run_ref[...] + tile_counts


def route(logits):
    n = logits.shape[0]
    tm = next(t for t in ROUTE_TILES if n % t == 0)
    return pl.pallas_call(
        _route_kernel,
        out_shape=(jax.ShapeDtypeStruct((n, LANES), F32), jax.ShapeDtypeStruct((1, LANES), F32)),
        grid=(2, n // tm),
        in_specs=[pl.BlockSpec((tm, LANES), lambda ph, i: (i, 0))],
        out_specs=(pl.BlockSpec((tm, LANES), lambda ph, i: (i * ph, 0)), pl.BlockSpec((1, LANES), lambda ph, i: (0, 0))),
        scratch_shapes=[pltpu.VMEM((1, LANES), F32), pltpu.VMEM((1, LANES), F32)],
        compiler_params=_params("arbitrary", "arbitrary"),
        name="route",
    )(logits)


def _combine_norm_kernel(x_ref, y_ref, rt_ref, mod_ref, ln_ref, *rest):
    o_ref = rest[-1]
    f = None
    for k in range(TOP_K):
        term = rt_ref[0, :, ROUTE_PROB + k:ROUTE_PROB + k + 1] * y_ref[k, 0].astype(F32)
        f = term if f is None else f + term
    x2 = DN_ALPHA * x_ref[0] + mod_ref[0, 0, 5:6] * f
    o_ref[0] = _normalize(x2, 1e-5) * ln_ref[0:1] + ln_ref[1:2]


def combine_norm(x1, ys, table, mods, ln, first_tile, batch_lo, prev):
    b, t, d = x1.shape
    nb = ys.shape[1]
    tm = ROW_TILE
    n_tiles = t // tm - first_tile
    in_specs = [
        pl.BlockSpec((1, tm, d), lambda bi, i: (batch_lo + bi, i + first_tile, 0)),
        pl.BlockSpec((TOP_K, 1, tm, d), lambda bi, i: (0, bi, i, 0)),
        pl.BlockSpec((1, tm, LANES), lambda bi, i: (batch_lo + bi, i, 0)),
        pl.BlockSpec((1, 1, 6, d), lambda bi, i: (batch_lo + bi, jnp.minimum(i + first_tile, 1), 0, 0)),
        pl.BlockSpec((2, d), lambda bi, i: (0, 0)),
    ]
    args = [x1, ys, table, mods, ln]
    aliases = {}
    if prev is not None:
        in_specs.append(pl.BlockSpec(memory_space=pl.ANY))
        aliases = {len(args): 0}
        args.append(prev)
    return pl.pallas_call(
        _combine_norm_kernel,
        out_shape=jax.ShapeDtypeStruct((b, n_tiles * tm, d), F32),
        grid=(nb, n_tiles),
        in_specs=in_specs,
        out_specs=pl.BlockSpec((1, tm, d), lambda bi, i: (batch_lo + bi, i, 0)),
        input_output_aliases=aliases,
        compiler_params=_params("parallel", "parallel"),
        name="combine_norm",
    )(*args)


def _proj_columns():
    o = IN_OFFS
    cols = list(range(o[0], o[7]))
    xbc, dt = o[7], o[8]
    for g in range(SSM_G):
        cols += [xbc + g * SSM_GW + i for i in range(SSM_GW)]
        cols += [xbc + SSM_W + g * SSM_N + i for i in range(SSM_N)]
        cols += [xbc + SSM_W + SSM_G * SSM_N + g * SSM_N + i for i in range(SSM_N)]
        cols += [dt + d * SSM_HEADS + g * SSM_K + k for d in range(2) for k in range(SSM_K)]
        cols += [-1] * (BRANCH_W - SSD_XBC - 2 * SSM_K)
    return np.asarray(cols, np.int32)


def _ssd_conv_columns():
    cols = []
    for g in range(SSM_G):
        cols += [g * SSM_GW + i for i in range(SSM_GW)]
        cols += [SSM_W + g * SSM_N + i for i in range(SSM_N)]
        cols += [SSM_W + SSM_G * SSM_N + g * SSM_N + i for i in range(SSM_N)]
    return np.asarray(cols, np.int32)


def _rope_tables(n_lat, n_ctx):
    rows = n_lat // GRID_W
    row = jnp.repeat(jnp.arange(rows, dtype=F32), GRID_W)
    col = jnp.tile(jnp.arange(GRID_W, dtype=F32), rows)
    inv = ROPE_BASE ** (-jnp.arange(ROPE_F, dtype=F32) / ROPE_F)
    ang = jnp.stack([row[:, None] * inv, col[:, None] * inv], axis=1)
    cos = jnp.broadcast_to(jnp.cos(ang)[:, None, :, None, :], (n_lat, 2, 2, 2, ROPE_F))
    sin = jnp.sin(ang)[:, None, :, None, :] * jnp.asarray([-1.0, 1.0], F32)[None, None, None, :, None]
    sin = jnp.broadcast_to(sin, (n_lat, 2, 2, 2, ROPE_F))
    hw = 2 * ATT_DIM
    cos = jnp.concatenate([jnp.ones((n_ctx, hw), F32), cos.reshape(n_lat, hw)], axis=0)
    sin = jnp.concatenate([jnp.zeros((n_ctx, hw), F32), sin.reshape(n_lat, hw)], axis=0)
    return cos, sin


def _block_diag(w):
    n, r, c = w.shape
    eye = jnp.eye(n, dtype=w.dtype)
    return (w[:, :, None, :] * eye[:, None, :, None]).reshape(n * r, n * c)


def _lane_pad(v, width=LANES):
    return jnp.pad(v, [(0, 0)] * (v.ndim - 1) + [(0, width - v.shape[-1])])


def _slot_tables(dest, counts):
    n = dest.shape[0]
    a = n * TOP_K
    n_blocks = -(-a // EXPERT_BLOCK) + N_EXPERTS
    pblocks = (counts + EXPERT_BLOCK - 1) // EXPERT_BLOCK
    pend_b = jnp.cumsum(pblocks)
    start = jnp.cumsum(counts) - counts
    blk = jnp.arange(n_blocks, dtype=jnp.int32)
    block_e = jnp.minimum(jnp.sum((blk[:, None] >= pend_b[None, :]).astype(jnp.int32), axis=1), N_EXPERTS - 1)
    n_used = pend_b[N_EXPERTS - 1:].astype(jnp.int32)
    tok = jnp.arange(a, dtype=jnp.int32) // TOP_K
    _, sorted_tok = lax.sort_key_val(dest.reshape(a), tok)
    slots = n_blocks * EXPERT_BLOCK
    max_pad = N_EXPERTS * EXPERT_BLOCK
    padded = jnp.concatenate([jnp.zeros((max_pad,), jnp.int32), sorted_tok, jnp.zeros((slots - a,), jnp.int32)])
    shift = (pend_b - pblocks) * EXPERT_BLOCK - start
    onehot = block_e[:, None] == jnp.arange(N_EXPERTS)[None, :]
    first_blk = jnp.sum(jnp.where(onehot, (pend_b - pblocks)[None, :], 0), axis=1)
    n_valid = jnp.sum(jnp.where(onehot, counts[None, :], 0), axis=1) - (blk - first_blk) * EXPERT_BLOCK
    valid = (blk < n_used[0])[:, None] & (jnp.arange(EXPERT_BLOCK)[None, :] < n_valid[:, None])
    slot_e = jnp.where(valid, block_e[:, None], N_EXPERTS).reshape(slots)
    slot_tok = jnp.zeros((slots,), jnp.int32)
    for e in range(N_EXPERTS):
        slot_tok = jnp.where(slot_e == e, lax.dynamic_slice(padded, (max_pad - shift[e],), (slots,)), slot_tok)
    return block_e, n_used, slot_tok


EXPERT_PIECES = 4
COMBINE_PARTS = 4


def kernel(x, c, ctx, c_ctx, w_ada, b_ada, w_in, rnn_conv_w, rnn_conv_b, rnn_wa, rnn_ba, rnn_wx, rnn_bx, rnn_lam, att_lambda, att_subln, pool_w, pool_b, pool_scale, ssm_conv_w, ssm_conv_b, ssm_dt_bias, ssm_a_log, ssm_d, ssm_norm, w_branch, w_out, ln1_g, ln1_b, ln2_g, ln2_b, router_w, router_b, w_gate_up, b_gate_up, w_down, b_down):
    bsz, n_lat, d = x.shape
    n_ctx = ctx.shape[1]
    total = n_ctx + n_lat
    assert n_ctx == ROW_TILE and n_lat % ROW_TILE == 0 and n_lat % GRID_W == 0
    ctx_tiles = n_ctx // ROW_TILE
    cos_t, sin_t = _rope_tables(n_lat, n_ctx)
    pcols = _proj_columns()
    w_main = jnp.where(pcols[None, None, :] >= 0, jnp.take(w_in, jnp.maximum(pcols, 0), axis=2), 0.0).astype(BF16)
    w_gates = w_in[:, :, IN_OFFS[9]:].astype(BF16)
    w_branch_b = w_branch.astype(BF16)
    w_out_b = w_out.astype(BF16)
    w_router = _lane_pad(router_w).astype(BF16)
    b_router = _lane_pad(router_b)[:, None, :]
    n_cg = RNN_W // LANES
    bpg = LANES // RNN_BLOCK
    gate_blocks = jnp.stack([rnn_wa[:, 0], rnn_wx[:, 0], rnn_wa[:, 1], rnn_wx[:, 1]], axis=1)
    gate_blocks = gate_blocks.reshape(DEPTH, 4, n_cg, bpg, RNN_BLOCK, RNN_BLOCK)
    rnn_wg = jax.vmap(jax.vmap(jax.vmap(_block_diag)))(gate_blocks)
    rnn_wg = rnn_wg.transpose(0, 2, 3, 1, 4).reshape(DEPTH, n_cg, LANES, 4 * LANES).astype(BF16)
    gate_bias = jnp.stack([rnn_ba[:, 0], rnn_bx[:, 0], rnn_ba[:, 1], rnn_bx[:, 1]], axis=1)
    rnn_bg = gate_bias.reshape(DEPTH, 4, n_cg, LANES).transpose(0, 2, 1, 3).reshape(DEPTH, n_cg, 1, 4 * LANES)
    rnn_sp = jax.nn.softplus(-rnn_lam).reshape(DEPTH, 2, n_cg, LANES).transpose(0, 2, 1, 3).reshape(DEPTH, n_cg, 1, 2 * LANES)
    ccols = _ssd_conv_columns()
    ssd_cw = jnp.take(ssm_conv_w, ccols, axis=2).reshape(DEPTH, CONV_W, SSM_G, SSD_XBC).transpose(0, 2, 1, 3)
    ssd_cb = jnp.take(ssm_conv_b, ccols, axis=1).reshape(DEPTH, SSM_G, 1, SSD_XBC)
    per_group = lambda v: v.reshape(DEPTH, 2, SSM_G, SSM_K).transpose(0, 2, 1, 3).reshape(DEPTH, SSM_G, 1, 2 * SSM_K)
    ssd_dtb = _lane_pad(per_group(ssm_dt_bias))
    ssd_alog = _lane_pad(per_group(ssm_a_log))
    ssd_dsk = jnp.repeat((ssm_d[:, 0] + ssm_d[:, 1]).reshape(DEPTH, SSM_G, 1, SSM_K), SSM_P, axis=-1)
    ssd_nw = ssm_norm.reshape(DEPTH, SSM_G, 1, SSM_GW)
    pool_wb = pool_w.astype(BF16)

    xs = jnp.concatenate([ctx, x], axis=1)
    out = None
    for li in range(DEPTH):
        need_ctx = li < DEPTH - 1
        first = 0 if need_ctx else ctx_tiles
        mod_l = jax.nn.silu(c) @ w_ada[li] + b_ada[li]
        mod_c = jnp.broadcast_to(jax.nn.silu(c_ctx) @ w_ada[li] + b_ada[li], mod_l.shape)
        mods = jnp.stack([mod_c, mod_l], axis=1).reshape(bsz, 2, 6, d)

        proj = in_proj(xs, mods, w_main[li], n_ctx)
        y_rnn = rglru(proj, rnn_conv_w[li], rnn_conv_b[li][None], rnn_wg[li], rnn_bg[li], rnn_sp[li], n_ctx)
        lam_init = 0.8 - 0.6 * math.exp(-0.3 * li)
        lv = att_lambda[li]
        lam = jnp.exp(jnp.sum(lv[0] * lv[1])) - jnp.exp(jnp.sum(lv[2] * lv[3])) + lam_init
        y_att = diff_attention(proj, cos_t, sin_t, att_subln[li], lam, jnp.asarray(1.0 - lam_init, F32), n_ctx, need_ctx)
        y_pool = pool(proj, pool_wb[li], pool_b[li][None], pool_scale[li][None], n_ctx)
        y_ssd = ssd(proj, ssd_cw[li], ssd_cb[li], ssd_dtb[li], ssd_alog[li], ssd_dsk[li], ssd_nw[li], n_ctx)
        ln1 = jnp.stack([ln1_g[li], ln1_b[li]])
        x1, h2, logits = merge(xs, mods, (y_rnn, y_att, y_pool, y_ssd), w_gates[li], w_branch_b[li], w_out_b[li], ln1,
                               w_router[li], b_router[li], first)

        n_rout = total - first * ROW_TILE
        table, counts = route(logits[:, first * ROW_TILE:].reshape(bsz * n_rout, LANES))
        dest = table[:, ROUTE_DEST:ROUTE_DEST + TOP_K].astype(jnp.int32)
        block_e, n_used, slot_tok = _slot_tables(dest, counts[0, :N_EXPERTS].astype(jnp.int32))
        slot_row = slot_tok + (slot_tok // n_rout + 1) * (first * ROW_TILE)
        h2_rows = h2.reshape(bsz * total, d)
        n_blocks = block_e.shape[0]
        pieces = EXPERT_PIECES if n_blocks % EXPERT_PIECES == 0 else 1
        piece_rows = n_blocks // pieces * EXPERT_BLOCK
        yb = None
        for p in range(pieces):
            xb = h2_rows[slot_row[p * piece_rows:(p + 1) * piece_rows]]
            yb = expert_blocks(li, block_e, n_used, xb, yb, p * (n_blocks // pieces), n_blocks,
                               w_gate_up, b_gate_up, w_down, b_down)
        ln2 = jnp.stack([ln2_g[li], ln2_b[li]])
        table_b = table.reshape(bsz, n_rout, LANES)
        dest_b = dest.reshape(bsz, n_rout, TOP_K)
        parts = COMBINE_PARTS if bsz % COMBINE_PARTS == 0 else 1
        nb = bsz // parts
        out = None
        for p in range(parts):
            idx = jnp.transpose(dest_b[p * nb:(p + 1) * nb], (2, 0, 1)).reshape(-1)
            ys = yb[idx].reshape(TOP_K, nb, n_rout, d)
            out = combine_norm(x1, ys, table_b, mods, ln2, first, p * nb, out)
        xs = out
    return out
```
